```python
import math
import jax, jax.numpy as jnp
from jax import lax
import numpy as np

D_MODEL = 1024
BATCH = 32
SEQ = 2048
DEPTH = 1

D_MIX = D_MODEL
D_HYENA = D_MIX // 2
D_ATTN = D_MIX - D_HYENA
HEAD_DIM = 64
N_Q_HEADS = D_ATTN // HEAD_DIM
N_KV_HEADS = N_Q_HEADS // 4
GQA_GROUP = N_Q_HEADS // N_KV_HEADS
WINDOW = 128
BLOCK = 128
ROPE_THETA = 10000.0
HYENA_ORDER = 2
SHORT_CONV = 3
FILTER_EMB = 33
FILTER_HIDDEN = 64
N_DIRS = 2
DECAY_TARGET = 1e-2
FAST_DECAY_PCT = 0.3
SLOW_DECAY_PCT = 1.5
EPS = 1e-6

N_HY_PROJ = (HYENA_ORDER + 1) * D_HYENA
D_KV = N_KV_HEADS * HEAD_DIM
D_IN = N_HY_PROJ + D_HYENA + D_ATTN + 2 * D_KV + D_ATTN

kernel_name = "hymba_hyena_swa_hybrid_encoder"

F32 = jnp.float32


def rms_norm(x, g):
    xf = x.astype(F32)
    y = xf * lax.rsqrt(jnp.mean(xf * xf, axis=-1, keepdims=True) + EPS)
    return (y * g.astype(F32)).astype(x.dtype)


def centred_short_conv(u, w, b):
    L = u.shape[1]
    half = SHORT_CONV // 2
    up = jnp.pad(u, ((0, 0), (half, half), (0, 0)))
    out = b
    for j in range(SHORT_CONV):
        out = out + up[:, j:j + L] * w[j]
    return out


def hyena_filters(L, w1, b1, w2, b2, w3, b3, w4, sin_freq):
    t = jnp.linspace(0.0, 1.0, L, dtype=F32)[:, None]
    bands = (FILTER_EMB - 1) // 2
    f = jnp.linspace(1e-4, bands - 1, bands, dtype=F32)[None, :]
    w = 2.0 * math.pi * jnp.arange(L, dtype=F32)[:, None] / L
    z = jnp.concatenate([t, jnp.cos(f * w), -jnp.sin(f * w)], axis=-1)
    fr = sin_freq.astype(F32)
    h = jnp.sin(fr * (z @ w1.astype(F32) + b1.astype(F32)))
    h = jnp.sin(fr * (h @ w2.astype(F32) + b2.astype(F32)))
    h = jnp.sin(fr * (h @ w3.astype(F32) + b3.astype(F32)))
    h = (h @ w4.astype(F32)).reshape(L, HYENA_ORDER, N_DIRS, D_HYENA)
    max_decay = math.log(DECAY_TARGET) / FAST_DECAY_PCT
    min_decay = math.log(DECAY_TARGET) / SLOW_DECAY_PCT
    deltas = jnp.linspace(min_decay, max_decay, D_HYENA, dtype=F32)
    decay = jnp.exp(-t * jnp.abs(deltas)[None, :])
    return h * decay[:, None, None, :]


def two_sided_spectrum(h):
    L = h.shape[0]
    zero = jnp.zeros((1,) + h.shape[1:2] + h.shape[3:], F32)
    k = jnp.concatenate([h[:, :, 0], zero, h[1:, :, 1][::-1]], axis=0)
    return jnp.fft.rfft(k, n=2 * L, axis=0)


def long_conv(u, k_f, bias):
    L = u.shape[1]
    y = jnp.fft.irfft(jnp.fft.rfft(u, n=2 * L, axis=1) * k_f[None], n=2 * L, axis=1)[:, :L]
    return y + u * bias


def rope(x, pos):
    half = HEAD_DIM // 2
    inv = ROPE_THETA ** (-jnp.arange(half, dtype=F32) / half)
    ang = pos[:, None] * inv[None, :]
    cos = jnp.cos(ang)[None, :, None, :]
    sin = jnp.sin(ang)[None, :, None, :]
    xf = x.astype(F32)
    x1, x2 = xf[..., :half], xf[..., half:]
    return jnp.concatenate([x1 * cos - x2 * sin, x2 * cos + x1 * sin], axis=-1).astype(x.dtype)


def windowed_sink_attention(q, k, v, sink):
    B, S = q.shape[0], q.shape[1]
    nb = S // BLOCK
    span = BLOCK + 2 * WINDOW
    kp = jnp.pad(k, ((0, 0), (WINDOW, WINDOW), (0, 0), (0, 0)))
    vp = jnp.pad(v, ((0, 0), (WINDOW, WINDOW), (0, 0), (0, 0)))
    qb = q.reshape(B, nb, BLOCK, N_KV_HEADS, GQA_GROUP, HEAD_DIM).transpose(1, 0, 2, 3, 4, 5)
    scale = HEAD_DIM ** -0.5
    sink32 = sink.astype(F32)[None, :, :, None, None]

    def block(args):
        qi, i = args
        start = i * BLOCK
        kw = lax.dynamic_slice_in_dim(kp, start, span, axis=1)
        vw = lax.dynamic_slice_in_dim(vp, start, span, axis=1)
        s = jnp.einsum('bqkgd,bskd->bkgqs', qi.astype(F32), kw.astype(F32)) * scale
        qpos = start + jnp.arange(BLOCK)
        kpos = start - WINDOW + jnp.arange(span)
        valid = ((jnp.abs(kpos[None, :] - qpos[:, None]) <= WINDOW)
                 & (kpos >= 0)[None, :] & (kpos < S)[None, :])
        s = jnp.where(valid, s, -jnp.inf)
        m = jnp.maximum(jnp.max(s, axis=-1, keepdims=True), sink32)
        p = jnp.exp(s - m)
        denom = jnp.sum(p, axis=-1, keepdims=True) + jnp.exp(sink32 - m)
        o = jnp.einsum('bkgqs,bskd->bqkgd', p / denom, vw.astype(F32))
        return o.astype(q.dtype)

    out = lax.map(block, (qb, jnp.arange(nb)))
    return out.transpose(1, 0, 2, 3, 4, 5).reshape(B, S, N_Q_HEADS * HEAD_DIM)


def setup_inputs(seed: int = 0) -> dict:
    key = jax.random.key(seed)
    ks = jax.random.split(key, 24)
    nrm = lambda k, shape, s: jax.random.normal(k, shape, F32) * s
    return {
        'x': nrm(ks[0], (BATCH, SEQ, D_MODEL), 1.0),
        'norm_g': 1.0 + nrm(ks[1], (DEPTH, D_MODEL), 0.02),
        'w_in': nrm(ks[2], (DEPTH, D_MODEL, D_IN), D_MODEL ** -0.5),
        'conv_w': nrm(ks[3], (DEPTH, SHORT_CONV, N_HY_PROJ), SHORT_CONV ** -0.5),
        'conv_b': nrm(ks[4], (DEPTH, N_HY_PROJ), 0.01),
        'filt_w1': nrm(ks[5], (DEPTH, FILTER_EMB, FILTER_HIDDEN), FILTER_EMB ** -0.5),
        'filt_b1': nrm(ks[6], (DEPTH, FILTER_HIDDEN), 0.1),
        'filt_w2': nrm(ks[7], (DEPTH, FILTER_HIDDEN, FILTER_HIDDEN), FILTER_HIDDEN ** -0.5),
        'filt_b2': nrm(ks[8], (DEPTH, FILTER_HIDDEN), 0.1),
        'filt_w3': nrm(ks[9], (DEPTH, FILTER_HIDDEN, FILTER_HIDDEN), FILTER_HIDDEN ** -0.5),
        'filt_b3': nrm(ks[10], (DEPTH, FILTER_HIDDEN), 0.1),
        'filt_w4': nrm(ks[11], (DEPTH, FILTER_HIDDEN, HYENA_ORDER * N_DIRS * D_HYENA), FILTER_HIDDEN ** -0.5),
        'filt_sin_freq': 1.0 + nrm(ks[12], (DEPTH, FILTER_HIDDEN), 0.02),
        'hyena_bias': nrm(ks[13], (DEPTH, HYENA_ORDER, D_HYENA), 1.0),
        'q_norm_g': 1.0 + nrm(ks[14], (DEPTH, HEAD_DIM), 0.02),
        'k_norm_g': 1.0 + nrm(ks[15], (DEPTH, HEAD_DIM), 0.02),
        'attn_sink': nrm(ks[16], (DEPTH, N_Q_HEADS), 0.5),
        'hy_out_norm_g': 1.0 + nrm(ks[17], (DEPTH, D_HYENA), 0.02),
        'attn_out_norm_g': 1.0 + nrm(ks[18], (DEPTH, D_ATTN), 0.02),
        'w_out': nrm(ks[19], (DEPTH, D_MIX, D_MODEL), D_MIX ** -0.5),
    }


def reference(x, norm_g, w_in, conv_w, conv_b, filt_w1, filt_b1, filt_w2, filt_b2,
              filt_w3, filt_b3, filt_w4, filt_sin_freq, hyena_bias, q_norm_g, k_norm_g,
              attn_sink, hy_out_norm_g, attn_out_norm_g, w_out):
    B, S = x.shape[0], x.shape[1]
    pos = jnp.arange(S, dtype=F32)
    for l in range(DEPTH):
        h = rms_norm(x, norm_g[l])
        proj = h @ w_in[l]
        o0 = N_HY_PROJ
        o1 = o0 + D_HYENA
        o2 = o1 + D_ATTN
        o3 = o2 + D_KV
        o4 = o3 + D_KV
        hy_in, g_h = proj[..., :o0], proj[..., o0:o1]
        q, k, v = proj[..., o1:o2], proj[..., o2:o3], proj[..., o3:o4]
        g_a = proj[..., o4:]

        u = centred_short_conv(hy_in, conv_w[l], conv_b[l]).astype(F32)
        hv, hx1, hx2 = jnp.split(u, HYENA_ORDER + 1, axis=-1)
        filt = hyena_filters(S, filt_w1[l], filt_b1[l], filt_w2[l], filt_b2[l],
                             filt_w3[l], filt_b3[l], filt_w4[l], filt_sin_freq[l])
        k_f = two_sided_spectrum(filt)
        hb = hyena_bias[l].astype(F32)
        z = hx1 * long_conv(hv, k_f[:, 0], hb[0])
        z = hx2 * long_conv(z, k_f[:, 1], hb[1])
        y_h = z.astype(x.dtype)

        q = rms_norm(q.reshape(B, S, N_Q_HEADS, HEAD_DIM), q_norm_g[l])
        k = rms_norm(k.reshape(B, S, N_KV_HEADS, HEAD_DIM), k_norm_g[l])
        q = rope(q, pos).reshape(B, S, N_KV_HEADS, GQA_GROUP, HEAD_DIM)
        k = rope(k, pos)
        v = v.reshape(B, S, N_KV_HEADS, HEAD_DIM)
        y_a = windowed_sink_attention(q, k, v, attn_sink[l].reshape(N_KV_HEADS, GQA_GROUP))

        y = jnp.concatenate([rms_norm(y_h, hy_out_norm_g[l]) * jax.nn.silu(g_h),
                             rms_norm(y_a, attn_out_norm_g[l]) * jax.nn.silu(g_a)], axis=-1)
        x = x + y @ w_out[l]
    return x
```

```python
import functools
import math

import jax
import jax.numpy as jnp
from jax import lax
from jax.experimental import pallas as pl
from jax.experimental.pallas import tpu as pltpu

F32 = jnp.float32
BF16 = jnp.bfloat16
HI = lax.Precision.HIGHEST

D_MODEL = 1024
SEQ = 2048
D_HYENA = 512
D_ATTN = 512
HEAD_DIM = 64
N_Q_HEADS = 8
N_KV_HEADS = 2
WINDOW = 128
ROPE_THETA = 10000.0
FILTER_HIDDEN = 64
N_BANDS = 16
DECAY_TARGET = 1e-2
FAST_DECAY_PCT = 0.3
SLOW_DECAY_PCT = 1.5
EPS = 1e-6

LANES = 128
SUBLANES = 8
TBLK = 256
N_TBLK = SEQ // TBLK
N_SHIFT = 2 * N_TBLK - 1
N_HY_T = 4 * D_HYENA
N_HY_TILES = N_HY_T // SUBLANES
HY_TILES = D_HYENA // SUBLANES
N_ATT_COLS = D_ATTN + 2 * HEAD_DIM * N_KV_HEADS + D_ATTN
D_KV = HEAD_DIM * N_KV_HEADS

FILT_LPAD = 128
N_WIN = 520
WIN_PER_STEP = 8
TOK_TILE = 512
VMEM_LIMIT = 56 * 1024 * 1024


def _nt_dot(a, b, precision=None):
    return lax.dot_general(a, b, (((1,), (1,)), ((), ())), preferred_element_type=F32,
                           precision=precision)


def _filter_kernel(w1t_ref, w1c_ref, w1s_ref, b1_ref, w2_ref, b2_ref, w3_ref, b3_ref,
                   w4t_ref, fr_ref, o_ref):
    step = pl.program_id(0)
    npos = WIN_PER_STEP * LANES
    centre = FILT_LPAD + SEQ

    def tap_offset(idx):
        return 8 * WIN_PER_STEP * step + 8 * (idx >> 7) + (idx & (LANES - 1)) - centre

    off_c = tap_offset(lax.broadcasted_iota(jnp.int32, (npos, 1), 0))
    off_r = tap_offset(lax.broadcasted_iota(jnp.int32, (1, npos), 1))
    idx_c = jnp.minimum(jnp.abs(off_c), SEQ - 1).astype(F32)
    idx_r = jnp.minimum(jnp.abs(off_r), SEQ - 1).astype(F32)

    t_c = idx_c / (SEQ - 1)
    w_c = (2.0 * math.pi) * idx_c / SEQ
    band = lax.broadcasted_iota(jnp.int32, (1, N_BANDS), 1).astype(F32)
    freqs = 1e-4 + band * ((N_BANDS - 1 - 1e-4) / (N_BANDS - 1))
    ang = w_c * freqs
    fr = fr_ref[...]
    pre = (t_c * w1t_ref[...]
           + jnp.dot(jnp.cos(ang), w1c_ref[...], preferred_element_type=F32, precision=HI)
           - jnp.dot(jnp.sin(ang), w1s_ref[...], preferred_element_type=F32, precision=HI)
           + b1_ref[...])
    h = jnp.sin(fr * pre)
    h = jnp.sin(fr * (jnp.dot(h, w2_ref[...], preferred_element_type=F32, precision=HI) + b2_ref[...]))
    h = jnp.sin(fr * (jnp.dot(h, w3_ref[...], preferred_element_type=F32, precision=HI) + b3_ref[...]))
    taps = _nt_dot(w4t_ref[...], h, precision=HI)

    max_decay = math.log(DECAY_TARGET) / FAST_DECAY_PCT
    min_decay = math.log(DECAY_TARGET) / SLOW_DECAY_PCT
    chan = lax.broadcasted_iota(jnp.int32, (D_HYENA, 1), 0).astype(F32)
    deltas = min_decay + chan * ((max_decay - min_decay) / (D_HYENA - 1))
    decay = jnp.exp(-(idx_r / (SEQ - 1)) * jnp.abs(deltas))
    decay = jnp.where(jnp.abs(off_r) <= SEQ - 1, decay, 0.0)
    backward = off_r < 0
    for order in range(2):
        fwd = taps[(2 * order) * D_HYENA:(2 * order + 1) * D_HYENA]
        bwd = taps[(2 * order + 1) * D_HYENA:(2 * order + 2) * D_HYENA]
        val = jnp.where(backward, bwd, fwd) * decay
        for w in range(WIN_PER_STEP):
            o_ref[order * HY_TILES:(order + 1) * HY_TILES, w, :, :] = (
                val[:, w * LANES:(w + 1) * LANES].reshape(HY_TILES, SUBLANES, LANES))


def _filter_windows(w1, b1, w2, b2, w3, b3, w4, sin_freq):
    full = lambda shape: pl.BlockSpec(shape, lambda i: (0,) * len(shape))
    args = (w1[0:1], w1[1:1 + N_BANDS], w1[1 + N_BANDS:], b1[None], w2, b2[None], w3, b3[None],
            w4.T, sin_freq[None])
    out = pl.pallas_call(
        _filter_kernel,
        grid=(N_WIN // WIN_PER_STEP,),
        in_specs=[full(a.shape) for a in args],
        out_specs=pl.BlockSpec((2 * HY_TILES, WIN_PER_STEP, SUBLANES, LANES), lambda i: (0, i, 0, 0)),
        out_shape=jax.ShapeDtypeStruct((2 * HY_TILES, N_WIN, SUBLANES, LANES), F32),
        compiler_params=pltpu.CompilerParams(dimension_semantics=("arbitrary",),
                                             vmem_limit_bytes=VMEM_LIMIT),
        name="filter_windows",
    )(*args)
    return out.reshape(2 * HY_TILES, N_WIN * SUBLANES, LANES)


def _in_proj_kernel(x_ref, g_ref, wht_ref, wa_ref, seg_ref, cos_ref, sin_ref, qg_ref, kg_ref,
                    hy_ref, q_ref, k_ref, v_ref, ga_ref):
    x = x_ref[0]
    ms = jnp.mean(x * x, axis=-1, keepdims=True)
    h = (x * lax.rsqrt(ms + EPS) * g_ref[...]).astype(BF16)

    ctile_blk = 256
    for cb in range(N_HY_T // ctile_blk):
        res = _nt_dot(wht_ref[cb * ctile_blk:(cb + 1) * ctile_blk, :], h)
        for kk in range(TOK_TILE // LANES):
            blk, half = kk // 2, kk % 2
            hy_ref[cb * 32:(cb + 1) * 32, half, blk, 0, :, :] = (
                res[:, kk * LANES:(kk + 1) * LANES].reshape(32, SUBLANES, LANES))

    att = jnp.dot(h, wa_ref[...], preferred_element_type=F32)
    cos = cos_ref[...]
    sin = sin_ref[...]
    lane = lax.broadcasted_iota(jnp.int32, (TOK_TILE, LANES), 1)
    first_half = (lane % HEAD_DIM) < (HEAD_DIM // 2)

    def norm_rope(t, gain, seg):
        ms_h = jnp.dot((t * t).astype(BF16), seg, preferred_element_type=F32)
        tn = t * lax.rsqrt(ms_h + EPS) * gain
        outs = []
        for j in range(t.shape[1] // LANES):
            c = tn[:, j * LANES:(j + 1) * LANES]
            swapped = jnp.where(first_half, pltpu.roll(c, LANES - HEAD_DIM // 2, 1),
                                pltpu.roll(c, HEAD_DIM // 2, 1))
            outs.append(c * cos + swapped * sin)
        return outs[0] if len(outs) == 1 else jnp.concatenate(outs, axis=1)

    q = norm_rope(att[:, :D_ATTN], qg_ref[...], seg_ref[...])
    k = norm_rope(att[:, D_ATTN:D_ATTN + D_KV], kg_ref[...], seg_ref[:D_KV, :D_KV])
    v = att[:, D_ATTN + D_KV:D_ATTN + 2 * D_KV]
    q_ref[0] = (q * (HEAD_DIM ** -0.5)).astype(BF16)
    k_ref[0] = jnp.concatenate([k, pltpu.roll(k, HEAD_DIM, 1)], axis=1).astype(BF16)
    v_ref[0] = jnp.concatenate([v, pltpu.roll(v, HEAD_DIM, 1)], axis=1).astype(BF16)
    ga_ref[0] = att[:, D_ATTN + 2 * D_KV:].astype(BF16)


def _in_proj(x, norm_g, w_in, q_norm_g, k_norm_g):
    bsz = x.shape[0]
    n_tt = SEQ // TOK_TILE
    o0 = 3 * D_HYENA
    o1 = o0 + D_HYENA
    wht = w_in[:, :o1].T.astype(BF16)
    wa = w_in[:, o1:].astype(BF16)
    head = jnp.arange(D_ATTN) // HEAD_DIM
    seg = jnp.where(head[:, None] == head[None, :], 1.0 / HEAD_DIM, 0.0).astype(BF16)
    half = HEAD_DIM // 2
    inv = ROPE_THETA ** (-jnp.arange(half, dtype=F32) / half)
    ang = jnp.arange(SEQ, dtype=F32)[:, None] * inv[None, :]
    cos = jnp.tile(jnp.cos(ang), (1, LANES // half))
    sin = jnp.tile(jnp.concatenate([-jnp.sin(ang), jnp.sin(ang)], axis=1), (1, LANES // HEAD_DIM))
    qg = jnp.tile(q_norm_g, D_ATTN // HEAD_DIM)[None]
    kg = jnp.tile(k_norm_g, D_KV // HEAD_DIM)[None]
    const = lambda shape: pl.BlockSpec(shape, lambda b, t: (0,) * len(shape))
    return pl.pallas_call(
        _in_proj_kernel,
        grid=(bsz, n_tt),
        in_specs=[
            pl.BlockSpec((1, TOK_TILE, D_MODEL), lambda b, t: (b, t, 0)),
            const((1, D_MODEL)),
            const(wht.shape),
            const(wa.shape),
            const(seg.shape),
            pl.BlockSpec((TOK_TILE, LANES), lambda b, t: (t, 0)),
            pl.BlockSpec((TOK_TILE, LANES), lambda b, t: (t, 0)),
            const((1, D_ATTN)),
            const((1, D_KV)),
        ],
        out_specs=[
            pl.BlockSpec((N_HY_TILES, 2, TOK_TILE // TBLK, 1, SUBLANES, LANES),
                         lambda b, t: (0, 0, t, b, 0, 0)),
            pl.BlockSpec((1, TOK_TILE, D_ATTN), lambda b, t: (b, t, 0)),
            pl.BlockSpec((1, TOK_TILE, 2 * D_KV), lambda b, t: (b, t, 0)),
            pl.BlockSpec((1, TOK_TILE, 2 * D_KV), lambda b, t: (b, t, 0)),
            pl.BlockSpec((1, TOK_TILE, D_ATTN), lambda b, t: (b, t, 0)),
        ],
        out_shape=[
            jax.ShapeDtypeStruct((N_HY_TILES, 2, N_TBLK, bsz, SUBLANES, LANES), F32),
            jax.ShapeDtypeStruct((bsz, SEQ, D_ATTN), BF16),
            jax.ShapeDtypeStruct((bsz, SEQ, 2 * D_KV), BF16),
            jax.ShapeDtypeStruct((bsz, SEQ, 2 * D_KV), BF16),
            jax.ShapeDtypeStruct((bsz, SEQ, D_ATTN), BF16),
        ],
        compiler_params=pltpu.CompilerParams(dimension_semantics=("arbitrary", "arbitrary"),
                                             vmem_limit_bytes=VMEM_LIMIT),
        name="in_proj",
    )(x, norm_g[None], wht, wa, seg, cos, sin, qg, kg)


def _hyena_kernel(bsz, v_ref, x1_ref, x2_ref, g0_ref, g1_ref, cw_ref, cb_ref, hb_ref,
                  o_ref, wt0_ref, wt1_ref, upad_ref):
    ct = pl.program_id(0)
    rows = N_TBLK * bsz
    pad = (N_TBLK - 1) * bsz
    half_rows = rows * SUBLANES

    sub = lax.broadcasted_iota(jnp.int32, (SUBLANES, LANES), 0)
    lane = lax.broadcasted_iota(jnp.int32, (SUBLANES, LANES), 1)
    keep = lane >= sub
    lane_t = lax.broadcasted_iota(jnp.int32, (rows, TBLK), 1)
    zeros_blk = jnp.zeros((bsz, TBLK), F32)

    upad_ref[0:pad, :] = jnp.zeros((pad, TBLK), BF16)
    upad_ref[pad + rows:, :] = jnp.zeros((pad, TBLK), BF16)

    def channel(c, carry):
        ch = ct * SUBLANES + c

        def load_rows(ref):
            lo = ref[pl.ds(c, rows, stride=SUBLANES), :]
            hi = ref[pl.ds(half_rows + c, rows, stride=SUBLANES), :]
            return jnp.concatenate([lo, hi], axis=1)

        def short_conv(u, stream):
            idx = stream * D_HYENA + ch
            um = pltpu.roll(u, 1, 1)
            up = pltpu.roll(u, TBLK - 1, 1)
            prev_blk = jnp.concatenate([zeros_blk, um[:rows - bsz]], axis=0)
            next_blk = jnp.concatenate([up[bsz:], zeros_blk], axis=0)
            um = jnp.where(lane_t == 0, prev_blk, um)
            up = jnp.where(lane_t == TBLK - 1, next_blk, up)
            return cw_ref[0, idx] * um + cw_ref[1, idx] * u + cw_ref[2, idx] * up + cb_ref[idx]

        def build_toeplitz(g_ref, wt_ref):
            rolled = {}

            def rot(m):
                if m not in rolled:
                    window = g_ref[pl.ds((m + 16) * SUBLANES + c, 1), :]
                    window = jnp.broadcast_to(window, (SUBLANES, LANES))
                    rolled[m] = pltpu.roll(window, 0, 1, stride=1, stride_axis=0)
                return rolled[m]

            n_pairs = (N_SHIFT * TBLK + LANES) // 16
            left_pairs = N_SHIFT * TBLK // 16
            for k in range(n_pairs, 0, -1):
                va = jnp.where(keep, rot(2 * k), rot(2 * k - 16))
                vb = jnp.where(keep, rot(2 * k - 1), rot(2 * k - 17))
                rolled.pop(2 * k, None)
                rolled.pop(2 * k - 1, None)
                pair = jnp.concatenate([va, vb], axis=0).astype(BF16)
                if k <= left_pairs:
                    wt_ref[16 * (left_pairs - k):16 * (left_pairs - k) + 16, 0:LANES] = pair
                if k > n_pairs - left_pairs:
                    wt_ref[16 * (n_pairs - k):16 * (n_pairs - k) + 16, LANES:2 * LANES] = pair

        def long_conv(u, wt_ref):
            upad_ref[pad:pad + rows, :] = u.astype(BF16)
            lhs = jnp.concatenate([upad_ref[bsz * e:bsz * e + rows, :] for e in range(N_SHIFT)], axis=1)
            return jnp.dot(lhs, wt_ref[...], preferred_element_type=F32)

        hv = short_conv(load_rows(v_ref), 0)
        hx1 = short_conv(load_rows(x1_ref), 1)
        hx2 = short_conv(load_rows(x2_ref), 2)
        build_toeplitz(g0_ref, wt0_ref)
        build_toeplitz(g1_ref, wt1_ref)
        z = hx1 * (long_conv(hv, wt0_ref) + hb_ref[0, ch] * hv)
        y = hx2 * (long_conv(z, wt1_ref) + hb_ref[1, ch] * z)
        o_ref[pl.ds(c, rows, stride=SUBLANES), :] = y[:, :LANES]
        o_ref[pl.ds(half_rows + c, rows, stride=SUBLANES), :] = y[:, LANES:]
        return carry

    lax.fori_loop(0, SUBLANES, channel, 0)


def _hyena(hy, gm, conv_w, conv_b, hyena_bias, bsz):
    rows = N_TBLK * bsz
    prow = 2 * rows * SUBLANES
    pad = (N_TBLK - 1) * bsz
    hy = hy.reshape(N_HY_TILES, prow, LANES)
    smem = pl.BlockSpec(memory_space=pltpu.SMEM)
    gspec = lambda off: pl.BlockSpec((None, N_WIN * SUBLANES, LANES), lambda i: (i + off, 0, 0))
    uspec = lambda off: pl.BlockSpec((None, prow, LANES), lambda i: (i + off, 0, 0))
    return pl.pallas_call(
        functools.partial(_hyena_kernel, bsz),
        grid=(HY_TILES,),
        in_specs=[uspec(0), uspec(HY_TILES), uspec(2 * HY_TILES), gspec(0), gspec(HY_TILES),
                  smem, smem, smem],
        out_specs=pl.BlockSpec((None, prow, LANES), lambda i: (i, 0, 0)),
        out_shape=jax.ShapeDtypeStruct((HY_TILES, prow, LANES), F32),
        scratch_shapes=[
            pltpu.VMEM((N_SHIFT * TBLK, TBLK), BF16),
            pltpu.VMEM((N_SHIFT * TBLK, TBLK), BF16),
            pltpu.VMEM((rows + 2 * pad, TBLK), BF16),
        ],
        compiler_params=pltpu.CompilerParams(dimension_semantics=("arbitrary",),
                                             vmem_limit_bytes=VMEM_LIMIT),
        name="hyena",
    )(hy, hy, hy, gm, gm, conv_w, conv_b, hyena_bias)


def _attn_kernel(q_ref, k_ref, v_ref, sink_ref, o_ref):
    qb = WINDOW
    span = 3 * WINDOW
    n_blk = SEQ // qb
    lane = lax.broadcasted_iota(jnp.int32, (qb, LANES), 1)
    low = lane < HEAD_DIM
    row2 = lax.broadcasted_iota(jnp.int32, (2 * qb, 1), 0)
    qrow = lax.broadcasted_iota(jnp.int32, (2 * qb, span), 0) % qb
    kcol = lax.broadcasted_iota(jnp.int32, (2 * qb, span), 1)
    rel = kcol - qrow
    lane2 = lax.broadcasted_iota(jnp.int32, (2 * qb, LANES), 1)

    for kvh in range(N_KV_HEADS):
        same = slice(0, LANES) if kvh == 0 else slice(LANES, 2 * LANES)
        swap = slice(LANES, 2 * LANES) if kvh == 0 else slice(0, LANES)
        sink_e = jnp.where(row2 < qb, sink_ref[4 * kvh], sink_ref[4 * kvh + 2])
        sink_o = jnp.where(row2 < qb, sink_ref[4 * kvh + 1], sink_ref[4 * kvh + 3])

        def block(i, carry):
            q0 = pl.multiple_of(i * qb, qb)
            ws = pl.multiple_of(jnp.clip(q0 - WINDOW, 0, SEQ - span), WINDOW)
            qa = q_ref[0, pl.ds(q0, qb), (2 * kvh) * LANES:(2 * kvh + 1) * LANES]
            qc = q_ref[0, pl.ds(q0, qb), (2 * kvh + 1) * LANES:(2 * kvh + 2) * LANES]
            zero = jnp.zeros_like(qa)
            q_even = jnp.concatenate([jnp.where(low, qa, zero), jnp.where(low, qc, zero)], axis=0)
            q_odd = jnp.concatenate([jnp.where(low, zero, qa), jnp.where(low, zero, qc)], axis=0)
            k_e = k_ref[0, pl.ds(ws, span), same]
            k_o = k_ref[0, pl.ds(ws, span), swap]
            v_e = v_ref[0, pl.ds(ws, span), same]
            v_o = v_ref[0, pl.ds(ws, span), swap]
            valid = jnp.abs(rel + (ws - q0)) <= WINDOW

            def head_pair(qm, km, vm, sink):
                s = jnp.where(valid, _nt_dot(qm, km), -jnp.inf)
                m = jnp.maximum(jnp.max(s, axis=-1, keepdims=True), sink)
                p = jnp.exp(s - m)
                denom = jnp.sum(p, axis=-1, keepdims=True) + jnp.exp(sink - m)
                o = jnp.dot(p.astype(BF16), vm, preferred_element_type=F32)
                return o / denom

            o_e = head_pair(q_even, k_e, v_e, sink_e)
            o_o = head_pair(q_odd, k_o, v_o, sink_o)
            o = jnp.where(lane2 < HEAD_DIM, o_e, o_o)
            o_ref[0, pl.ds(q0, qb), (2 * kvh) * LANES:(2 * kvh + 1) * LANES] = o[:qb]
            o_ref[0, pl.ds(q0, qb), (2 * kvh + 1) * LANES:(2 * kvh + 2) * LANES] = o[qb:]
            return carry

        lax.fori_loop(0, n_blk, block, 0)


def _attention(q, k2, v2, sink):
    bsz = q.shape[0]
    return pl.pallas_call(
        _attn_kernel,
        grid=(bsz,),
        in_specs=[
            pl.BlockSpec((1, SEQ, D_ATTN), lambda b: (b, 0, 0)),
            pl.BlockSpec((1, SEQ, 2 * D_KV), lambda b: (b, 0, 0)),
            pl.BlockSpec((1, SEQ, 2 * D_KV), lambda b: (b, 0, 0)),
            pl.BlockSpec(memory_space=pltpu.SMEM),
        ],
        out_specs=pl.BlockSpec((1, SEQ, D_ATTN), lambda b: (b, 0, 0)),
        out_shape=jax.ShapeDtypeStruct((bsz, SEQ, D_ATTN), F32),
        compiler_params=pltpu.CompilerParams(dimension_semantics=("arbitrary",),
                                             vmem_limit_bytes=VMEM_LIMIT),
        name="attention",
    )(q, k2, v2, sink)


def _out_proj_kernel(x_ref, yh_ref, gh_ref, ya_ref, ga_ref, hg_ref, ag_ref, woh_ref, woa_ref, o_ref):
    def chan_major(ref):
        parts = []
        for kk in range(TOK_TILE // LANES):
            blk, half = kk // 2, kk % 2
            parts.append(ref[:, half, blk, 0, :, :].reshape(D_HYENA, LANES))
        return jnp.concatenate(parts, axis=1)

    yh = chan_major(yh_ref)
    gh = chan_major(gh_ref)
    yh_n = yh * lax.rsqrt(jnp.mean(yh * yh, axis=0, keepdims=True) + EPS) * hg_ref[...]
    yh_g = (yh_n * (gh * jax.nn.sigmoid(gh))).astype(BF16)
    acc = lax.dot_general(yh_g, woh_ref[...], (((0,), (0,)), ((), ())), preferred_element_type=F32)

    ya = ya_ref[0]
    ga = ga_ref[0].astype(F32)
    ya_n = ya * lax.rsqrt(jnp.mean(ya * ya, axis=-1, keepdims=True) + EPS) * ag_ref[...]
    ya_g = (ya_n * (ga * jax.nn.sigmoid(ga))).astype(BF16)
    acc = acc + jnp.dot(ya_g, woa_ref[...], preferred_element_type=F32)
    o_ref[0] = x_ref[0] + acc


def _out_proj(x, yh, hy, ya, ga, hy_out_norm_g, attn_out_norm_g, w_out):
    bsz = x.shape[0]
    n_tt = SEQ // TOK_TILE
    yh = yh.reshape(HY_TILES, 2, N_TBLK, bsz, SUBLANES, LANES)
    woh = w_out[:D_HYENA].astype(BF16)
    woa = w_out[D_HYENA:].astype(BF16)
    const = lambda shape: pl.BlockSpec(shape, lambda b, t: (0,) * len(shape))
    packed = lambda tile_blk: pl.BlockSpec(
        (HY_TILES, 2, TOK_TILE // TBLK, 1, SUBLANES, LANES), lambda b, t: (tile_blk, 0, t, b, 0, 0))
    return pl.pallas_call(
        _out_proj_kernel,
        grid=(bsz, n_tt),
        in_specs=[
            pl.BlockSpec((1, TOK_TILE, D_MODEL), lambda b, t: (b, t, 0)),
            packed(0),
            packed(3),
            pl.BlockSpec((1, TOK_TILE, D_ATTN), lambda b, t: (b, t, 0)),
            pl.BlockSpec((1, TOK_TILE, D_ATTN), lambda b, t: (b, t, 0)),
            const((D_HYENA, 1)),
            const((1, D_ATTN)),
            const(woh.shape),
            const(woa.shape),
        ],
        out_specs=pl.BlockSpec((1, TOK_TILE, D_MODEL), lambda b, t: (b, t, 0)),
        out_shape=jax.ShapeDtypeStruct(x.shape, x.dtype),
        compiler_params=pltpu.CompilerParams(dimension_semantics=("arbitrary", "arbitrary"),
                                             vmem_limit_bytes=VMEM_LIMIT),
        name="out_proj",
    )(x, yh, hy, ya, ga, hy_out_norm_g[:, None], attn_out_norm_g[None], woh, woa)


def kernel(x, norm_g, w_in, conv_w, conv_b, filt_w1, filt_b1, filt_w2, filt_b2, filt_w3, filt_b3,
           filt_w4, filt_sin_freq, hyena_bias, q_norm_g, k_norm_g, attn_sink, hy_out_norm_g,
           attn_out_norm_g, w_out):
    bsz, seq, d_model = x.shape
    assert seq == SEQ and d_model == D_MODEL and bsz % SUBLANES == 0
    assert norm_g.shape[0] == 1, "one layer"
    gm = _filter_windows(filt_w1[0], filt_b1[0], filt_w2[0], filt_b2[0], filt_w3[0], filt_b3[0],
                         filt_w4[0], filt_sin_freq[0])
    hy, q, k2, v2, ga = _in_proj(x, norm_g[0], w_in[0], q_norm_g[0], k_norm_g[0])
    yh = _hyena(hy, gm, conv_w[0], conv_b[0], hyena_bias[0], bsz)
    ya = _attention(q, k2, v2, attn_sink[0])
    return _out_proj(x, yh, hy, ya, ga, hy_out_norm_g[0], attn_out_norm_g[0], w_out[0])
```

```python
import functools
import math

import jax
import jax.numpy as jnp
from jax import lax
from jax.experimental import pallas as pl
from jax.experimental.pallas import tpu as pltpu

F32 = jnp.float32
BF16 = jnp.bfloat16
HI = lax.Precision.HIGHEST

D_MODEL = 1024
SEQ = 2048
D_HYENA = 512
D_ATTN = 512
HEAD_DIM = 64
N_Q_HEADS = 8
N_KV_HEADS = 2
WINDOW = 128
ROPE_THETA = 10000.0
FILTER_HIDDEN = 64
N_BANDS = 16
DECAY_TARGET = 1e-2
FAST_DECAY_PCT = 0.3
SLOW_DECAY_PCT = 1.5
EPS = 1e-6

LANES = 128
SUBLANES = 8
TBLK = 256
N_TBLK = SEQ // TBLK
N_SHIFT = 2 * N_TBLK - 1
N_HY_T = 4 * D_HYENA
N_HY_TILES = N_HY_T // SUBLANES
HY_TILES = D_HYENA // SUBLANES
N_ATT_COLS = D_ATTN + 2 * HEAD_DIM * N_KV_HEADS + D_ATTN
D_KV = HEAD_DIM * N_KV_HEADS

FILT_LPAD = 128
N_POS = 34 * LANES
WIN_SHIFTS = LANES // SUBLANES
N_WIN = 33 * WIN_SHIFTS
FILT_CH = 32
TOK_TILE = 512
VMEM_LIMIT = 56 * 1024 * 1024


def _nt_dot(a, b, precision=None):
    return lax.dot_general(a, b, (((1,), (1,)), ((), ())), preferred_element_type=F32,
                           precision=precision)


def _filter_kernel(w1t_ref, w1c_ref, w1s_ref, b1_ref, w2_ref, b2_ref, w3_ref, b3_ref,
                   w4_ref, fr_ref, o_ref, hid_ref):
    centre = FILT_LPAD + SEQ
    first = jnp.logical_and(pl.program_id(0) == 0, pl.program_id(1) == 0)

    @pl.when(first)
    def _():
        off_c = lax.broadcasted_iota(jnp.int32, (N_POS, 1), 0) - centre
        idx_c = jnp.minimum(jnp.abs(off_c), SEQ - 1).astype(F32)
        t_c = idx_c / (SEQ - 1)
        w_c = (2.0 * math.pi) * idx_c / SEQ
        band = lax.broadcasted_iota(jnp.int32, (1, N_BANDS), 1).astype(F32)
        freqs = 1e-4 + band * ((N_BANDS - 1 - 1e-4) / (N_BANDS - 1))
        ang = w_c * freqs
        fr = fr_ref[...]
        pre = (t_c * w1t_ref[...]
               + jnp.dot(jnp.cos(ang), w1c_ref[...], preferred_element_type=F32, precision=HI)
               - jnp.dot(jnp.sin(ang), w1s_ref[...], preferred_element_type=F32, precision=HI)
               + b1_ref[...])
        h = jnp.sin(fr * pre)
        h = jnp.sin(fr * (jnp.dot(h, w2_ref[...], preferred_element_type=F32, precision=HI) + b2_ref[...]))
        h = jnp.sin(fr * (jnp.dot(h, w3_ref[...], preferred_element_type=F32, precision=HI) + b3_ref[...]))
        hid_ref[...] = h

    hid = hid_ref[...]
    fwd = _nt_dot(w4_ref[0, 0], hid, precision=HI)
    bwd = _nt_dot(w4_ref[0, 1], hid, precision=HI)
    off_r = lax.broadcasted_iota(jnp.int32, (1, N_POS), 1) - centre
    idx_r = jnp.minimum(jnp.abs(off_r), SEQ - 1).astype(F32)
    max_decay = math.log(DECAY_TARGET) / FAST_DECAY_PCT
    min_decay = math.log(DECAY_TARGET) / SLOW_DECAY_PCT
    chan = (pl.program_id(1) * FILT_CH
            + lax.broadcasted_iota(jnp.int32, (FILT_CH, 1), 0)).astype(F32)
    deltas = min_decay + chan * ((max_decay - min_decay) / (D_HYENA - 1))
    decay = jnp.exp(-(idx_r / (SEQ - 1)) * jnp.abs(deltas))
    decay = jnp.where(jnp.abs(off_r) <= SEQ - 1, decay, 0.0)
    val = jnp.where(off_r < 0, bwd, fwd) * decay
    for r0 in range(WIN_SHIFTS):
        shifted = val if r0 == 0 else pltpu.roll(val, N_POS - SUBLANES * r0, 1)
        for r1 in range(N_WIN // WIN_SHIFTS):
            o_ref[:, WIN_SHIFTS * r1 + r0, :, :] = (
                shifted[:, r1 * LANES:(r1 + 1) * LANES].reshape(FILT_CH // SUBLANES, SUBLANES, LANES))


def _filter_windows(w1, b1, w2, b2, w3, b3, w4, sin_freq):
    full = lambda shape: pl.BlockSpec(shape, lambda o, j: (0,) * len(shape))
    w4r = w4.T.reshape(2, 2, D_HYENA, FILTER_HIDDEN)
    args = (w1[0:1], w1[1:1 + N_BANDS], w1[1 + N_BANDS:], b1[None], w2, b2[None], w3, b3[None])
    n_j = D_HYENA // FILT_CH
    tiles = FILT_CH // SUBLANES
    out = pl.pallas_call(
        _filter_kernel,
        grid=(2, n_j),
        in_specs=[full(a.shape) for a in args] + [
            pl.BlockSpec((1, 2, FILT_CH, FILTER_HIDDEN), lambda o, j: (o, 0, j, 0)),
            full((1, FILTER_HIDDEN)),
        ],
        out_specs=pl.BlockSpec((tiles, N_WIN, SUBLANES, LANES), lambda o, j: (o * n_j + j, 0, 0, 0)),
        out_shape=jax.ShapeDtypeStruct((2 * HY_TILES, N_WIN, SUBLANES, LANES), F32),
        scratch_shapes=[pltpu.VMEM((N_POS, FILTER_HIDDEN), F32)],
        compiler_params=pltpu.CompilerParams(dimension_semantics=("arbitrary", "arbitrary"),
                                             vmem_limit_bytes=VMEM_LIMIT),
        name="filter_windows",
    )(*args, w4r, sin_freq[None])
    return out.reshape(2 * HY_TILES, N_WIN * SUBLANES, LANES)


def _in_proj_kernel(x_ref, g_ref, wht_ref, wa_ref, seg_ref, cos_ref, sin_ref, qg_ref, kg_ref,
                    hy_ref, q_ref, k_ref, v_ref, ga_ref):
    x = x_ref[0]
    ms = jnp.mean(x * x, axis=-1, keepdims=True)
    h = (x * lax.rsqrt(ms + EPS) * g_ref[...]).astype(BF16)

    ctile_blk = 256
    for cb in range(N_HY_T // ctile_blk):
        res = _nt_dot(wht_ref[cb * ctile_blk:(cb + 1) * ctile_blk, :], h)
        for kk in range(TOK_TILE // LANES):
            blk, half = kk // 2, kk % 2
            hy_ref[cb * 32:(cb + 1) * 32, half, blk, 0, :, :] = (
                res[:, kk * LANES:(kk + 1) * LANES].reshape(32, SUBLANES, LANES))

    att = jnp.dot(h, wa_ref[...], preferred_element_type=F32)
    cos = cos_ref[...]
    sin = sin_ref[...]
    lane = lax.broadcasted_iota(jnp.int32, (TOK_TILE, LANES), 1)
    first_half = (lane % HEAD_DIM) < (HEAD_DIM // 2)

    def norm_rope(t, gain, seg):
        ms_h = jnp.dot((t * t).astype(BF16), seg, preferred_element_type=F32)
        tn = t * lax.rsqrt(ms_h + EPS) * gain
        outs = []
        for j in range(t.shape[1] // LANES):
            c = tn[:, j * LANES:(j + 1) * LANES]
            swapped = jnp.where(first_half, pltpu.roll(c, LANES - HEAD_DIM // 2, 1),
                                pltpu.roll(c, HEAD_DIM // 2, 1))
            outs.append(c * cos + swapped * sin)
        return outs[0] if len(outs) == 1 else jnp.concatenate(outs, axis=1)

    q = norm_rope(att[:, :D_ATTN], qg_ref[...], seg_ref[...])
    k = norm_rope(att[:, D_ATTN:D_ATTN + D_KV], kg_ref[...], seg_ref[:D_KV, :D_KV])
    v = att[:, D_ATTN + D_KV:D_ATTN + 2 * D_KV]
    q_ref[0] = (q * (HEAD_DIM ** -0.5)).astype(BF16)
    k_ref[0] = jnp.concatenate([k, pltpu.roll(k, HEAD_DIM, 1)], axis=1).astype(BF16)
    v_ref[0] = jnp.concatenate([v, pltpu.roll(v, HEAD_DIM, 1)], axis=1).astype(BF16)
    ga_ref[0] = att[:, D_ATTN + 2 * D_KV:].astype(BF16)


def _in_proj(x, norm_g, w_in, q_norm_g, k_norm_g):
    bsz = x.shape[0]
    n_tt = SEQ // TOK_TILE
    o0 = 3 * D_HYENA
    o1 = o0 + D_HYENA
    wht = w_in[:, :o1].T.astype(BF16)
    wa = w_in[:, o1:].astype(BF16)
    head = jnp.arange(D_ATTN) // HEAD_DIM
    seg = jnp.where(head[:, None] == head[None, :], 1.0 / HEAD_DIM, 0.0).astype(BF16)
    half = HEAD_DIM // 2
    inv = ROPE_THETA ** (-jnp.arange(half, dtype=F32) / half)
    ang = jnp.arange(SEQ, dtype=F32)[:, None] * inv[None, :]
    cos = jnp.tile(jnp.cos(ang), (1, LANES // half))
    sin = jnp.tile(jnp.concatenate([-jnp.sin(ang), jnp.sin(ang)], axis=1), (1, LANES // HEAD_DIM))
    qg = jnp.tile(q_norm_g, D_ATTN // HEAD_DIM)[None]
    kg = jnp.tile(k_norm_g, D_KV // HEAD_DIM)[None]
    const = lambda shape: pl.BlockSpec(shape, lambda b, t: (0,) * len(shape))
    return pl.pallas_call(
        _in_proj_kernel,
        grid=(bsz, n_tt),
        in_specs=[
            pl.BlockSpec((1, TOK_TILE, D_MODEL), lambda b, t: (b, t, 0)),
            const((1, D_MODEL)),
            const(wht.shape),
            const(wa.shape),
            const(seg.shape),
            pl.BlockSpec((TOK_TILE, LANES), lambda b, t: (t, 0)),
            pl.BlockSpec((TOK_TILE, LANES), lambda b, t: (t, 0)),
            const((1, D_ATTN)),
            const((1, D_KV)),
        ],
        out_specs=[
            pl.BlockSpec((N_HY_TILES, 2, TOK_TILE // TBLK, 1, SUBLANES, LANES),
                         lambda b, t: (0, 0, t, b, 0, 0)),
            pl.BlockSpec((1, TOK_TILE, D_ATTN), lambda b, t: (b, t, 0)),
            pl.BlockSpec((1, TOK_TILE, 2 * D_KV), lambda b, t: (b, t, 0)),
            pl.BlockSpec((1, TOK_TILE, 2 * D_KV), lambda b, t: (b, t, 0)),
            pl.BlockSpec((1, TOK_TILE, D_ATTN), lambda b, t: (b, t, 0)),
        ],
        out_shape=[
            jax.ShapeDtypeStruct((N_HY_TILES, 2, N_TBLK, bsz, SUBLANES, LANES), F32),
            jax.ShapeDtypeStruct((bsz, SEQ, D_ATTN), BF16),
            jax.ShapeDtypeStruct((bsz, SEQ, 2 * D_KV), BF16),
            jax.ShapeDtypeStruct((bsz, SEQ, 2 * D_KV), BF16),
            jax.ShapeDtypeStruct((bsz, SEQ, D_ATTN), BF16),
        ],
        compiler_params=pltpu.CompilerParams(dimension_semantics=("arbitrary", "arbitrary"),
                                             vmem_limit_bytes=VMEM_LIMIT),
        name="in_proj",
    )(x, norm_g[None], wht, wa, seg, cos, sin, qg, kg)


STRIP_ROWS = N_SHIFT * TBLK + LANES


def _hyena_kernel(bsz, n_ch, v_ref, x1_ref, x2_ref, g0_ref, g1_ref, cw_ref, cb_ref, hb_ref,
                  o_ref, sa0_ref, sa1_ref, sb0_ref, sb1_ref, u0_ref, u1_ref):
    ct = pl.program_id(0)
    rows = N_TBLK * bsz
    half_rows = rows * SUBLANES
    n_pairs = STRIP_ROWS // 16

    row16 = lax.broadcasted_iota(jnp.int32, (16, LANES), 0)
    lane16 = lax.broadcasted_iota(jnp.int32, (16, LANES), 1)
    keep = lane16 >= (row16 % SUBLANES)
    lane_h = lax.broadcasted_iota(jnp.int32, (rows, LANES), 1)
    zeros_blk = jnp.zeros((bsz, LANES), F32)

    def build_strip(g_ref, c, strip_ref):
        packed = {}

        def rotated_pair(k):
            if k not in packed:
                tiles = []
                for m in (2 * k, 2 * k - 1):
                    window = g_ref[pl.ds((m + 16) * SUBLANES + c, 1), :]
                    window = jnp.broadcast_to(window, (SUBLANES, LANES))
                    tiles.append(pltpu.roll(window, 0, 1, stride=1, stride_axis=0))
                packed[k] = jnp.concatenate(tiles, axis=0).astype(BF16)
            return packed[k]

        for k in range(n_pairs, 0, -1):
            pair = jnp.where(keep, rotated_pair(k), rotated_pair(k - 8))
            packed.pop(k)
            strip_ref[16 * (n_pairs - k):16 * (n_pairs - k) + 16, :] = pair

    def weights(strip_ref, e):
        return jnp.concatenate([strip_ref[TBLK * e + LANES:TBLK * e + LANES + TBLK, :],
                                strip_ref[TBLK * e:TBLK * e + TBLK, :]], axis=1)

    def load_rows(ref, c):
        lo = ref[pl.ds(c, rows, stride=SUBLANES), :]
        hi = ref[pl.ds(half_rows + c, rows, stride=SUBLANES), :]
        return jnp.concatenate([lo, hi], axis=1)

    def short_conv(u, stream, ch):
        idx = stream * n_ch + ch
        um = pltpu.roll(u, 1, 1)
        up = pltpu.roll(u, TBLK - 1, 1)
        um_lo = um[:, :LANES]
        prev_blk = jnp.concatenate([zeros_blk, um_lo[:rows - bsz]], axis=0)
        um = jnp.concatenate([jnp.where(lane_h == 0, prev_blk, um_lo), um[:, LANES:]], axis=1)
        up_hi = up[:, LANES:]
        next_blk = jnp.concatenate([up_hi[bsz:], zeros_blk], axis=0)
        up = jnp.concatenate([up[:, :LANES], jnp.where(lane_h == LANES - 1, next_blk, up_hi)], axis=1)
        return cw_ref[0, idx] * um + cw_ref[1, idx] * u + cw_ref[2, idx] * up + cb_ref[idx]

    def long_conv(u, strip_ref, u_ref):
        u_ref[...] = u.astype(BF16)
        dot = lambda lhs, e: jnp.dot(lhs, weights(strip_ref, e), preferred_element_type=F32)
        acc_p = None
        acc_n = None
        for d in range(N_TBLK - 1, 0, -1):
            res = dot(u_ref[0:rows - bsz * d, :], N_TBLK - 1 - d)
            acc_p = res if acc_p is None else jnp.concatenate([res[:bsz], res[bsz:] + acc_p], axis=0)
            res = dot(u_ref[bsz * d:rows, :], N_TBLK - 1 + d)
            acc_n = res if acc_n is None else jnp.concatenate([res[:-bsz] + acc_n, res[-bsz:]], axis=0)
        res = dot(u_ref[...], N_TBLK - 1)
        return jnp.concatenate([res[:bsz] + acc_n[:bsz],
                                res[bsz:rows - bsz] + acc_n[bsz:] + acc_p[:rows - 2 * bsz],
                                res[rows - bsz:] + acc_p[rows - 2 * bsz:]], axis=0)

    def process(c, s0_ref, s1_ref):
        ch = ct * SUBLANES + c
        hv = short_conv(load_rows(v_ref, c), 0, ch)
        hx1 = short_conv(load_rows(x1_ref, c), 1, ch)
        hx2 = short_conv(load_rows(x2_ref, c), 2, ch)
        z = hx1 * (long_conv(hv, s0_ref, u0_ref) + hb_ref[0, ch] * hv)
        y = hx2 * (long_conv(z, s1_ref, u1_ref) + hb_ref[1, ch] * z)
        o_ref[pl.ds(c, rows, stride=SUBLANES), :] = y[:, :LANES]
        o_ref[pl.ds(half_rows + c, rows, stride=SUBLANES), :] = y[:, LANES:]

    build_strip(g0_ref, 0, sa0_ref)
    build_strip(g1_ref, 0, sa1_ref)

    def channel_pair(i, carry):
        c = 2 * i
        build_strip(g0_ref, c + 1, sb0_ref)
        build_strip(g1_ref, c + 1, sb1_ref)
        process(c, sa0_ref, sa1_ref)
        nxt = jnp.minimum(c + 2, SUBLANES - 1)
        build_strip(g0_ref, nxt, sa0_ref)
        build_strip(g1_ref, nxt, sa1_ref)
        process(c + 1, sb0_ref, sb1_ref)
        return carry

    lax.fori_loop(0, SUBLANES // 2, channel_pair, 0)


def _hyena(hy, gm, conv_w, conv_b, hyena_bias, bsz):
    n_tiles = gm.shape[0] // 2
    rows = N_TBLK * bsz
    prow = 2 * rows * SUBLANES
    hy = hy.reshape(hy.shape[0], prow, LANES)
    smem = pl.BlockSpec(memory_space=pltpu.SMEM)
    gspec = lambda off: pl.BlockSpec((None, N_WIN * SUBLANES, LANES), lambda i: (i + off, 0, 0))
    uspec = lambda off: pl.BlockSpec((None, prow, LANES), lambda i: (i + off, 0, 0))
    strip = pltpu.VMEM((STRIP_ROWS, LANES), BF16)
    ubuf = pltpu.VMEM((rows, TBLK), BF16)
    return pl.pallas_call(
        functools.partial(_hyena_kernel, bsz, n_tiles * SUBLANES),
        grid=(n_tiles,),
        in_specs=[uspec(0), uspec(n_tiles), uspec(2 * n_tiles), gspec(0), gspec(n_tiles),
                  smem, smem, smem],
        out_specs=pl.BlockSpec((None, prow, LANES), lambda i: (i, 0, 0)),
        out_shape=jax.ShapeDtypeStruct((n_tiles, prow, LANES), F32),
        scratch_shapes=[strip, strip, strip, strip, ubuf, ubuf],
        compiler_params=pltpu.CompilerParams(dimension_semantics=("arbitrary",),
                                             vmem_limit_bytes=VMEM_LIMIT),
        name="hyena",
    )(hy, hy, hy, gm, gm, conv_w, conv_b, hyena_bias)


def _attn_kernel(q_ref, k_ref, v_ref, sink_ref, o_ref):
    qb = WINDOW
    span = 3 * WINDOW
    n_blk = SEQ // qb
    lane = lax.broadcasted_iota(jnp.int32, (qb, LANES), 1)
    low = lane < HEAD_DIM
    row2 = lax.broadcasted_iota(jnp.int32, (2 * qb, 1), 0)
    qrow = lax.broadcasted_iota(jnp.int32, (2 * qb, span), 0) % qb
    kcol = lax.broadcasted_iota(jnp.int32, (2 * qb, span), 1)
    rel = kcol - qrow
    lane2 = lax.broadcasted_iota(jnp.int32, (2 * qb, LANES), 1)

    for kvh in range(N_KV_HEADS):
        same = slice(0, LANES) if kvh == 0 else slice(LANES, 2 * LANES)
        swap = slice(LANES, 2 * LANES) if kvh == 0 else slice(0, LANES)
        sink_e = jnp.where(row2 < qb, sink_ref[4 * kvh], sink_ref[4 * kvh + 2])
        sink_o = jnp.where(row2 < qb, sink_ref[4 * kvh + 1], sink_ref[4 * kvh + 3])

        def block(i, carry):
            q0 = pl.multiple_of(i * qb, qb)
            ws = pl.multiple_of(jnp.clip(q0 - WINDOW, 0, SEQ - span), WINDOW)
            qa = q_ref[0, pl.ds(q0, qb), (2 * kvh) * LANES:(2 * kvh + 1) * LANES]
            qc = q_ref[0, pl.ds(q0, qb), (2 * kvh + 1) * LANES:(2 * kvh + 2) * LANES]
            zero = jnp.zeros_like(qa)
            q_even = jnp.concatenate([jnp.where(low, qa, zero), jnp.where(low, qc, zero)], axis=0)
            q_odd = jnp.concatenate([jnp.where(low, zero, qa), jnp.where(low, zero, qc)], axis=0)
            k_e = k_ref[0, pl.ds(ws, span), same]
            k_o = k_ref[0, pl.ds(ws, span), swap]
            v_e = v_ref[0, pl.ds(ws, span), same]
            v_o = v_ref[0, pl.ds(ws, span), swap]
            valid = jnp.abs(rel + (ws - q0)) <= WINDOW

            def head_pair(qm, km, vm, sink):
                s = jnp.where(valid, _nt_dot(qm, km), -jnp.inf)
                m = jnp.maximum(jnp.max(s, axis=-1, keepdims=True), sink)
                p = jnp.exp(s - m)
                denom = jnp.sum(p, axis=-1, keepdims=True) + jnp.exp(sink - m)
                o = jnp.dot(p.astype(BF16), vm, preferred_element_type=F32)
                return o / denom

            o_e = head_pair(q_even, k_e, v_e, sink_e)
            o_o = head_pair(q_odd, k_o, v_o, sink_o)
            o = jnp.where(lane2 < HEAD_DIM, o_e, o_o)
            o_ref[0, pl.ds(q0, qb), (2 * kvh) * LANES:(2 * kvh + 1) * LANES] = o[:qb]
            o_ref[0, pl.ds(q0, qb), (2 * kvh + 1) * LANES:(2 * kvh + 2) * LANES] = o[qb:]
            return carry

        lax.fori_loop(0, n_blk, block, 0)


def _attention(q, k2, v2, sink):
    bsz = q.shape[0]
    return pl.pallas_call(
        _attn_kernel,
        grid=(bsz,),
        in_specs=[
            pl.BlockSpec((1, SEQ, D_ATTN), lambda b: (b, 0, 0)),
            pl.BlockSpec((1, SEQ, 2 * D_KV), lambda b: (b, 0, 0)),
            pl.BlockSpec((1, SEQ, 2 * D_KV), lambda b: (b, 0, 0)),
            pl.BlockSpec(memory_space=pltpu.SMEM),
        ],
        out_specs=pl.BlockSpec((1, SEQ, D_ATTN), lambda b: (b, 0, 0)),
        out_shape=jax.ShapeDtypeStruct((bsz, SEQ, D_ATTN), F32),
        compiler_params=pltpu.CompilerParams(dimension_semantics=("arbitrary",),
                                             vmem_limit_bytes=VMEM_LIMIT),
        name="attention",
    )(q, k2, v2, sink)


def _out_proj_kernel(x_ref, yh_ref, gh_ref, ya_ref, ga_ref, hg_ref, ag_ref, woh_ref, woa_ref, o_ref):
    def chan_major(ref):
        parts = []
        for kk in range(TOK_TILE // LANES):
            blk, half = kk // 2, kk % 2
            parts.append(ref[:, half, blk, 0, :, :].reshape(D_HYENA, LANES))
        return jnp.concatenate(parts, axis=1)

    yh = chan_major(yh_ref)
    gh = chan_major(gh_ref)
    yh_n = yh * lax.rsqrt(jnp.mean(yh * yh, axis=0, keepdims=True) + EPS) * hg_ref[...]
    yh_g = (yh_n * (gh * jax.nn.sigmoid(gh))).astype(BF16)
    acc = lax.dot_general(yh_g, woh_ref[...], (((0,), (0,)), ((), ())), preferred_element_type=F32)

    ya = ya_ref[0]
    ga = ga_ref[0].astype(F32)
    ya_n = ya * lax.rsqrt(jnp.mean(ya * ya, axis=-1, keepdims=True) + EPS) * ag_ref[...]
    ya_g = (ya_n * (ga * jax.nn.sigmoid(ga))).astype(BF16)
    acc = acc + jnp.dot(ya_g, woa_ref[...], preferred_element_type=F32)
    o_ref[0] = x_ref[0] + acc


def _out_proj(x, yh, hy, ya, ga, hy_out_norm_g, attn_out_norm_g, w_out):
    bsz = x.shape[0]
    n_tt = SEQ // TOK_TILE
    yh = yh.reshape(HY_TILES, 2, N_TBLK, bsz, SUBLANES, LANES)
    woh = w_out[:D_HYENA].astype(BF16)
    woa = w_out[D_HYENA:].astype(BF16)
    const = lambda shape: pl.BlockSpec(shape, lambda b, t: (0,) * len(shape))
    packed = lambda tile_blk: pl.BlockSpec(
        (HY_TILES, 2, TOK_TILE // TBLK, 1, SUBLANES, LANES), lambda b, t: (tile_blk, 0, t, b, 0, 0))
    return pl.pallas_call(
        _out_proj_kernel,
        grid=(bsz, n_tt),
        in_specs=[
            pl.BlockSpec((1, TOK_TILE, D_MODEL), lambda b, t: (b, t, 0)),
            packed(0),
            packed(3),
            pl.BlockSpec((1, TOK_TILE, D_ATTN), lambda b, t: (b, t, 0)),
            pl.BlockSpec((1, TOK_TILE, D_ATTN), lambda b, t: (b, t, 0)),
            const((D_HYENA, 1)),
            const((1, D_ATTN)),
            const(woh.shape),
            const(woa.shape),
        ],
        out_specs=pl.BlockSpec((1, TOK_TILE, D_MODEL), lambda b, t: (b, t, 0)),
        out_shape=jax.ShapeDtypeStruct(x.shape, x.dtype),
        compiler_params=pltpu.CompilerParams(dimension_semantics=("arbitrary", "arbitrary"),
                                             vmem_limit_bytes=VMEM_LIMIT),
        name="out_proj",
    )(x, yh, hy, ya, ga, hy_out_norm_g[:, None], attn_out_norm_g[None], woh, woa)


def kernel(x, norm_g, w_in, conv_w, conv_b, filt_w1, filt_b1, filt_w2, filt_b2, filt_w3, filt_b3,
           filt_w4, filt_sin_freq, hyena_bias, q_norm_g, k_norm_g, attn_sink, hy_out_norm_g,
           attn_out_norm_g, w_out):
    bsz, seq, d_model = x.shape
    assert seq == SEQ and d_model == D_MODEL and bsz % SUBLANES == 0
    assert norm_g.shape[0] == 1, "one layer"
    gm = _filter_windows(filt_w1[0], filt_b1[0], filt_w2[0], filt_b2[0], filt_w3[0], filt_b3[0],
                         filt_w4[0], filt_sin_freq[0])
    hy, q, k2, v2, ga = _in_proj(x, norm_g[0], w_in[0], q_norm_g[0], k_norm_g[0])
    yh = _hyena(hy, gm, conv_w[0], conv_b[0], hyena_bias[0], bsz)
    ya = _attention(q, k2, v2, attn_sink[0])
    return _out_proj(x, yh, hy, ya, ga, hy_out_norm_g[0], attn_out_norm_g[0], w_out[0])
```

```python
import functools
import math

import jax
import jax.numpy as jnp
from jax import lax
from jax.experimental import pallas as pl
from jax.experimental.pallas import tpu as pltpu

F32 = jnp.float32
BF16 = jnp.bfloat16
HI = lax.Precision.HIGHEST

D_MODEL = 1024
SEQ = 2048
D_HYENA = 512
D_ATTN = 512
HEAD_DIM = 64
N_Q_HEADS = 8
N_KV_HEADS = 2
WINDOW = 128
ROPE_THETA = 10000.0
FILTER_HIDDEN = 64
N_BANDS = 16
DECAY_TARGET = 1e-2
FAST_DECAY_PCT = 0.3
SLOW_DECAY_PCT = 1.5
EPS = 1e-6

LANES = 128
SUBLANES = 8
TBLK = 256
N_TBLK = SEQ // TBLK
N_SHIFT = 2 * N_TBLK - 1
N_HY_T = 4 * D_HYENA
N_HY_TILES = N_HY_T // SUBLANES
HY_TILES = D_HYENA // SUBLANES
N_ATT_COLS = D_ATTN + 2 * HEAD_DIM * N_KV_HEADS + D_ATTN
D_KV = HEAD_DIM * N_KV_HEADS

FILT_LPAD = 128
N_POS = 34 * LANES
WIN_SHIFTS = LANES // SUBLANES
N_WIN = 33 * WIN_SHIFTS
FILT_CH = 32
TOK_TILE = 512
VMEM_LIMIT = 56 * 1024 * 1024


def _nt_dot(a, b, precision=None):
    return lax.dot_general(a, b, (((1,), (1,)), ((), ())), preferred_element_type=F32,
                           precision=precision)


def _filter_kernel(w1t_ref, w1c_ref, w1s_ref, b1_ref, w2_ref, b2_ref, w3_ref, b3_ref,
                   w4_ref, fr_ref, o_ref, hid_ref):
    centre = FILT_LPAD + SEQ
    first = jnp.logical_and(pl.program_id(0) == 0, pl.program_id(1) == 0)

    @pl.when(first)
    def _():
        off_c = lax.broadcasted_iota(jnp.int32, (N_POS, 1), 0) - centre
        idx_c = jnp.minimum(jnp.abs(off_c), SEQ - 1).astype(F32)
        t_c = idx_c / (SEQ - 1)
        w_c = (2.0 * math.pi) * idx_c / SEQ
        band = lax.broadcasted_iota(jnp.int32, (1, N_BANDS), 1).astype(F32)
        freqs = 1e-4 + band * ((N_BANDS - 1 - 1e-4) / (N_BANDS - 1))
        ang = w_c * freqs
        fr = fr_ref[...]
        pre = (t_c * w1t_ref[...]
               + jnp.dot(jnp.cos(ang), w1c_ref[...], preferred_element_type=F32, precision=HI)
               - jnp.dot(jnp.sin(ang), w1s_ref[...], preferred_element_type=F32, precision=HI)
               + b1_ref[...])
        h = jnp.sin(fr * pre)
        h = jnp.sin(fr * (jnp.dot(h, w2_ref[...], preferred_element_type=F32, precision=HI) + b2_ref[...]))
        h = jnp.sin(fr * (jnp.dot(h, w3_ref[...], preferred_element_type=F32, precision=HI) + b3_ref[...]))
        hid_ref[...] = h

    hid = hid_ref[...]
    fwd = _nt_dot(w4_ref[0, 0], hid, precision=HI)
    bwd = _nt_dot(w4_ref[0, 1], hid, precision=HI)
    off_r = lax.broadcasted_iota(jnp.int32, (1, N_POS), 1) - centre
    idx_r = jnp.minimum(jnp.abs(off_r), SEQ - 1).astype(F32)
    max_decay = math.log(DECAY_TARGET) / FAST_DECAY_PCT
    min_decay = math.log(DECAY_TARGET) / SLOW_DECAY_PCT
    chan = (pl.program_id(1) * FILT_CH
            + lax.broadcasted_iota(jnp.int32, (FILT_CH, 1), 0)).astype(F32)
    deltas = min_decay + chan * ((max_decay - min_decay) / (D_HYENA - 1))
    decay = jnp.exp(-(idx_r / (SEQ - 1)) * jnp.abs(deltas))
    decay = jnp.where(jnp.abs(off_r) <= SEQ - 1, decay, 0.0)
    val = jnp.where(off_r < 0, bwd, fwd) * decay
    for r0 in range(WIN_SHIFTS):
        shifted = val if r0 == 0 else pltpu.roll(val, N_POS - SUBLANES * r0, 1)
        for r1 in range(N_WIN // WIN_SHIFTS):
            o_ref[:, WIN_SHIFTS * r1 + r0, :, :] = (
                shifted[:, r1 * LANES:(r1 + 1) * LANES].reshape(FILT_CH // SUBLANES, SUBLANES, LANES))


def _filter_windows(w1, b1, w2, b2, w3, b3, w4, sin_freq):
    full = lambda shape: pl.BlockSpec(shape, lambda o, j: (0,) * len(shape))
    w4r = w4.T.reshape(2, 2, D_HYENA, FILTER_HIDDEN)
    args = (w1[0:1], w1[1:1 + N_BANDS], w1[1 + N_BANDS:], b1[None], w2, b2[None], w3, b3[None])
    n_j = D_HYENA // FILT_CH
    tiles = FILT_CH // SUBLANES
    out = pl.pallas_call(
        _filter_kernel,
        grid=(2, n_j),
        in_specs=[full(a.shape) for a in args] + [
            pl.BlockSpec((1, 2, FILT_CH, FILTER_HIDDEN), lambda o, j: (o, 0, j, 0)),
            full((1, FILTER_HIDDEN)),
        ],
        out_specs=pl.BlockSpec((tiles, N_WIN, SUBLANES, LANES), lambda o, j: (o * n_j + j, 0, 0, 0)),
        out_shape=jax.ShapeDtypeStruct((2 * HY_TILES, N_WIN, SUBLANES, LANES), F32),
        scratch_shapes=[pltpu.VMEM((N_POS, FILTER_HIDDEN), F32)],
        compiler_params=pltpu.CompilerParams(dimension_semantics=("arbitrary", "arbitrary"),
                                             vmem_limit_bytes=VMEM_LIMIT),
        name="filter_windows",
    )(*args, w4r, sin_freq[None])
    return out.reshape(2 * HY_TILES, N_WIN * SUBLANES, LANES)


def _in_proj_kernel(x_ref, g_ref, wht_ref, wa_ref, seg_ref, cos_ref, sin_ref, qg_ref, kg_ref,
                    hy_ref, q_ref, k_ref, v_ref, ga_ref):
    x = x_ref[0]
    ms = jnp.mean(x * x, axis=-1, keepdims=True)
    h = (x * lax.rsqrt(ms + EPS) * g_ref[...]).astype(BF16)

    ctile_blk = 256
    for cb in range(N_HY_T // ctile_blk):
        res = _nt_dot(wht_ref[cb * ctile_blk:(cb + 1) * ctile_blk, :], h)
        for kk in range(TOK_TILE // LANES):
            blk, half = kk // 2, kk % 2
            hy_ref[cb * 32:(cb + 1) * 32, half, blk, 0, :, :] = (
                res[:, kk * LANES:(kk + 1) * LANES].reshape(32, SUBLANES, LANES))

    att = jnp.dot(h, wa_ref[...], preferred_element_type=F32)
    cos = cos_ref[...]
    sin = sin_ref[...]
    lane = lax.broadcasted_iota(jnp.int32, (TOK_TILE, LANES), 1)
    first_half = (lane % HEAD_DIM) < (HEAD_DIM // 2)

    def norm_rope(t, gain, seg):
        ms_h = jnp.dot((t * t).astype(BF16), seg, preferred_element_type=F32)
        tn = t * lax.rsqrt(ms_h + EPS) * gain
        outs = []
        for j in range(t.shape[1] // LANES):
            c = tn[:, j * LANES:(j + 1) * LANES]
            swapped = jnp.where(first_half, pltpu.roll(c, LANES - HEAD_DIM // 2, 1),
                                pltpu.roll(c, HEAD_DIM // 2, 1))
            outs.append(c * cos + swapped * sin)
        return outs[0] if len(outs) == 1 else jnp.concatenate(outs, axis=1)

    q = norm_rope(att[:, :D_ATTN], qg_ref[...], seg_ref[...])
    k = norm_rope(att[:, D_ATTN:D_ATTN + D_KV], kg_ref[...], seg_ref[:D_KV, :D_KV])
    v = att[:, D_ATTN + D_KV:D_ATTN + 2 * D_KV]
    q_ref[0] = (q * (HEAD_DIM ** -0.5)).astype(BF16)
    k_ref[0] = jnp.concatenate([k, pltpu.roll(k, HEAD_DIM, 1)], axis=1).astype(BF16)
    v_sw = pltpu.roll(v, HEAD_DIM, 1)
    low = lane < HEAD_DIM
    v_ref[0] = jnp.concatenate([jnp.where(low, v, 1.0), jnp.where(low, 1.0, v_sw),
                                jnp.where(low, v_sw, 1.0), jnp.where(low, 1.0, v)], axis=1).astype(BF16)
    ga_ref[0] = att[:, D_ATTN + 2 * D_KV:].astype(BF16)


def _in_proj(x, norm_g, w_in, q_norm_g, k_norm_g):
    bsz = x.shape[0]
    n_tt = SEQ // TOK_TILE
    o0 = 3 * D_HYENA
    o1 = o0 + D_HYENA
    wht = w_in[:, :o1].T.astype(BF16)
    wa = w_in[:, o1:].astype(BF16)
    head = jnp.arange(D_ATTN) // HEAD_DIM
    seg = jnp.where(head[:, None] == head[None, :], 1.0 / HEAD_DIM, 0.0).astype(BF16)
    half = HEAD_DIM // 2
    inv = ROPE_THETA ** (-jnp.arange(half, dtype=F32) / half)
    ang = jnp.arange(SEQ, dtype=F32)[:, None] * inv[None, :]
    cos = jnp.tile(jnp.cos(ang), (1, LANES // half))
    sin = jnp.tile(jnp.concatenate([-jnp.sin(ang), jnp.sin(ang)], axis=1), (1, LANES // HEAD_DIM))
    qg = jnp.tile(q_norm_g, D_ATTN // HEAD_DIM)[None]
    kg = jnp.tile(k_norm_g, D_KV // HEAD_DIM)[None]
    const = lambda shape: pl.BlockSpec(shape, lambda b, t: (0,) * len(shape))
    return pl.pallas_call(
        _in_proj_kernel,
        grid=(bsz, n_tt),
        in_specs=[
            pl.BlockSpec((1, TOK_TILE, D_MODEL), lambda b, t: (b, t, 0)),
            const((1, D_MODEL)),
            const(wht.shape),
            const(wa.shape),
            const(seg.shape),
            pl.BlockSpec((TOK_TILE, LANES), lambda b, t: (t, 0)),
            pl.BlockSpec((TOK_TILE, LANES), lambda b, t: (t, 0)),
            const((1, D_ATTN)),
            const((1, D_KV)),
        ],
        out_specs=[
            pl.BlockSpec((N_HY_TILES, 2, TOK_TILE // TBLK, 1, SUBLANES, LANES),
                         lambda b, t: (0, 0, t, b, 0, 0)),
            pl.BlockSpec((1, TOK_TILE, D_ATTN), lambda b, t: (b, t, 0)),
            pl.BlockSpec((1, TOK_TILE, 2 * D_KV), lambda b, t: (b, t, 0)),
            pl.BlockSpec((1, TOK_TILE, 4 * D_KV), lambda b, t: (b, t, 0)),
            pl.BlockSpec((1, TOK_TILE, D_ATTN), lambda b, t: (b, t, 0)),
        ],
        out_shape=[
            jax.ShapeDtypeStruct((N_HY_TILES, 2, N_TBLK, bsz, SUBLANES, LANES), F32),
            jax.ShapeDtypeStruct((bsz, SEQ, D_ATTN), BF16),
            jax.ShapeDtypeStruct((bsz, SEQ, 2 * D_KV), BF16),
            jax.ShapeDtypeStruct((bsz, SEQ, 4 * D_KV), BF16),
            jax.ShapeDtypeStruct((bsz, SEQ, D_ATTN), BF16),
        ],
        compiler_params=pltpu.CompilerParams(dimension_semantics=("arbitrary", "arbitrary"),
                                             vmem_limit_bytes=VMEM_LIMIT),
        name="in_proj",
    )(x, norm_g[None], wht, wa, seg, cos, sin, qg, kg)


STRIP_ROWS = N_SHIFT * TBLK + LANES


def _hyena_kernel(bsz, n_ch, v_ref, x1_ref, x2_ref, g0_ref, g1_ref, cw_ref, cb_ref, hb_ref,
                  o_ref, sa0_ref, sa1_ref, sb0_ref, sb1_ref):
    ct = pl.program_id(0)
    rows = N_TBLK * bsz
    half_rows = rows * SUBLANES
    n_pairs = STRIP_ROWS // 16

    row16 = lax.broadcasted_iota(jnp.int32, (16, LANES), 0)
    lane16 = lax.broadcasted_iota(jnp.int32, (16, LANES), 1)
    keep = lane16 >= (row16 % SUBLANES)
    lane_h = lax.broadcasted_iota(jnp.int32, (rows, LANES), 1)
    zeros_blk = jnp.zeros((bsz, LANES), F32)

    def build_strip(g_ref, c, strip_ref):
        packed = {}

        def rotated_pair(k):
            if k not in packed:
                tiles = []
                for m in (2 * k, 2 * k - 1):
                    window = g_ref[pl.ds((m + 16) * SUBLANES + c, 1), :]
                    window = jnp.broadcast_to(window, (SUBLANES, LANES))
                    tiles.append(pltpu.roll(window, 0, 1, stride=1, stride_axis=0))
                packed[k] = jnp.concatenate(tiles, axis=0).astype(BF16)
            return packed[k]

        for k in range(n_pairs, 0, -1):
            pair = jnp.where(keep, rotated_pair(k), rotated_pair(k - 8))
            packed.pop(k)
            strip_ref[16 * (n_pairs - k):16 * (n_pairs - k) + 16, :] = pair

    def weights(strip_ref, e):
        return jnp.concatenate([strip_ref[TBLK * e + LANES:TBLK * e + LANES + TBLK, :],
                                strip_ref[TBLK * e:TBLK * e + TBLK, :]], axis=1)

    def load_rows(ref, c):
        lo = ref[pl.ds(c, rows, stride=SUBLANES), :]
        hi = ref[pl.ds(half_rows + c, rows, stride=SUBLANES), :]
        return jnp.concatenate([lo, hi], axis=1)

    def short_conv(u, stream, ch):
        idx = stream * n_ch + ch
        um = pltpu.roll(u, 1, 1)
        up = pltpu.roll(u, TBLK - 1, 1)
        um_lo = um[:, :LANES]
        prev_blk = jnp.concatenate([zeros_blk, um_lo[:rows - bsz]], axis=0)
        um = jnp.concatenate([jnp.where(lane_h == 0, prev_blk, um_lo), um[:, LANES:]], axis=1)
        up_hi = up[:, LANES:]
        next_blk = jnp.concatenate([up_hi[bsz:], zeros_blk], axis=0)
        up = jnp.concatenate([up[:, :LANES], jnp.where(lane_h == LANES - 1, next_blk, up_hi)], axis=1)
        return cw_ref[0, idx] * um + cw_ref[1, idx] * u + cw_ref[2, idx] * up + cb_ref[idx]

    def long_conv(u, strip_ref):
        ub = u.astype(BF16)
        dot = lambda lhs, e: jnp.dot(lhs, weights(strip_ref, e), preferred_element_type=F32)
        acc_p = None
        acc_n = None
        for d in range(N_TBLK - 1, 0, -1):
            res = dot(ub[0:rows - bsz * d], N_TBLK - 1 - d)
            acc_p = res if acc_p is None else jnp.concatenate([res[:bsz], res[bsz:] + acc_p], axis=0)
            res = dot(ub[bsz * d:rows], N_TBLK - 1 + d)
            acc_n = res if acc_n is None else jnp.concatenate([res[:-bsz] + acc_n, res[-bsz:]], axis=0)
        res = dot(ub, N_TBLK - 1)
        return jnp.concatenate([res[:bsz] + acc_n[:bsz],
                                res[bsz:rows - bsz] + acc_n[bsz:] + acc_p[:rows - 2 * bsz],
                                res[rows - bsz:] + acc_p[rows - 2 * bsz:]], axis=0)

    def process(c, s0_ref, s1_ref):
        ch = ct * SUBLANES + c
        hv = short_conv(load_rows(v_ref, c), 0, ch)
        hx1 = short_conv(load_rows(x1_ref, c), 1, ch)
        hx2 = short_conv(load_rows(x2_ref, c), 2, ch)
        z = hx1 * (long_conv(hv, s0_ref) + hb_ref[0, ch] * hv)
        y = hx2 * (long_conv(z, s1_ref) + hb_ref[1, ch] * z)
        o_ref[pl.ds(c, rows, stride=SUBLANES), :] = y[:, :LANES]
        o_ref[pl.ds(half_rows + c, rows, stride=SUBLANES), :] = y[:, LANES:]

    build_strip(g0_ref, 0, sa0_ref)
    build_strip(g1_ref, 0, sa1_ref)

    def channel_pair(i, carry):
        c = 2 * i
        build_strip(g0_ref, c + 1, sb0_ref)
        build_strip(g1_ref, c + 1, sb1_ref)
        process(c, sa0_ref, sa1_ref)
        nxt = jnp.minimum(c + 2, SUBLANES - 1)
        build_strip(g0_ref, nxt, sa0_ref)
        build_strip(g1_ref, nxt, sa1_ref)
        process(c + 1, sb0_ref, sb1_ref)
        return carry

    lax.fori_loop(0, SUBLANES // 2, channel_pair, 0)


def _hyena(hy, gm, conv_w, conv_b, hyena_bias, bsz):
    n_tiles = gm.shape[0] // 2
    rows = N_TBLK * bsz
    prow = 2 * rows * SUBLANES
    hy = hy.reshape(hy.shape[0], prow, LANES)
    smem = pl.BlockSpec(memory_space=pltpu.SMEM)
    gspec = lambda off: pl.BlockSpec((None, N_WIN * SUBLANES, LANES), lambda i: (i + off, 0, 0))
    uspec = lambda off: pl.BlockSpec((None, prow, LANES), lambda i: (i + off, 0, 0))
    strip = pltpu.VMEM((STRIP_ROWS, LANES), BF16)
    return pl.pallas_call(
        functools.partial(_hyena_kernel, bsz, n_tiles * SUBLANES),
        grid=(n_tiles,),
        in_specs=[uspec(0), uspec(n_tiles), uspec(2 * n_tiles), gspec(0), gspec(n_tiles),
                  smem, smem, smem],
        out_specs=pl.BlockSpec((None, prow, LANES), lambda i: (i, 0, 0)),
        out_shape=jax.ShapeDtypeStruct((n_tiles, prow, LANES), F32),
        scratch_shapes=[strip, strip, strip, strip],
        compiler_params=pltpu.CompilerParams(dimension_semantics=("arbitrary",),
                                             vmem_limit_bytes=VMEM_LIMIT),
        name="hyena",
    )(hy, hy, hy, gm, gm, conv_w, conv_b, hyena_bias)


def _attn_kernel(q_ref, k_ref, v_ref, sink_ref, o_ref, bias_ref):
    qb = WINDOW
    span = 3 * WINDOW
    n_blk = SEQ // qb

    @pl.when(pl.program_id(0) == 0)
    def _():
        r = lax.broadcasted_iota(jnp.int32, (qb, span), 0)
        c = lax.broadcasted_iota(jnp.int32, (qb, span), 1)
        for j in range(3):
            bias_ref[j] = jnp.where(jnp.abs(c - r - j * WINDOW) <= WINDOW, 0.0, -jnp.inf)

    lane = lax.broadcasted_iota(jnp.int32, (2 * qb, LANES), 1)
    low = lane < HEAD_DIM
    row2 = lax.broadcasted_iota(jnp.int32, (2 * qb, 1), 0)
    sinks = [[jnp.where(row2 < qb, sink_ref[4 * kvh + par], sink_ref[4 * kvh + 2 + par])
              for par in range(2)] for kvh in range(N_KV_HEADS)]

    def block(i, carry):
        q0 = pl.multiple_of(i * qb, qb)
        ws = pl.multiple_of(jnp.clip(q0 - WINDOW, 0, SEQ - span), WINDOW)
        bias1 = bias_ref[(q0 - ws) // WINDOW]
        bias = jnp.concatenate([bias1, bias1], axis=0)
        scores = []
        for kvh in range(N_KV_HEADS):
            qs = jnp.concatenate(
                [q_ref[0, pl.ds(q0, qb), (2 * kvh) * LANES:(2 * kvh + 1) * LANES],
                 q_ref[0, pl.ds(q0, qb), (2 * kvh + 1) * LANES:(2 * kvh + 2) * LANES]], axis=0)
            zero = jnp.zeros_like(qs)
            for par in range(2):
                qm = jnp.where(low, qs, zero) if par == 0 else jnp.where(low, zero, qs)
                ks = (kvh + par) % 2
                km = k_ref[0, pl.ds(ws, span), ks * LANES:(ks + 1) * LANES]
                scores.append(_nt_dot(qm, km) + bias)
        for kvh in range(N_KV_HEADS):
            parts = []
            for par in range(2):
                s = scores[2 * kvh + par]
                vs = 2 * kvh + par
                vm = v_ref[0, pl.ds(ws, span), vs * LANES:(vs + 1) * LANES]
                sink = sinks[kvh][par]
                m = jnp.maximum(jnp.max(s, axis=-1, keepdims=True), sink)
                p = jnp.exp((s - m).astype(BF16))
                parts.append((jnp.dot(p, vm, preferred_element_type=F32), jnp.exp(sink - m)))
            (o_e, sink_e), (o_o, sink_o) = parts
            num = jnp.where(low, o_e, o_o)
            den = jnp.where(low, pltpu.roll(o_e, HEAD_DIM, 1) + sink_e,
                            pltpu.roll(o_o, HEAD_DIM, 1) + sink_o)
            o = num / den
            o_ref[0, pl.ds(q0, qb), (2 * kvh) * LANES:(2 * kvh + 1) * LANES] = o[:qb]
            o_ref[0, pl.ds(q0, qb), (2 * kvh + 1) * LANES:(2 * kvh + 2) * LANES] = o[qb:]
        return carry

    lax.fori_loop(0, n_blk, block, 0, unroll=4)


def _attention(q, k2, v2, sink):
    bsz = q.shape[0]
    return pl.pallas_call(
        _attn_kernel,
        grid=(bsz,),
        in_specs=[
            pl.BlockSpec((1, SEQ, D_ATTN), lambda b: (b, 0, 0)),
            pl.BlockSpec((1, SEQ, 2 * D_KV), lambda b: (b, 0, 0)),
            pl.BlockSpec((1, SEQ, 4 * D_KV), lambda b: (b, 0, 0)),
            pl.BlockSpec(memory_space=pltpu.SMEM),
        ],
        out_specs=pl.BlockSpec((1, SEQ, D_ATTN), lambda b: (b, 0, 0)),
        out_shape=jax.ShapeDtypeStruct((bsz, SEQ, D_ATTN), F32),
        scratch_shapes=[pltpu.VMEM((3, WINDOW, 3 * WINDOW), F32)],
        compiler_params=pltpu.CompilerParams(dimension_semantics=("arbitrary",),
                                             vmem_limit_bytes=VMEM_LIMIT),
        name="attention",
    )(q, k2, v2, sink)


def _out_proj_kernel(x_ref, yh_ref, gh_ref, ya_ref, ga_ref, hg_ref, ag_ref, woh_ref, woa_ref, o_ref):
    def chan_major(ref):
        parts = []
        for kk in range(TOK_TILE // LANES):
            blk, half = kk // 2, kk % 2
            parts.append(ref[:, half, blk, 0, :, :].reshape(D_HYENA, LANES))
        return jnp.concatenate(parts, axis=1)

    yh = chan_major(yh_ref)
    gh = chan_major(gh_ref)
    yh_n = yh * lax.rsqrt(jnp.mean(yh * yh, axis=0, keepdims=True) + EPS) * hg_ref[...]
    yh_g = (yh_n * (gh * jax.nn.sigmoid(gh))).astype(BF16)
    acc = lax.dot_general(yh_g, woh_ref[...], (((0,), (0,)), ((), ())), preferred_element_type=F32)

    ya = ya_ref[0]
    ga = ga_ref[0].astype(F32)
    ya_n = ya * lax.rsqrt(jnp.mean(ya * ya, axis=-1, keepdims=True) + EPS) * ag_ref[...]
    ya_g = (ya_n * (ga * jax.nn.sigmoid(ga))).astype(BF16)
    acc = acc + jnp.dot(ya_g, woa_ref[...], preferred_element_type=F32)
    o_ref[0] = x_ref[0] + acc


def _out_proj(x, yh, hy, ya, ga, hy_out_norm_g, attn_out_norm_g, w_out):
    bsz = x.shape[0]
    n_tt = SEQ // TOK_TILE
    yh = yh.reshape(HY_TILES, 2, N_TBLK, bsz, SUBLANES, LANES)
    woh = w_out[:D_HYENA].astype(BF16)
    woa = w_out[D_HYENA:].astype(BF16)
    const = lambda shape: pl.BlockSpec(shape, lambda b, t: (0,) * len(shape))
    packed = lambda tile_blk: pl.BlockSpec(
        (HY_TILES, 2, TOK_TILE // TBLK, 1, SUBLANES, LANES), lambda b, t: (tile_blk, 0, t, b, 0, 0))
    return pl.pallas_call(
        _out_proj_kernel,
        grid=(bsz, n_tt),
        in_specs=[
            pl.BlockSpec((1, TOK_TILE, D_MODEL), lambda b, t: (b, t, 0)),
            packed(0),
            packed(3),
            pl.BlockSpec((1, TOK_TILE, D_ATTN), lambda b, t: (b, t, 0)),
            pl.BlockSpec((1, TOK_TILE, D_ATTN), lambda b, t: (b, t, 0)),
            const((D_HYENA, 1)),
            const((1, D_ATTN)),
            const(woh.shape),
            const(woa.shape),
        ],
        out_specs=pl.BlockSpec((1, TOK_TILE, D_MODEL), lambda b, t: (b, t, 0)),
        out_shape=jax.ShapeDtypeStruct(x.shape, x.dtype),
        compiler_params=pltpu.CompilerParams(dimension_semantics=("arbitrary", "arbitrary"),
                                             vmem_limit_bytes=VMEM_LIMIT),
        name="out_proj",
    )(x, yh, hy, ya, ga, hy_out_norm_g[:, None], attn_out_norm_g[None], woh, woa)


def kernel(x, norm_g, w_in, conv_w, conv_b, filt_w1, filt_b1, filt_w2, filt_b2, filt_w3, filt_b3,
           filt_w4, filt_sin_freq, hyena_bias, q_norm_g, k_norm_g, attn_sink, hy_out_norm_g,
           attn_out_norm_g, w_out):
    bsz, seq, d_model = x.shape
    assert seq == SEQ and d_model == D_MODEL and bsz % SUBLANES == 0
    assert norm_g.shape[0] == 1, "one layer"
    gm = _filter_windows(filt_w1[0], filt_b1[0], filt_w2[0], filt_b2[0], filt_w3[0], filt_b3[0],
                         filt_w4[0], filt_sin_freq[0])
    hy, q, k2, v2, ga = _in_proj(x, norm_g[0], w_in[0], q_norm_g[0], k_norm_g[0])
    yh = _hyena(hy, gm, conv_w[0], conv_b[0], hyena_bias[0], bsz)
    ya = _attention(q, k2, v2, attn_sink[0])
    return _out_proj(x, yh, hy, ya, ga, hy_out_norm_g[0], attn_out_norm_g[0], w_out[0])
```

```python
import functools
import math

import jax
import jax.numpy as jnp
from jax import lax
from jax.experimental import pallas as pl
from jax.experimental.pallas import tpu as pltpu

F32 = jnp.float32
BF16 = jnp.bfloat16
HI = lax.Precision.HIGHEST

D_MODEL = 1024
SEQ = 2048
D_HYENA = 512
D_ATTN = 512
HEAD_DIM = 64
N_Q_HEADS = 8
N_KV_HEADS = 2
WINDOW = 128
ROPE_THETA = 10000.0
FILTER_HIDDEN = 64
N_BANDS = 16
DECAY_TARGET = 1e-2
FAST_DECAY_PCT = 0.3
SLOW_DECAY_PCT = 1.5
EPS = 1e-6

LANES = 128
SUBLANES = 8
TBLK = 256
N_TBLK = SEQ // TBLK
N_SHIFT = 2 * N_TBLK - 1
N_HY_T = 4 * D_HYENA
N_HY_TILES = N_HY_T // SUBLANES
HY_TILES = D_HYENA // SUBLANES
N_ATT_COLS = D_ATTN + 2 * HEAD_DIM * N_KV_HEADS + D_ATTN
D_KV = HEAD_DIM * N_KV_HEADS

FILT_LPAD = 128
N_POS = 34 * LANES
WIN_SHIFTS = LANES // SUBLANES
N_WIN = 33 * WIN_SHIFTS
FILT_CH = 32
TOK_TILE = 512
VMEM_LIMIT = 56 * 1024 * 1024


def _nt_dot(a, b, precision=None):
    return lax.dot_general(a, b, (((1,), (1,)), ((), ())), preferred_element_type=F32,
                           precision=precision)


def _filter_kernel(w1t_ref, w1c_ref, w1s_ref, b1_ref, w2_ref, b2_ref, w3_ref, b3_ref,
                   w4_ref, fr_ref, hb_ref, o_ref, hid_ref):
    centre = FILT_LPAD + SEQ
    first = jnp.logical_and(pl.program_id(0) == 0, pl.program_id(1) == 0)

    @pl.when(first)
    def _():
        off_c = lax.broadcasted_iota(jnp.int32, (N_POS, 1), 0) - centre
        idx_c = jnp.minimum(jnp.abs(off_c), SEQ - 1).astype(F32)
        t_c = idx_c / (SEQ - 1)
        w_c = (2.0 * math.pi) * idx_c / SEQ
        band = lax.broadcasted_iota(jnp.int32, (1, N_BANDS), 1).astype(F32)
        freqs = 1e-4 + band * ((N_BANDS - 1 - 1e-4) / (N_BANDS - 1))
        ang = w_c * freqs
        fr = fr_ref[...]
        pre = (t_c * w1t_ref[...]
               + jnp.dot(jnp.cos(ang), w1c_ref[...], preferred_element_type=F32, precision=HI)
               - jnp.dot(jnp.sin(ang), w1s_ref[...], preferred_element_type=F32, precision=HI)
               + b1_ref[...])
        h = jnp.sin(fr * pre)
        h = jnp.sin(fr * (jnp.dot(h, w2_ref[...], preferred_element_type=F32, precision=HI) + b2_ref[...]))
        h = jnp.sin(fr * (jnp.dot(h, w3_ref[...], preferred_element_type=F32, precision=HI) + b3_ref[...]))
        hid_ref[...] = h

    hid = hid_ref[...]
    fwd = _nt_dot(w4_ref[0, 0], hid, precision=HI)
    bwd = _nt_dot(w4_ref[0, 1], hid, precision=HI)
    off_r = lax.broadcasted_iota(jnp.int32, (1, N_POS), 1) - centre
    idx_r = jnp.minimum(jnp.abs(off_r), SEQ - 1).astype(F32)
    max_decay = math.log(DECAY_TARGET) / FAST_DECAY_PCT
    min_decay = math.log(DECAY_TARGET) / SLOW_DECAY_PCT
    chan = (pl.program_id(1) * FILT_CH
            + lax.broadcasted_iota(jnp.int32, (FILT_CH, 1), 0)).astype(F32)
    deltas = min_decay + chan * ((max_decay - min_decay) / (D_HYENA - 1))
    decay = jnp.exp(-(idx_r / (SEQ - 1)) * jnp.abs(deltas))
    decay = jnp.where(jnp.abs(off_r) <= SEQ - 1, decay, 0.0)
    val = jnp.where(off_r < 0, bwd, fwd) * decay
    val = val + jnp.where(off_r == 0, hb_ref[0], 0.0)
    for r0 in range(WIN_SHIFTS):
        shifted = val if r0 == 0 else pltpu.roll(val, N_POS - SUBLANES * r0, 1)
        for r1 in range(N_WIN // WIN_SHIFTS):
            o_ref[:, WIN_SHIFTS * r1 + r0, :, :] = (
                shifted[:, r1 * LANES:(r1 + 1) * LANES].reshape(FILT_CH // SUBLANES, SUBLANES, LANES))


def _filter_windows(w1, b1, w2, b2, w3, b3, w4, sin_freq, hyena_bias):
    full = lambda shape: pl.BlockSpec(shape, lambda o, j: (0,) * len(shape))
    w4r = w4.T.reshape(2, 2, D_HYENA, FILTER_HIDDEN)
    args = (w1[0:1], w1[1:1 + N_BANDS], w1[1 + N_BANDS:], b1[None], w2, b2[None], w3, b3[None])
    n_j = D_HYENA // FILT_CH
    tiles = FILT_CH // SUBLANES
    out = pl.pallas_call(
        _filter_kernel,
        grid=(2, n_j),
        in_specs=[full(a.shape) for a in args] + [
            pl.BlockSpec((1, 2, FILT_CH, FILTER_HIDDEN), lambda o, j: (o, 0, j, 0)),
            full((1, FILTER_HIDDEN)),
            pl.BlockSpec((1, FILT_CH, 1), lambda o, j: (o, j, 0)),
        ],
        out_specs=pl.BlockSpec((tiles, N_WIN, SUBLANES, LANES), lambda o, j: (o * n_j + j, 0, 0, 0)),
        out_shape=jax.ShapeDtypeStruct((2 * HY_TILES, N_WIN, SUBLANES, LANES), F32),
        scratch_shapes=[pltpu.VMEM((N_POS, FILTER_HIDDEN), F32)],
        compiler_params=pltpu.CompilerParams(dimension_semantics=("arbitrary", "arbitrary"),
                                             vmem_limit_bytes=VMEM_LIMIT),
        name="filter_windows",
    )(*args, w4r, sin_freq[None], hyena_bias[:, :, None])
    return out.reshape(2 * HY_TILES, N_WIN * SUBLANES, LANES)


N_STREAM = 4
CONV_SUB = 64


def _in_proj_kernel(x_ref, g_ref, wht_ref, cw_ref, wa_ref, seg_ref, cos_ref, sin_ref, qg_ref, kg_ref,
                    hy_ref, q_ref, k_ref, v_ref, ga_ref, h_ref, r0_ref, r1_ref):
    j = pl.program_id(1)
    n_tt = SEQ // TOK_TILE
    n_chunk = SEQ // LANES
    lane = lax.broadcasted_iota(jnp.int32, (TOK_TILE, LANES), 1)
    lane_c = lax.broadcasted_iota(jnp.int32, (CONV_SUB, LANES), 1)

    def normalise():
        for t in range(n_tt):
            x = x_ref[0, t * TOK_TILE:(t + 1) * TOK_TILE, :]
            ms = jnp.mean(x * x, axis=-1, keepdims=True)
            h_ref[t * TOK_TILE:(t + 1) * TOK_TILE, :] = (x * lax.rsqrt(ms + EPS) * g_ref[...]).astype(BF16)

    def project(dst_ref):
        for t in range(n_tt):
            dst_ref[:, t * TOK_TILE:(t + 1) * TOK_TILE] = _nt_dot(
                wht_ref[...], h_ref[t * TOK_TILE:(t + 1) * TOK_TILE, :])

    def emit(src_ref, conv):
        for sub in range(D_HYENA // CONV_SUB):
            rows = slice(sub * CONV_SUB, (sub + 1) * CONV_SUB)
            tiles = slice(sub * CONV_SUB // SUBLANES, (sub + 1) * CONV_SUB // SUBLANES)
            if conv:
                w0, w1, w2, wb = (cw_ref[0, i, rows, :] for i in range(4))
                chunk = lambda k: src_ref[rows, k * LANES:(k + 1) * LANES]
                prev_r = None
                cur = chunk(0)
                cur_r, cur_l = pltpu.roll(cur, 1, 1), pltpu.roll(cur, LANES - 1, 1)
            for k in range(n_chunk):
                if conv:
                    if k + 1 < n_chunk:
                        nxt = chunk(k + 1)
                        nxt_r, nxt_l = pltpu.roll(nxt, 1, 1), pltpu.roll(nxt, LANES - 1, 1)
                    um = jnp.where(lane_c == 0, 0.0 if prev_r is None else prev_r, cur_r)
                    up = jnp.where(lane_c == LANES - 1, nxt_l if k + 1 < n_chunk else 0.0, cur_l)
                    out = w0 * um + w1 * cur + w2 * up + wb
                    prev_r = cur_r
                    if k + 1 < n_chunk:
                        cur, cur_r, cur_l = nxt, nxt_r, nxt_l
                else:
                    out = src_ref[rows, k * LANES:(k + 1) * LANES]
                blk, half = k // 2, k % 2
                hy_ref[tiles, half, blk, 0, :, :] = out.reshape(CONV_SUB // SUBLANES, SUBLANES, LANES)

    def attention_cols():
        t0 = pl.multiple_of(j * TOK_TILE, TOK_TILE)
        att = jnp.dot(h_ref[pl.ds(t0, TOK_TILE), :], wa_ref[...], preferred_element_type=F32)
        cos = cos_ref[...]
        sin = sin_ref[...]
        first_half = (lane % HEAD_DIM) < (HEAD_DIM // 2)

        def norm_rope(t, gain, seg):
            ms_h = jnp.dot((t * t).astype(BF16), seg, preferred_element_type=F32)
            tn = t * lax.rsqrt(ms_h + EPS) * gain
            outs = []
            for i in range(t.shape[1] // LANES):
                c = tn[:, i * LANES:(i + 1) * LANES]
                swapped = jnp.where(first_half, pltpu.roll(c, LANES - HEAD_DIM // 2, 1),
                                    pltpu.roll(c, HEAD_DIM // 2, 1))
                outs.append(c * cos + swapped * sin)
            return outs[0] if len(outs) == 1 else jnp.concatenate(outs, axis=1)

        q = norm_rope(att[:, :D_ATTN], qg_ref[...], seg_ref[...])
        k = norm_rope(att[:, D_ATTN:D_ATTN + D_KV], kg_ref[...], seg_ref[:D_KV, :D_KV])
        v = att[:, D_ATTN + D_KV:D_ATTN + 2 * D_KV]
        q_ref[0] = (q * (HEAD_DIM ** -0.5)).astype(BF16)
        k_ref[0] = jnp.concatenate([k, pltpu.roll(k, HEAD_DIM, 1)], axis=1).astype(BF16)
        v_sw = pltpu.roll(v, HEAD_DIM, 1)
        low = lane < HEAD_DIM
        v_ref[0] = jnp.concatenate([jnp.where(low, v, 1.0), jnp.where(low, 1.0, v_sw),
                                    jnp.where(low, v_sw, 1.0), jnp.where(low, 1.0, v)], axis=1).astype(BF16)
        ga_ref[0] = att[:, D_ATTN + 2 * D_KV:].astype(BF16)

    @pl.when(j == 0)
    def _():
        normalise()
        project(r0_ref)
        attention_cols()

    for step in range(1, N_STREAM):
        @pl.when(j == step)
        def _(step=step):
            project(r1_ref if step % 2 else r0_ref)
            emit(r0_ref if step % 2 else r1_ref, True)
            attention_cols()

    @pl.when(j == N_STREAM)
    def _():
        emit(r1_ref, False)


def _in_proj(x, norm_g, w_in, conv_w, conv_b, q_norm_g, k_norm_g):
    bsz = x.shape[0]
    n_tt = SEQ // TOK_TILE
    assert n_tt == N_STREAM
    o1 = N_STREAM * D_HYENA
    wht = w_in[:, :o1].T.astype(BF16)
    wa = w_in[:, o1:].astype(BF16)
    cw = jnp.concatenate([conv_w.reshape(3, 3, D_HYENA).transpose(1, 0, 2),
                          conv_b.reshape(3, 1, D_HYENA)], axis=1)
    cw = jnp.broadcast_to(cw[..., None], (3, 4, D_HYENA, LANES))
    head = jnp.arange(D_ATTN) // HEAD_DIM
    seg = jnp.where(head[:, None] == head[None, :], 1.0 / HEAD_DIM, 0.0).astype(BF16)
    half = HEAD_DIM // 2
    inv = ROPE_THETA ** (-jnp.arange(half, dtype=F32) / half)
    ang = jnp.arange(SEQ, dtype=F32)[:, None] * inv[None, :]
    cos = jnp.tile(jnp.cos(ang), (1, LANES // half))
    sin = jnp.tile(jnp.concatenate([-jnp.sin(ang), jnp.sin(ang)], axis=1), (1, LANES // HEAD_DIM))
    qg = jnp.tile(q_norm_g, D_ATTN // HEAD_DIM)[None]
    kg = jnp.tile(k_norm_g, D_KV // HEAD_DIM)[None]
    const = lambda shape: pl.BlockSpec(shape, lambda b, j: (0,) * len(shape))
    last = N_STREAM - 1
    tok = lambda width: pl.BlockSpec((1, TOK_TILE, width), lambda b, j: (b, jnp.minimum(j, last), 0))
    return pl.pallas_call(
        _in_proj_kernel,
        grid=(bsz, N_STREAM + 1),
        in_specs=[
            pl.BlockSpec((1, SEQ, D_MODEL), lambda b, j: (b, 0, 0)),
            const((1, D_MODEL)),
            pl.BlockSpec((D_HYENA, D_MODEL), lambda b, j: (jnp.minimum(j, last), 0)),
            pl.BlockSpec((1, 4, D_HYENA, LANES), lambda b, j: (jnp.clip(j - 1, 0, 2), 0, 0, 0)),
            const(wa.shape),
            const(seg.shape),
            pl.BlockSpec((TOK_TILE, LANES), lambda b, j: (jnp.minimum(j, last), 0)),
            pl.BlockSpec((TOK_TILE, LANES), lambda b, j: (jnp.minimum(j, last), 0)),
            const((1, D_ATTN)),
            const((1, D_KV)),
        ],
        out_specs=[
            pl.BlockSpec((HY_TILES, 2, N_TBLK, 1, SUBLANES, LANES),
                         lambda b, j: (jnp.maximum(j - 1, 0), 0, 0, b, 0, 0)),
            tok(D_ATTN), tok(2 * D_KV), tok(4 * D_KV), tok(D_ATTN),
        ],
        out_shape=[
            jax.ShapeDtypeStruct((N_HY_TILES, 2, N_TBLK, bsz, SUBLANES, LANES), F32),
            jax.ShapeDtypeStruct((bsz, SEQ, D_ATTN), BF16),
            jax.ShapeDtypeStruct((bsz, SEQ, 2 * D_KV), BF16),
            jax.ShapeDtypeStruct((bsz, SEQ, 4 * D_KV), BF16),
            jax.ShapeDtypeStruct((bsz, SEQ, D_ATTN), BF16),
        ],
        scratch_shapes=[pltpu.VMEM((SEQ, D_MODEL), BF16),
                        pltpu.VMEM((D_HYENA, SEQ), F32), pltpu.VMEM((D_HYENA, SEQ), F32)],
        compiler_params=pltpu.CompilerParams(dimension_semantics=("arbitrary", "arbitrary"),
                                             vmem_limit_bytes=VMEM_LIMIT),
        name="in_proj",
    )(x, norm_g[None], wht, cw, wa, seg, cos, sin, qg, kg)


STRIP_ROWS = N_SHIFT * TBLK + LANES


def _hyena_kernel(bsz, v_ref, x1_ref, x2_ref, g0_ref, g1_ref, g0n_ref, g1n_ref,
                  o_ref, *strip_refs):
    ct = pl.program_id(0)
    rows = N_TBLK * bsz
    half_rows = rows * SUBLANES
    n_pairs = STRIP_ROWS // 16
    sets = [[strip_refs[0:2], strip_refs[2:4]], [strip_refs[4:6], strip_refs[6:8]]]

    row16 = lax.broadcasted_iota(jnp.int32, (16, LANES), 0)
    lane16 = lax.broadcasted_iota(jnp.int32, (16, LANES), 1)
    keep = lane16 >= (row16 % SUBLANES)

    def build_strip(g_ref, c, strip_ref):
        packed = {}

        def rotated_pair(k):
            if k not in packed:
                tiles = []
                for m in (2 * k, 2 * k - 1):
                    window = g_ref[pl.ds((m + 16) * SUBLANES + c, 1), :]
                    window = jnp.broadcast_to(window, (SUBLANES, LANES))
                    tiles.append(pltpu.roll(window, 0, 1, stride=1, stride_axis=0))
                packed[k] = jnp.concatenate(tiles, axis=0).astype(BF16)
            return packed[k]

        for k in range(n_pairs, 0, -1):
            pair = jnp.where(keep, rotated_pair(k), rotated_pair(k - 8))
            packed.pop(k)
            strip_ref[16 * (n_pairs - k):16 * (n_pairs - k) + 16, :] = pair

    def weights(strip_ref, e):
        return jnp.concatenate([strip_ref[TBLK * e + LANES:TBLK * e + LANES + TBLK, :],
                                strip_ref[TBLK * e:TBLK * e + TBLK, :]], axis=1)

    def load_rows(ref, c):
        lo = ref[pl.ds(c, rows, stride=SUBLANES), :]
        hi = ref[pl.ds(half_rows + c, rows, stride=SUBLANES), :]
        return jnp.concatenate([lo, hi], axis=1)

    def build_set(s, c0, g_refs):
        for i in range(2):
            for order in range(2):
                build_strip(g_refs[order], c0 + i, sets[s][i][order])

    def long_convs(us, strips):
        ubs = [u.astype(BF16) for u in us]
        n = len(us)
        dot = lambda i, lhs, e: jnp.dot(lhs, weights(strips[i], e), preferred_element_type=F32)
        acc_p = [None] * n
        acc_n = [None] * n
        for d in range(N_TBLK - 1, 0, -1):
            for i in range(n):
                res = dot(i, ubs[i][0:rows - bsz * d], N_TBLK - 1 - d)
                acc_p[i] = res if acc_p[i] is None else jnp.concatenate(
                    [res[:bsz], res[bsz:] + acc_p[i]], axis=0)
            for i in range(n):
                res = dot(i, ubs[i][bsz * d:rows], N_TBLK - 1 + d)
                acc_n[i] = res if acc_n[i] is None else jnp.concatenate(
                    [res[:-bsz] + acc_n[i], res[-bsz:]], axis=0)
        outs = []
        for i in range(n):
            res = dot(i, ubs[i], N_TBLK - 1)
            outs.append(jnp.concatenate(
                [res[:bsz] + acc_n[i][:bsz],
                 res[bsz:rows - bsz] + acc_n[i][bsz:] + acc_p[i][:rows - 2 * bsz],
                 res[rows - bsz:] + acc_p[i][rows - 2 * bsz:]], axis=0))
        return outs

    def process_pair(c0, strips):
        cs = (c0, c0 + 1)
        hv = [load_rows(v_ref, c) for c in cs]
        conv = long_convs(hv, [strips[i][0] for i in range(2)])
        z = [load_rows(x1_ref, c) * conv[i] for i, c in enumerate(cs)]
        conv = long_convs(z, [strips[i][1] for i in range(2)])
        for i, c in enumerate(cs):
            y = load_rows(x2_ref, c) * conv[i]
            o_ref[pl.ds(c, rows, stride=SUBLANES), :] = y[:, :LANES]
            o_ref[pl.ds(half_rows + c, rows, stride=SUBLANES), :] = y[:, LANES:]

    @pl.when(ct == 0)
    def _():
        build_set(0, 0, (g0_ref, g1_ref))

    n_cp = SUBLANES // 2
    for p in range(n_cp):
        if p + 1 < n_cp:
            build_set((p + 1) % 2, 2 * (p + 1), (g0_ref, g1_ref))
        else:
            build_set((p + 1) % 2, 0, (g0n_ref, g1n_ref))
        process_pair(2 * p, sets[p % 2])


def _hyena(hy, gm, bsz):
    n_tiles = gm.shape[0] // 2
    rows = N_TBLK * bsz
    prow = 2 * rows * SUBLANES
    hy = hy.reshape(hy.shape[0], prow, LANES)
    gspec = lambda off: pl.BlockSpec((None, N_WIN * SUBLANES, LANES), lambda i: (i + off, 0, 0))
    gnext = lambda off: pl.BlockSpec((None, N_WIN * SUBLANES, LANES),
                                     lambda i: (jnp.minimum(i + 1, n_tiles - 1) + off, 0, 0))
    uspec = lambda off: pl.BlockSpec((None, prow, LANES), lambda i: (i + off, 0, 0))
    strip = pltpu.VMEM((STRIP_ROWS, LANES), BF16)
    return pl.pallas_call(
        functools.partial(_hyena_kernel, bsz),
        grid=(n_tiles,),
        in_specs=[uspec(0), uspec(n_tiles), uspec(2 * n_tiles), gspec(0), gspec(n_tiles),
                  gnext(0), gnext(n_tiles)],
        out_specs=pl.BlockSpec((None, prow, LANES), lambda i: (i, 0, 0)),
        out_shape=jax.ShapeDtypeStruct((n_tiles, prow, LANES), F32),
        scratch_shapes=[strip] * 8,
        compiler_params=pltpu.CompilerParams(dimension_semantics=("arbitrary",),
                                             vmem_limit_bytes=VMEM_LIMIT),
        name="hyena",
    )(hy, hy, hy, gm, gm, gm, gm)


def _attn_kernel(q_ref, k_ref, v_ref, sink_ref, o_ref, bias_ref):
    qb = WINDOW
    span = 3 * WINDOW
    n_blk = SEQ // qb

    @pl.when(pl.program_id(0) == 0)
    def _():
        r = lax.broadcasted_iota(jnp.int32, (qb, span), 0)
        c = lax.broadcasted_iota(jnp.int32, (qb, span), 1)
        for j in range(3):
            bias_ref[j] = jnp.where(jnp.abs(c - r - j * WINDOW) <= WINDOW, 0.0, -jnp.inf)

    lane = lax.broadcasted_iota(jnp.int32, (2 * qb, LANES), 1)
    low = lane < HEAD_DIM
    row2 = lax.broadcasted_iota(jnp.int32, (2 * qb, 1), 0)
    sinks = [[jnp.where(row2 < qb, sink_ref[4 * kvh + par], sink_ref[4 * kvh + 2 + par])
              for par in range(2)] for kvh in range(N_KV_HEADS)]

    def block(i, carry):
        q0 = pl.multiple_of(i * qb, qb)
        ws = pl.multiple_of(jnp.clip(q0 - WINDOW, 0, SEQ - span), WINDOW)
        bias1 = bias_ref[(q0 - ws) // WINDOW]
        bias = jnp.concatenate([bias1, bias1], axis=0)
        scores = []
        for kvh in range(N_KV_HEADS):
            qs = jnp.concatenate(
                [q_ref[0, pl.ds(q0, qb), (2 * kvh) * LANES:(2 * kvh + 1) * LANES],
                 q_ref[0, pl.ds(q0, qb), (2 * kvh + 1) * LANES:(2 * kvh + 2) * LANES]], axis=0)
            zero = jnp.zeros_like(qs)
            for par in range(2):
                qm = jnp.where(low, qs, zero) if par == 0 else jnp.where(low, zero, qs)
                ks = (kvh + par) % 2
                km = k_ref[0, pl.ds(ws, span), ks * LANES:(ks + 1) * LANES]
                scores.append(_nt_dot(qm, km) + bias)
        for kvh in range(N_KV_HEADS):
            parts = []
            for par in range(2):
                s = scores[2 * kvh + par]
                vs = 2 * kvh + par
                vm = v_ref[0, pl.ds(ws, span), vs * LANES:(vs + 1) * LANES]
                sink = sinks[kvh][par]
                m = jnp.maximum(jnp.max(s, axis=-1, keepdims=True), sink)
                p = jnp.exp((s - m).astype(BF16))
                parts.append((jnp.dot(p, vm, preferred_element_type=F32), jnp.exp(sink - m)))
            (o_e, sink_e), (o_o, sink_o) = parts
            num = jnp.where(low, o_e, o_o)
            den = jnp.where(low, pltpu.roll(o_e, HEAD_DIM, 1) + sink_e,
                            pltpu.roll(o_o, HEAD_DIM, 1) + sink_o)
            o = num / den
            o_ref[0, pl.ds(q0, qb), (2 * kvh) * LANES:(2 * kvh + 1) * LANES] = o[:qb]
            o_ref[0, pl.ds(q0, qb), (2 * kvh + 1) * LANES:(2 * kvh + 2) * LANES] = o[qb:]
        return carry

    lax.fori_loop(0, n_blk, block, 0, unroll=4)


def _attention(q, k2, v2, sink):
    bsz = q.shape[0]
    return pl.pallas_call(
        _attn_kernel,
        grid=(bsz,),
        in_specs=[
            pl.BlockSpec((1, SEQ, D_ATTN), lambda b: (b, 0, 0)),
            pl.BlockSpec((1, SEQ, 2 * D_KV), lambda b: (b, 0, 0)),
            pl.BlockSpec((1, SEQ, 4 * D_KV), lambda b: (b, 0, 0)),
            pl.BlockSpec(memory_space=pltpu.SMEM),
        ],
        out_specs=pl.BlockSpec((1, SEQ, D_ATTN), lambda b: (b, 0, 0)),
        out_shape=jax.ShapeDtypeStruct((bsz, SEQ, D_ATTN), F32),
        scratch_shapes=[pltpu.VMEM((3, WINDOW, 3 * WINDOW), F32)],
        compiler_params=pltpu.CompilerParams(dimension_semantics=("arbitrary",),
                                             vmem_limit_bytes=VMEM_LIMIT),
        name="attention",
    )(q, k2, v2, sink)


def _out_proj_kernel(x_ref, yh_ref, gh_ref, ya_ref, ga_ref, hg_ref, ag_ref, woh_ref, woa_ref, o_ref):
    def chan_major(ref):
        parts = []
        for kk in range(TOK_TILE // LANES):
            blk, half = kk // 2, kk % 2
            parts.append(ref[:, half, blk, 0, :, :].reshape(D_HYENA, LANES))
        return jnp.concatenate(parts, axis=1)

    yh = chan_major(yh_ref)
    gh = chan_major(gh_ref)
    yh_n = yh * lax.rsqrt(jnp.mean(yh * yh, axis=0, keepdims=True) + EPS) * hg_ref[...]
    yh_g = (yh_n * (gh * jax.nn.sigmoid(gh))).astype(BF16)
    acc = lax.dot_general(yh_g, woh_ref[...], (((0,), (0,)), ((), ())), preferred_element_type=F32)

    ya = ya_ref[0]
    ga = ga_ref[0].astype(F32)
    ya_n = ya * lax.rsqrt(jnp.mean(ya * ya, axis=-1, keepdims=True) + EPS) * ag_ref[...]
    ya_g = (ya_n * (ga * jax.nn.sigmoid(ga))).astype(BF16)
    acc = acc + jnp.dot(ya_g, woa_ref[...], preferred_element_type=F32)
    o_ref[0] = x_ref[0] + acc


def _out_proj(x, yh, hy, ya, ga, hy_out_norm_g, attn_out_norm_g, w_out):
    bsz = x.shape[0]
    n_tt = SEQ // TOK_TILE
    yh = yh.reshape(HY_TILES, 2, N_TBLK, bsz, SUBLANES, LANES)
    woh = w_out[:D_HYENA].astype(BF16)
    woa = w_out[D_HYENA:].astype(BF16)
    const = lambda shape: pl.BlockSpec(shape, lambda b, t: (0,) * len(shape))
    packed = lambda tile_blk: pl.BlockSpec(
        (HY_TILES, 2, TOK_TILE // TBLK, 1, SUBLANES, LANES), lambda b, t: (tile_blk, 0, t, b, 0, 0))
    return pl.pallas_call(
        _out_proj_kernel,
        grid=(bsz, n_tt),
        in_specs=[
            pl.BlockSpec((1, TOK_TILE, D_MODEL), lambda b, t: (b, t, 0)),
            packed(0),
            packed(3),
            pl.BlockSpec((1, TOK_TILE, D_ATTN), lambda b, t: (b, t, 0)),
            pl.BlockSpec((1, TOK_TILE, D_ATTN), lambda b, t: (b, t, 0)),
            const((D_HYENA, 1)),
            const((1, D_ATTN)),
            const(woh.shape),
            const(woa.shape),
        ],
        out_specs=pl.BlockSpec((1, TOK_TILE, D_MODEL), lambda b, t: (b, t, 0)),
        out_shape=jax.ShapeDtypeStruct(x.shape, x.dtype),
        compiler_params=pltpu.CompilerParams(dimension_semantics=("arbitrary", "arbitrary"),
                                             vmem_limit_bytes=VMEM_LIMIT),
        name="out_proj",
    )(x, yh, hy, ya, ga, hy_out_norm_g[:, None], attn_out_norm_g[None], woh, woa)


def kernel(x, norm_g, w_in, conv_w, conv_b, filt_w1, filt_b1, filt_w2, filt_b2, filt_w3, filt_b3,
           filt_w4, filt_sin_freq, hyena_bias, q_norm_g, k_norm_g, attn_sink, hy_out_norm_g,
           attn_out_norm_g, w_out):
    bsz, seq, d_model = x.shape
    assert seq == SEQ and d_model == D_MODEL and bsz % SUBLANES == 0
    assert norm_g.shape[0] == 1, "one layer"
    gm = _filter_windows(filt_w1[0], filt_b1[0], filt_w2[0], filt_b2[0], filt_w3[0], filt_b3[0],
                         filt_w4[0], filt_sin_freq[0], hyena_bias[0])
    hy, q, k2, v2, ga = _in_proj(x, norm_g[0], w_in[0], conv_w[0], conv_b[0], q_norm_g[0], k_norm_g[0])
    yh = _hyena(hy, gm, bsz)
    ya = _attention(q, k2, v2, attn_sink[0])
    return _out_proj(x, yh, hy, ya, ga, hy_out_norm_g[0], attn_out_norm_g[0], w_out[0])
```

```python
import functools
import math

import jax
import jax.numpy as jnp
from jax import lax
from jax.experimental import pallas as pl
from jax.experimental.pallas import tpu as pltpu

F32 = jnp.float32
BF16 = jnp.bfloat16
HI = lax.Precision.HIGHEST

D_MODEL = 1024
SEQ = 2048
D_HYENA = 512
D_ATTN = 512
HEAD_DIM = 64
N_Q_HEADS = 8
N_KV_HEADS = 2
WINDOW = 128
ROPE_THETA = 10000.0
FILTER_HIDDEN = 64
N_BANDS = 16
DECAY_TARGET = 1e-2
FAST_DECAY_PCT = 0.3
SLOW_DECAY_PCT = 1.5
EPS = 1e-6

LANES = 128
SUBLANES = 8
TBLK = 256
N_TBLK = SEQ // TBLK
N_SHIFT = 2 * N_TBLK - 1
N_HY_T = 4 * D_HYENA
N_HY_TILES = N_HY_T // SUBLANES
HY_TILES = D_HYENA // SUBLANES
N_ATT_COLS = D_ATTN + 2 * HEAD_DIM * N_KV_HEADS + D_ATTN
D_KV = HEAD_DIM * N_KV_HEADS

FILT_LPAD = 128
N_POS = 34 * LANES
WIN_SHIFTS = LANES // SUBLANES
N_WIN = 33 * WIN_SHIFTS
FILT_CH = 32
TOK_TILE = 512
VMEM_LIMIT = 56 * 1024 * 1024


def _nt_dot(a, b, precision=None):
    return lax.dot_general(a, b, (((1,), (1,)), ((), ())), preferred_element_type=F32,
                           precision=precision)


def _filter_kernel(w1t_ref, w1c_ref, w1s_ref, b1_ref, w2_ref, b2_ref, w3_ref, b3_ref,
                   w4_ref, fr_ref, hb_ref, o_ref, hid_ref):
    centre = FILT_LPAD + SEQ
    first = jnp.logical_and(pl.program_id(0) == 0, pl.program_id(1) == 0)

    @pl.when(first)
    def _():
        off_c = lax.broadcasted_iota(jnp.int32, (N_POS, 1), 0) - centre
        idx_c = jnp.minimum(jnp.abs(off_c), SEQ - 1).astype(F32)
        t_c = idx_c / (SEQ - 1)
        w_c = (2.0 * math.pi) * idx_c / SEQ
        band = lax.broadcasted_iota(jnp.int32, (1, N_BANDS), 1).astype(F32)
        freqs = 1e-4 + band * ((N_BANDS - 1 - 1e-4) / (N_BANDS - 1))
        ang = w_c * freqs
        fr = fr_ref[...]
        pre = (t_c * w1t_ref[...]
               + jnp.dot(jnp.cos(ang), w1c_ref[...], preferred_element_type=F32, precision=HI)
               - jnp.dot(jnp.sin(ang), w1s_ref[...], preferred_element_type=F32, precision=HI)
               + b1_ref[...])
        h = jnp.sin(fr * pre)
        h = jnp.sin(fr * (jnp.dot(h, w2_ref[...], preferred_element_type=F32, precision=HI) + b2_ref[...]))
        h = jnp.sin(fr * (jnp.dot(h, w3_ref[...], preferred_element_type=F32, precision=HI) + b3_ref[...]))
        hid_ref[...] = h

    hid = hid_ref[...]
    fwd = _nt_dot(w4_ref[0, 0], hid, precision=HI)
    bwd = _nt_dot(w4_ref[0, 1], hid, precision=HI)
    off_r = lax.broadcasted_iota(jnp.int32, (1, N_POS), 1) - centre
    idx_r = jnp.minimum(jnp.abs(off_r), SEQ - 1).astype(F32)
    max_decay = math.log(DECAY_TARGET) / FAST_DECAY_PCT
    min_decay = math.log(DECAY_TARGET) / SLOW_DECAY_PCT
    chan = (pl.program_id(1) * FILT_CH
            + lax.broadcasted_iota(jnp.int32, (FILT_CH, 1), 0)).astype(F32)
    deltas = min_decay + chan * ((max_decay - min_decay) / (D_HYENA - 1))
    decay = jnp.exp(-(idx_r / (SEQ - 1)) * jnp.abs(deltas))
    decay = jnp.where(jnp.abs(off_r) <= SEQ - 1, decay, 0.0)
    val = jnp.where(off_r < 0, bwd, fwd) * decay
    val = val + jnp.where(off_r == 0, hb_ref[0], 0.0)
    for r0 in range(WIN_SHIFTS):
        shifted = val if r0 == 0 else pltpu.roll(val, N_POS - SUBLANES * r0, 1)
        for r1 in range(N_WIN // WIN_SHIFTS):
            o_ref[:, WIN_SHIFTS * r1 + r0, :, :] = (
                shifted[:, r1 * LANES:(r1 + 1) * LANES].reshape(FILT_CH // SUBLANES, SUBLANES, LANES))


def _filter_windows(w1, b1, w2, b2, w3, b3, w4, sin_freq, hyena_bias):
    full = lambda shape: pl.BlockSpec(shape, lambda o, j: (0,) * len(shape))
    w4r = w4.T.reshape(2, 2, D_HYENA, FILTER_HIDDEN)
    args = (w1[0:1], w1[1:1 + N_BANDS], w1[1 + N_BANDS:], b1[None], w2, b2[None], w3, b3[None])
    n_j = D_HYENA // FILT_CH
    tiles = FILT_CH // SUBLANES
    out = pl.pallas_call(
        _filter_kernel,
        grid=(2, n_j),
        in_specs=[full(a.shape) for a in args] + [
            pl.BlockSpec((1, 2, FILT_CH, FILTER_HIDDEN), lambda o, j: (o, 0, j, 0)),
            full((1, FILTER_HIDDEN)),
            pl.BlockSpec((1, FILT_CH, 1), lambda o, j: (o, j, 0)),
        ],
        out_specs=pl.BlockSpec((tiles, N_WIN, SUBLANES, LANES), lambda o, j: (o * n_j + j, 0, 0, 0)),
        out_shape=jax.ShapeDtypeStruct((2 * HY_TILES, N_WIN, SUBLANES, LANES), F32),
        scratch_shapes=[pltpu.VMEM((N_POS, FILTER_HIDDEN), F32)],
        compiler_params=pltpu.CompilerParams(dimension_semantics=("arbitrary", "arbitrary"),
                                             vmem_limit_bytes=VMEM_LIMIT),
        name="filter_windows",
    )(*args, w4r, sin_freq[None], hyena_bias[:, :, None])
    return out.reshape(2 * HY_TILES, N_WIN * SUBLANES, LANES)


N_STREAM = 4
CONV_SUB = 64


def _in_proj_kernel(x_ref, g_ref, wht_ref, cw_ref, wa_ref, seg_ref, cos_ref, sin_ref, qg_ref, kg_ref,
                    hy_ref, q_ref, k_ref, v_ref, ga_ref, h_ref, r0_ref, r1_ref):
    bb = pl.program_id(0)
    j = pl.program_id(1)
    cur = bb % 2
    prev = 1 - cur
    n_tt = SEQ // TOK_TILE
    n_chunk = SEQ // LANES
    lane = lax.broadcasted_iota(jnp.int32, (TOK_TILE, LANES), 1)
    lane_c = lax.broadcasted_iota(jnp.int32, (CONV_SUB, LANES), 1)

    def normalise():
        x = x_ref[0]
        ms = jnp.mean(x * x, axis=-1, keepdims=True)
        t0 = pl.multiple_of(j * TOK_TILE, TOK_TILE)
        h_ref[cur, pl.ds(t0, TOK_TILE), :] = (x * lax.rsqrt(ms + EPS) * g_ref[...]).astype(BF16)

    def project(dst_ref):
        for t in range(n_tt):
            dst_ref[:, t * TOK_TILE:(t + 1) * TOK_TILE] = _nt_dot(
                wht_ref[...], h_ref[prev, t * TOK_TILE:(t + 1) * TOK_TILE, :])

    def emit(src_ref, conv):
        for sub in range(D_HYENA // CONV_SUB):
            rows = slice(sub * CONV_SUB, (sub + 1) * CONV_SUB)
            tiles = slice(sub * CONV_SUB // SUBLANES, (sub + 1) * CONV_SUB // SUBLANES)
            if conv:
                w0, w1, w2, wb = (cw_ref[0, i, rows, :] for i in range(4))
                chunk = lambda k: src_ref[rows, k * LANES:(k + 1) * LANES]
                prev_r = None
                cur = chunk(0)
                cur_r, cur_l = pltpu.roll(cur, 1, 1), pltpu.roll(cur, LANES - 1, 1)
            for k in range(n_chunk):
                if conv:
                    if k + 1 < n_chunk:
                        nxt = chunk(k + 1)
                        nxt_r, nxt_l = pltpu.roll(nxt, 1, 1), pltpu.roll(nxt, LANES - 1, 1)
                    um = jnp.where(lane_c == 0, 0.0 if prev_r is None else prev_r, cur_r)
                    up = jnp.where(lane_c == LANES - 1, nxt_l if k + 1 < n_chunk else 0.0, cur_l)
                    out = w0 * um + w1 * cur + w2 * up + wb
                    prev_r = cur_r
                    if k + 1 < n_chunk:
                        cur, cur_r, cur_l = nxt, nxt_r, nxt_l
                else:
                    out = src_ref[rows, k * LANES:(k + 1) * LANES]
                blk, half = k // 2, k % 2
                hy_ref[tiles, half, blk, 0, :, :] = out.reshape(CONV_SUB // SUBLANES, SUBLANES, LANES)

    def attention_cols(t):
        att = jnp.dot(h_ref[prev, t * TOK_TILE:(t + 1) * TOK_TILE, :], wa_ref[...],
                      preferred_element_type=F32)
        cos = cos_ref[...]
        sin = sin_ref[...]
        first_half = (lane % HEAD_DIM) < (HEAD_DIM // 2)

        def norm_rope(t, gain, seg):
            ms_h = jnp.dot((t * t).astype(BF16), seg, preferred_element_type=F32)
            tn = t * lax.rsqrt(ms_h + EPS) * gain
            outs = []
            for i in range(t.shape[1] // LANES):
                c = tn[:, i * LANES:(i + 1) * LANES]
                swapped = jnp.where(first_half, pltpu.roll(c, LANES - HEAD_DIM // 2, 1),
                                    pltpu.roll(c, HEAD_DIM // 2, 1))
                outs.append(c * cos + swapped * sin)
            return outs[0] if len(outs) == 1 else jnp.concatenate(outs, axis=1)

        q = norm_rope(att[:, :D_ATTN], qg_ref[...], seg_ref[...])
        k = norm_rope(att[:, D_ATTN:D_ATTN + D_KV], kg_ref[...], seg_ref[:D_KV, :D_KV])
        v = att[:, D_ATTN + D_KV:D_ATTN + 2 * D_KV]
        q_ref[0] = (q * (HEAD_DIM ** -0.5)).astype(BF16)
        k_ref[0] = jnp.concatenate([k, pltpu.roll(k, HEAD_DIM, 1)], axis=1).astype(BF16)
        v_sw = pltpu.roll(v, HEAD_DIM, 1)
        low = lane < HEAD_DIM
        v_ref[0] = jnp.concatenate([jnp.where(low, v, 1.0), jnp.where(low, 1.0, v_sw),
                                    jnp.where(low, v_sw, 1.0), jnp.where(low, 1.0, v)], axis=1).astype(BF16)
        ga_ref[0] = att[:, D_ATTN + 2 * D_KV:].astype(BF16)

    @pl.when(jnp.logical_and(bb == 0, j < N_STREAM))
    def _():
        normalise()

    @pl.when(jnp.logical_and(bb > 0, j == 0))
    def _():
        normalise()
        project(r0_ref)

    for step in range(1, N_STREAM):
        @pl.when(jnp.logical_and(bb > 0, j == step))
        def _(step=step):
            normalise()
            project(r1_ref if step % 2 else r0_ref)
            emit(r0_ref if step % 2 else r1_ref, True)
            attention_cols(step - 1)

    @pl.when(jnp.logical_and(bb > 0, j == N_STREAM))
    def _():
        emit(r1_ref, False)
        attention_cols(N_STREAM - 1)


def _in_proj(x, norm_g, w_in, conv_w, conv_b, q_norm_g, k_norm_g):
    bsz = x.shape[0]
    n_tt = SEQ // TOK_TILE
    assert n_tt == N_STREAM
    o1 = N_STREAM * D_HYENA
    wht = w_in[:, :o1].T.astype(BF16)
    wa = w_in[:, o1:].astype(BF16)
    cw = jnp.concatenate([conv_w.reshape(3, 3, D_HYENA).transpose(1, 0, 2),
                          conv_b.reshape(3, 1, D_HYENA)], axis=1)
    cw = jnp.broadcast_to(cw[..., None], (3, 4, D_HYENA, LANES))
    head = jnp.arange(D_ATTN) // HEAD_DIM
    seg = jnp.where(head[:, None] == head[None, :], 1.0 / HEAD_DIM, 0.0).astype(BF16)
    half = HEAD_DIM // 2
    inv = ROPE_THETA ** (-jnp.arange(half, dtype=F32) / half)
    ang = jnp.arange(SEQ, dtype=F32)[:, None] * inv[None, :]
    cos = jnp.tile(jnp.cos(ang), (1, LANES // half))
    sin = jnp.tile(jnp.concatenate([-jnp.sin(ang), jnp.sin(ang)], axis=1), (1, LANES // HEAD_DIM))
    qg = jnp.tile(q_norm_g, D_ATTN // HEAD_DIM)[None]
    kg = jnp.tile(k_norm_g, D_KV // HEAD_DIM)[None]
    const = lambda shape: pl.BlockSpec(shape, lambda b, j: (0,) * len(shape))
    last = N_STREAM - 1
    out_b = lambda bb: jnp.maximum(bb - 1, 0)
    att_t = lambda bb, j: jnp.where(bb == 0, 0, jnp.clip(j - 1, 0, last))
    tok = lambda width: pl.BlockSpec((1, TOK_TILE, width), lambda bb, j: (out_b(bb), att_t(bb, j), 0))
    return pl.pallas_call(
        _in_proj_kernel,
        grid=(bsz + 1, N_STREAM + 1),
        in_specs=[
            pl.BlockSpec((1, TOK_TILE, D_MODEL),
                         lambda bb, j: (jnp.minimum(bb, bsz - 1), jnp.minimum(j, last), 0)),
            const((1, D_MODEL)),
            pl.BlockSpec((D_HYENA, D_MODEL), lambda b, j: (jnp.minimum(j, last), 0)),
            pl.BlockSpec((1, 4, D_HYENA, LANES), lambda b, j: (jnp.clip(j - 1, 0, 2), 0, 0, 0)),
            const(wa.shape),
            const(seg.shape),
            pl.BlockSpec((TOK_TILE, LANES), lambda bb, j: (att_t(bb, j), 0)),
            pl.BlockSpec((TOK_TILE, LANES), lambda bb, j: (att_t(bb, j), 0)),
            const((1, D_ATTN)),
            const((1, D_KV)),
        ],
        out_specs=[
            pl.BlockSpec((HY_TILES, 2, N_TBLK, 1, SUBLANES, LANES),
                         lambda bb, j: (jnp.where(bb == 0, 0, jnp.maximum(j - 1, 0)), 0, 0, out_b(bb), 0, 0)),
            tok(D_ATTN), tok(2 * D_KV), tok(4 * D_KV), tok(D_ATTN),
        ],
        out_shape=[
            jax.ShapeDtypeStruct((N_HY_TILES, 2, N_TBLK, bsz, SUBLANES, LANES), F32),
            jax.ShapeDtypeStruct((bsz, SEQ, D_ATTN), BF16),
            jax.ShapeDtypeStruct((bsz, SEQ, 2 * D_KV), BF16),
            jax.ShapeDtypeStruct((bsz, SEQ, 4 * D_KV), BF16),
            jax.ShapeDtypeStruct((bsz, SEQ, D_ATTN), BF16),
        ],
        scratch_shapes=[pltpu.VMEM((2, SEQ, D_MODEL), BF16),
                        pltpu.VMEM((D_HYENA, SEQ), F32), pltpu.VMEM((D_HYENA, SEQ), F32)],
        compiler_params=pltpu.CompilerParams(dimension_semantics=("arbitrary", "arbitrary"),
                                             vmem_limit_bytes=VMEM_LIMIT),
        name="in_proj",
    )(x, norm_g[None], wht, cw, wa, seg, cos, sin, qg, kg)


STRIP_ROWS = N_SHIFT * TBLK + LANES


def _hyena_kernel(bsz, v_ref, x1_ref, x2_ref, g0_ref, g1_ref, g0n_ref, g1n_ref,
                  o_ref, *strip_refs):
    ct = pl.program_id(0)
    rows = N_TBLK * bsz
    half_rows = rows * SUBLANES
    n_pairs = STRIP_ROWS // 16
    sets = [[strip_refs[0:2], strip_refs[2:4]], [strip_refs[4:6], strip_refs[6:8]]]

    row16 = lax.broadcasted_iota(jnp.int32, (16, LANES), 0)
    lane16 = lax.broadcasted_iota(jnp.int32, (16, LANES), 1)
    keep = lane16 >= (row16 % SUBLANES)

    def build_strip(g_ref, c, strip_ref):
        packed = {}

        def rotated_pair(k):
            if k not in packed:
                tiles = []
                for m in (2 * k, 2 * k - 1):
                    window = g_ref[pl.ds((m + 16) * SUBLANES + c, 1), :]
                    window = jnp.broadcast_to(window, (SUBLANES, LANES))
                    tiles.append(pltpu.roll(window, 0, 1, stride=1, stride_axis=0))
                packed[k] = jnp.concatenate(tiles, axis=0).astype(BF16)
            return packed[k]

        for k in range(n_pairs, 0, -1):
            pair = jnp.where(keep, rotated_pair(k), rotated_pair(k - 8))
            packed.pop(k)
            strip_ref[16 * (n_pairs - k):16 * (n_pairs - k) + 16, :] = pair

    def weights(strip_ref, e):
        return jnp.concatenate([strip_ref[TBLK * e + LANES:TBLK * e + LANES + TBLK, :],
                                strip_ref[TBLK * e:TBLK * e + TBLK, :]], axis=1)

    def load_rows(ref, c):
        lo = ref[pl.ds(c, rows, stride=SUBLANES), :]
        hi = ref[pl.ds(half_rows + c, rows, stride=SUBLANES), :]
        return jnp.concatenate([lo, hi], axis=1)

    def build_set(s, c0, g_refs):
        for i in range(2):
            for order in range(2):
                build_strip(g_refs[order], c0 + i, sets[s][i][order])

    def long_convs(us, strips):
        ubs = [u.astype(BF16) for u in us]
        n = len(us)
        dot = lambda i, lhs, e: jnp.dot(lhs, weights(strips[i], e), preferred_element_type=F32)
        acc_p = [None] * n
        acc_n = [None] * n
        for d in range(N_TBLK - 1, 0, -1):
            for i in range(n):
                res = dot(i, ubs[i][0:rows - bsz * d], N_TBLK - 1 - d)
                acc_p[i] = res if acc_p[i] is None else jnp.concatenate(
                    [res[:bsz], res[bsz:] + acc_p[i]], axis=0)
            for i in range(n):
                res = dot(i, ubs[i][bsz * d:rows], N_TBLK - 1 + d)
                acc_n[i] = res if acc_n[i] is None else jnp.concatenate(
                    [res[:-bsz] + acc_n[i], res[-bsz:]], axis=0)
        outs = []
        for i in range(n):
            res = dot(i, ubs[i], N_TBLK - 1)
            outs.append(jnp.concatenate(
                [res[:bsz] + acc_n[i][:bsz],
                 res[bsz:rows - bsz] + acc_n[i][bsz:] + acc_p[i][:rows - 2 * bsz],
                 res[rows - bsz:] + acc_p[i][rows - 2 * bsz:]], axis=0))
        return outs

    def process_pair(c0, strips):
        cs = (c0, c0 + 1)
        hv = [load_rows(v_ref, c) for c in cs]
        conv = long_convs(hv, [strips[i][0] for i in range(2)])
        z = [load_rows(x1_ref, c) * conv[i] for i, c in enumerate(cs)]
        conv = long_convs(z, [strips[i][1] for i in range(2)])
        for i, c in enumerate(cs):
            y = load_rows(x2_ref, c) * conv[i]
            o_ref[pl.ds(c, rows, stride=SUBLANES), :] = y[:, :LANES]
            o_ref[pl.ds(half_rows + c, rows, stride=SUBLANES), :] = y[:, LANES:]

    @pl.when(ct == 0)
    def _():
        build_set(0, 0, (g0_ref, g1_ref))

    n_cp = SUBLANES // 2
    for p in range(n_cp):
        if p + 1 < n_cp:
            build_set((p + 1) % 2, 2 * (p + 1), (g0_ref, g1_ref))
        else:
            build_set((p + 1) % 2, 0, (g0n_ref, g1n_ref))
        process_pair(2 * p, sets[p % 2])


def _hyena(hy, gm, bsz):
    n_tiles = gm.shape[0] // 2
    rows = N_TBLK * bsz
    prow = 2 * rows * SUBLANES
    hy = hy.reshape(hy.shape[0], prow, LANES)
    gspec = lambda off: pl.BlockSpec((None, N_WIN * SUBLANES, LANES), lambda i: (i + off, 0, 0))
    gnext = lambda off: pl.BlockSpec((None, N_WIN * SUBLANES, LANES),
                                     lambda i: (jnp.minimum(i + 1, n_tiles - 1) + off, 0, 0))
    uspec = lambda off: pl.BlockSpec((None, prow, LANES), lambda i: (i + off, 0, 0))
    strip = pltpu.VMEM((STRIP_ROWS, LANES), BF16)
    return pl.pallas_call(
        functools.partial(_hyena_kernel, bsz),
        grid=(n_tiles,),
        in_specs=[uspec(0), uspec(n_tiles), uspec(2 * n_tiles), gspec(0), gspec(n_tiles),
                  gnext(0), gnext(n_tiles)],
        out_specs=pl.BlockSpec((None, prow, LANES), lambda i: (i, 0, 0)),
        out_shape=jax.ShapeDtypeStruct((n_tiles, prow, LANES), F32),
        scratch_shapes=[strip] * 8,
        compiler_params=pltpu.CompilerParams(dimension_semantics=("arbitrary",),
                                             vmem_limit_bytes=VMEM_LIMIT),
        name="hyena",
    )(hy, hy, hy, gm, gm, gm, gm)


def _attn_kernel(q_ref, k_ref, v_ref, sink_ref, o_ref, bias_ref):
    qb = WINDOW
    span = 3 * WINDOW
    n_blk = SEQ // qb

    @pl.when(pl.program_id(0) == 0)
    def _():
        r = lax.broadcasted_iota(jnp.int32, (qb, span), 0)
        c = lax.broadcasted_iota(jnp.int32, (qb, span), 1)
        for j in range(3):
            bias_ref[j] = jnp.where(jnp.abs(c - r - j * WINDOW) <= WINDOW, 0.0, -jnp.inf)

    lane = lax.broadcasted_iota(jnp.int32, (2 * qb, LANES), 1)
    low = lane < HEAD_DIM
    row2 = lax.broadcasted_iota(jnp.int32, (2 * qb, 1), 0)
    sinks = [[jnp.where(row2 < qb, sink_ref[4 * kvh + par], sink_ref[4 * kvh + 2 + par])
              for par in range(2)] for kvh in range(N_KV_HEADS)]

    def block(i, carry):
        q0 = pl.multiple_of(i * qb, qb)
        ws = pl.multiple_of(jnp.clip(q0 - WINDOW, 0, SEQ - span), WINDOW)
        bias1 = bias_ref[(q0 - ws) // WINDOW]
        bias = jnp.concatenate([bias1, bias1], axis=0)
        scores = []
        for kvh in range(N_KV_HEADS):
            qs = jnp.concatenate(
                [q_ref[0, pl.ds(q0, qb), (2 * kvh) * LANES:(2 * kvh + 1) * LANES],
                 q_ref[0, pl.ds(q0, qb), (2 * kvh + 1) * LANES:(2 * kvh + 2) * LANES]], axis=0)
            zero = jnp.zeros_like(qs)
            for par in range(2):
                qm = jnp.where(low, qs, zero) if par == 0 else jnp.where(low, zero, qs)
                ks = (kvh + par) % 2
                km = k_ref[0, pl.ds(ws, span), ks * LANES:(ks + 1) * LANES]
                scores.append(_nt_dot(qm, km) + bias)
        for kvh in range(N_KV_HEADS):
            parts = []
            for par in range(2):
                s = scores[2 * kvh + par]
                vs = 2 * kvh + par
                vm = v_ref[0, pl.ds(ws, span), vs * LANES:(vs + 1) * LANES]
                sink = sinks[kvh][par]
                m = jnp.maximum(jnp.max(s, axis=-1, keepdims=True), sink)
                p = jnp.exp((s - m).astype(BF16))
                parts.append((jnp.dot(p, vm, preferred_element_type=F32), jnp.exp(sink - m)))
            (o_e, sink_e), (o_o, sink_o) = parts
            num = jnp.where(low, o_e, o_o)
            den = jnp.where(low, pltpu.roll(o_e, HEAD_DIM, 1) + sink_e,
                            pltpu.roll(o_o, HEAD_DIM, 1) + sink_o)
            o = num / den
            o_ref[0, pl.ds(q0, qb), (2 * kvh) * LANES:(2 * kvh + 1) * LANES] = o[:qb]
            o_ref[0, pl.ds(q0, qb), (2 * kvh + 1) * LANES:(2 * kvh + 2) * LANES] = o[qb:]
        return carry

    lax.fori_loop(0, n_blk, block, 0, unroll=4)


def _attention(q, k2, v2, sink):
    bsz = q.shape[0]
    return pl.pallas_call(
        _attn_kernel,
        grid=(bsz,),
        in_specs=[
            pl.BlockSpec((1, SEQ, D_ATTN), lambda b: (b, 0, 0)),
            pl.BlockSpec((1, SEQ, 2 * D_KV), lambda b: (b, 0, 0)),
            pl.BlockSpec((1, SEQ, 4 * D_KV), lambda b: (b, 0, 0)),
            pl.BlockSpec(memory_space=pltpu.SMEM),
        ],
        out_specs=pl.BlockSpec((1, SEQ, D_ATTN), lambda b: (b, 0, 0)),
        out_shape=jax.ShapeDtypeStruct((bsz, SEQ, D_ATTN), F32),
        scratch_shapes=[pltpu.VMEM((3, WINDOW, 3 * WINDOW), F32)],
        compiler_params=pltpu.CompilerParams(dimension_semantics=("arbitrary",),
                                             vmem_limit_bytes=VMEM_LIMIT),
        name="attention",
    )(q, k2, v2, sink)


def _out_proj_kernel(x_ref, yh_ref, gh_ref, ya_ref, ga_ref, hg_ref, ag_ref, woh_ref, woa_ref, o_ref):
    def chan_major(ref):
        parts = []
        for kk in range(TOK_TILE // LANES):
            blk, half = kk // 2, kk % 2
            parts.append(ref[:, half, blk, 0, :, :].reshape(D_HYENA, LANES))
        return jnp.concatenate(parts, axis=1)

    yh = chan_major(yh_ref)
    gh = chan_major(gh_ref)
    yh_n = yh * lax.rsqrt(jnp.mean(yh * yh, axis=0, keepdims=True) + EPS) * hg_ref[...]
    yh_g = (yh_n * (gh * jax.nn.sigmoid(gh))).astype(BF16)
    acc = lax.dot_general(yh_g, woh_ref[...], (((0,), (0,)), ((), ())), preferred_element_type=F32)

    ya = ya_ref[0]
    ga = ga_ref[0].astype(F32)
    ya_n = ya * lax.rsqrt(jnp.mean(ya * ya, axis=-1, keepdims=True) + EPS) * ag_ref[...]
    ya_g = (ya_n * (ga * jax.nn.sigmoid(ga))).astype(BF16)
    acc = acc + jnp.dot(ya_g, woa_ref[...], preferred_element_type=F32)
    o_ref[0] = x_ref[0] + acc


def _out_proj(x, yh, hy, ya, ga, hy_out_norm_g, attn_out_norm_g, w_out):
    bsz = x.shape[0]
    n_tt = SEQ // TOK_TILE
    yh = yh.reshape(HY_TILES, 2, N_TBLK, bsz, SUBLANES, LANES)
    woh = w_out[:D_HYENA].astype(BF16)
    woa = w_out[D_HYENA:].astype(BF16)
    const = lambda shape: pl.BlockSpec(shape, lambda b, t: (0,) * len(shape))
    packed = lambda tile_blk: pl.BlockSpec(
        (HY_TILES, 2, TOK_TILE // TBLK, 1, SUBLANES, LANES), lambda b, t: (tile_blk, 0, t, b, 0, 0))
    return pl.pallas_call(
        _out_proj_kernel,
        grid=(bsz, n_tt),
        in_specs=[
            pl.BlockSpec((1, TOK_TILE, D_MODEL), lambda b, t: (b, t, 0)),
            packed(0),
            packed(3),
            pl.BlockSpec((1, TOK_TILE, D_ATTN), lambda b, t: (b, t, 0)),
            pl.BlockSpec((1, TOK_TILE, D_ATTN), lambda b, t: (b, t, 0)),
            const((D_HYENA, 1)),
            const((1, D_ATTN)),
            const(woh.shape),
            const(woa.shape),
        ],
        out_specs=pl.BlockSpec((1, TOK_TILE, D_MODEL), lambda b, t: (b, t, 0)),
        out_shape=jax.ShapeDtypeStruct(x.shape, x.dtype),
        compiler_params=pltpu.CompilerParams(dimension_semantics=("arbitrary", "arbitrary"),
                                             vmem_limit_bytes=VMEM_LIMIT),
        name="out_proj",
    )(x, yh, hy, ya, ga, hy_out_norm_g[:, None], attn_out_norm_g[None], woh, woa)


def kernel(x, norm_g, w_in, conv_w, conv_b, filt_w1, filt_b1, filt_w2, filt_b2, filt_w3, filt_b3,
           filt_w4, filt_sin_freq, hyena_bias, q_norm_g, k_norm_g, attn_sink, hy_out_norm_g,
           attn_out_norm_g, w_out):
    bsz, seq, d_model = x.shape
    assert seq == SEQ and d_model == D_MODEL and bsz % SUBLANES == 0
    assert norm_g.shape[0] == 1, "one layer"
    gm = _filter_windows(filt_w1[0], filt_b1[0], filt_w2[0], filt_b2[0], filt_w3[0], filt_b3[0],
                         filt_w4[0], filt_sin_freq[0], hyena_bias[0])
    hy, q, k2, v2, ga = _in_proj(x, norm_g[0], w_in[0], conv_w[0], conv_b[0], q_norm_g[0], k_norm_g[0])
    yh = _hyena(hy, gm, bsz)
    ya = _attention(q, k2, v2, attn_sink[0])
    return _out_proj(x, yh, hy, ya, ga, hy_out_norm_g[0], attn_out_norm_g[0], w_out[0])
```

```python
import functools
import math

import jax
import jax.numpy as jnp
from jax import lax
from jax.experimental import pallas as pl
from jax.experimental.pallas import tpu as pltpu

F32 = jnp.float32
BF16 = jnp.bfloat16
HI = lax.Precision.HIGHEST

D_MODEL = 1024
SEQ = 2048
D_HYENA = 512
D_ATTN = 512
HEAD_DIM = 64
N_Q_HEADS = 8
N_KV_HEADS = 2
WINDOW = 128
ROPE_THETA = 10000.0
FILTER_HIDDEN = 64
N_BANDS = 16
DECAY_TARGET = 1e-2
FAST_DECAY_PCT = 0.3
SLOW_DECAY_PCT = 1.5
EPS = 1e-6

LANES = 128
SUBLANES = 8
TBLK = 256
N_TBLK = SEQ // TBLK
N_SHIFT = 2 * N_TBLK - 1
N_HY_T = 4 * D_HYENA
N_HY_TILES = N_HY_T // SUBLANES
HY_TILES = D_HYENA // SUBLANES
N_ATT_COLS = D_ATTN + 2 * HEAD_DIM * N_KV_HEADS + D_ATTN
D_KV = HEAD_DIM * N_KV_HEADS

FILT_LPAD = 128
N_POS = 34 * LANES
N_WINQ = 33
WIN_PHASES = (0, SUBLANES)
N_WIN = len(WIN_PHASES) * N_WINQ
FILT_CH = 128
TOK_TILE = 512
VMEM_LIMIT = 56 * 1024 * 1024


def _nt_dot(a, b, precision=None):
    return lax.dot_general(a, b, (((1,), (1,)), ((), ())), preferred_element_type=F32,
                           precision=precision)


def _filter_kernel(w1t_ref, w1c_ref, w1s_ref, b1_ref, w2_ref, b2_ref, w3_ref, b3_ref,
                   w4_ref, fr_ref, hb_ref, o_ref, hid_ref):
    centre = FILT_LPAD + SEQ
    first = jnp.logical_and(pl.program_id(0) == 0, pl.program_id(1) == 0)

    @pl.when(first)
    def _():
        off_c = lax.broadcasted_iota(jnp.int32, (N_POS, 1), 0) - centre
        idx_c = jnp.minimum(jnp.abs(off_c), SEQ - 1).astype(F32)
        t_c = idx_c / (SEQ - 1)
        w_c = (2.0 * math.pi) * idx_c / SEQ
        band = lax.broadcasted_iota(jnp.int32, (1, N_BANDS), 1).astype(F32)
        freqs = 1e-4 + band * ((N_BANDS - 1 - 1e-4) / (N_BANDS - 1))
        ang = w_c * freqs
        fr = fr_ref[...]
        pre = (t_c * w1t_ref[...]
               + jnp.dot(jnp.cos(ang), w1c_ref[...], preferred_element_type=F32, precision=HI)
               - jnp.dot(jnp.sin(ang), w1s_ref[...], preferred_element_type=F32, precision=HI)
               + b1_ref[...])
        h = jnp.sin(fr * pre)
        h = jnp.sin(fr * (jnp.dot(h, w2_ref[...], preferred_element_type=F32, precision=HI) + b2_ref[...]))
        h = jnp.sin(fr * (jnp.dot(h, w3_ref[...], preferred_element_type=F32, precision=HI) + b3_ref[...]))
        hid_ref[...] = h

    hid = hid_ref[...]
    fwd = _nt_dot(w4_ref[0, 0], hid, precision=HI)
    bwd = _nt_dot(w4_ref[0, 1], hid, precision=HI)
    off_r = lax.broadcasted_iota(jnp.int32, (1, N_POS), 1) - centre
    idx_r = jnp.minimum(jnp.abs(off_r), SEQ - 1).astype(F32)
    max_decay = math.log(DECAY_TARGET) / FAST_DECAY_PCT
    min_decay = math.log(DECAY_TARGET) / SLOW_DECAY_PCT
    chan = (pl.program_id(1) * FILT_CH
            + lax.broadcasted_iota(jnp.int32, (FILT_CH, 1), 0)).astype(F32)
    deltas = min_decay + chan * ((max_decay - min_decay) / (D_HYENA - 1))
    decay = jnp.exp(-(idx_r / (SEQ - 1)) * jnp.abs(deltas))
    decay = jnp.where(jnp.abs(off_r) <= SEQ - 1, decay, 0.0)
    val = jnp.where(off_r < 0, bwd, fwd) * decay
    val = val + jnp.where(off_r == 0, hb_ref[0], 0.0)
    for p, phase in enumerate(WIN_PHASES):
        shifted = val if phase == 0 else pltpu.roll(val, phase, 1)
        for w in range(N_WINQ):
            o_ref[:, p * N_WINQ + w, :, :] = (
                shifted[:, w * LANES:(w + 1) * LANES].reshape(FILT_CH // SUBLANES, SUBLANES, LANES))


def _filter_windows(w1, b1, w2, b2, w3, b3, w4, sin_freq, hyena_bias):
    full = lambda shape: pl.BlockSpec(shape, lambda o, j: (0,) * len(shape))
    w4r = w4.T.reshape(2, 2, D_HYENA, FILTER_HIDDEN)
    args = (w1[0:1], w1[1:1 + N_BANDS], w1[1 + N_BANDS:], b1[None], w2, b2[None], w3, b3[None])
    n_j = D_HYENA // FILT_CH
    tiles = FILT_CH // SUBLANES
    out = pl.pallas_call(
        _filter_kernel,
        grid=(2, n_j),
        in_specs=[full(a.shape) for a in args] + [
            pl.BlockSpec((1, 2, FILT_CH, FILTER_HIDDEN), lambda o, j: (o, 0, j, 0)),
            full((1, FILTER_HIDDEN)),
            pl.BlockSpec((1, FILT_CH, 1), lambda o, j: (o, j, 0)),
        ],
        out_specs=pl.BlockSpec((tiles, N_WIN, SUBLANES, LANES), lambda o, j: (o * n_j + j, 0, 0, 0)),
        out_shape=jax.ShapeDtypeStruct((2 * HY_TILES, N_WIN, SUBLANES, LANES), F32),
        scratch_shapes=[pltpu.VMEM((N_POS, FILTER_HIDDEN), F32)],
        compiler_params=pltpu.CompilerParams(dimension_semantics=("arbitrary", "arbitrary"),
                                             vmem_limit_bytes=VMEM_LIMIT),
        name="filter_windows",
    )(*args, w4r, sin_freq[None], hyena_bias[:, :, None])
    return out.reshape(2 * HY_TILES, N_WIN * SUBLANES, LANES)


N_STREAM = 4
CONV_SUB = 64


def _in_proj_kernel(x_ref, g_ref, wht_ref, cw_ref, wa_ref, seg_ref, cos_ref, sin_ref, qg_ref, kg_ref,
                    hy_ref, q_ref, k_ref, v_ref, ga_ref, h0_ref, h1_ref, r0_ref, r1_ref):
    bb = pl.program_id(0)
    j = pl.program_id(1)
    n_tt = SEQ // TOK_TILE
    n_chunk = SEQ // LANES
    lane = lax.broadcasted_iota(jnp.int32, (TOK_TILE, LANES), 1)
    lane_c = lax.broadcasted_iota(jnp.int32, (CONV_SUB, LANES), 1)

    def normalise(h_cur):
        x = x_ref[0]
        ms = jnp.mean(x * x, axis=-1, keepdims=True)
        t0 = pl.multiple_of(j * TOK_TILE, TOK_TILE)
        h_cur[pl.ds(t0, TOK_TILE), :] = (x * lax.rsqrt(ms + EPS) * g_ref[...]).astype(BF16)

    def project(h_prev, dst_ref):
        for t in range(n_tt):
            dst_ref[:, t * TOK_TILE:(t + 1) * TOK_TILE] = _nt_dot(
                wht_ref[...], h_prev[t * TOK_TILE:(t + 1) * TOK_TILE, :])

    def emit(src_ref, conv):
        for sub in range(D_HYENA // CONV_SUB):
            rows = slice(sub * CONV_SUB, (sub + 1) * CONV_SUB)
            tiles = slice(sub * CONV_SUB // SUBLANES, (sub + 1) * CONV_SUB // SUBLANES)
            if conv:
                w0, w1, w2, wb = (cw_ref[0, i, rows, :] for i in range(4))
                chunk = lambda k: src_ref[rows, k * LANES:(k + 1) * LANES]
                prev_r = None
                cur = chunk(0)
                cur_r, cur_l = pltpu.roll(cur, 1, 1), pltpu.roll(cur, LANES - 1, 1)
            for k in range(n_chunk):
                if conv:
                    if k + 1 < n_chunk:
                        nxt = chunk(k + 1)
                        nxt_r, nxt_l = pltpu.roll(nxt, 1, 1), pltpu.roll(nxt, LANES - 1, 1)
                    um = jnp.where(lane_c == 0, 0.0 if prev_r is None else prev_r, cur_r)
                    up = jnp.where(lane_c == LANES - 1, nxt_l if k + 1 < n_chunk else 0.0, cur_l)
                    out = w0 * um + w1 * cur + w2 * up + wb
                    prev_r = cur_r
                    if k + 1 < n_chunk:
                        cur, cur_r, cur_l = nxt, nxt_r, nxt_l
                else:
                    out = src_ref[rows, k * LANES:(k + 1) * LANES]
                blk, half = k // 2, k % 2
                hy_ref[tiles, half, blk, 0, :, :] = out.reshape(CONV_SUB // SUBLANES, SUBLANES, LANES)

    def attention_cols(h_prev, t):
        att = jnp.dot(h_prev[t * TOK_TILE:(t + 1) * TOK_TILE, :], wa_ref[...],
                      preferred_element_type=F32)
        cos = cos_ref[...]
        sin = sin_ref[...]
        first_half = (lane % HEAD_DIM) < (HEAD_DIM // 2)

        def norm_rope(t, gain, seg):
            ms_h = jnp.dot((t * t).astype(BF16), seg, preferred_element_type=F32)
            tn = t * lax.rsqrt(ms_h + EPS) * gain
            outs = []
            for i in range(t.shape[1] // LANES):
                c = tn[:, i * LANES:(i + 1) * LANES]
                swapped = jnp.where(first_half, pltpu.roll(c, LANES - HEAD_DIM // 2, 1),
                                    pltpu.roll(c, HEAD_DIM // 2, 1))
                outs.append(c * cos + swapped * sin)
            return outs[0] if len(outs) == 1 else jnp.concatenate(outs, axis=1)

        q = norm_rope(att[:, :D_ATTN], qg_ref[...], seg_ref[...])
        k = norm_rope(att[:, D_ATTN:D_ATTN + D_KV], kg_ref[...], seg_ref[:D_KV, :D_KV])
        v = att[:, D_ATTN + D_KV:D_ATTN + 2 * D_KV]
        q_ref[0] = (q * (HEAD_DIM ** -0.5)).astype(BF16)
        k_ref[0] = jnp.concatenate([k, pltpu.roll(k, HEAD_DIM, 1)], axis=1).astype(BF16)
        v_sw = pltpu.roll(v, HEAD_DIM, 1)
        low = lane < HEAD_DIM
        v_ref[0] = jnp.concatenate([jnp.where(low, v, 1.0), jnp.where(low, 1.0, v_sw),
                                    jnp.where(low, v_sw, 1.0), jnp.where(low, 1.0, v)], axis=1).astype(BF16)
        ga_ref[0] = att[:, D_ATTN + 2 * D_KV:].astype(BF16)

    def steps(h_cur, h_prev):
        @pl.when(jnp.logical_and(bb == 0, j < N_STREAM))
        def _():
            normalise(h_cur)

        @pl.when(jnp.logical_and(bb > 0, j == 0))
        def _():
            normalise(h_cur)
            project(h_prev, r0_ref)

        for step in range(1, N_STREAM):
            @pl.when(jnp.logical_and(bb > 0, j == step))
            def _(step=step):
                attention_cols(h_prev, step - 1)
                normalise(h_cur)
                project(h_prev, r1_ref if step % 2 else r0_ref)
                emit(r0_ref if step % 2 else r1_ref, True)

        @pl.when(jnp.logical_and(bb > 0, j == N_STREAM))
        def _():
            attention_cols(h_prev, N_STREAM - 1)
            emit(r1_ref, False)

    @pl.when(bb % 2 == 0)
    def _():
        steps(h0_ref, h1_ref)

    @pl.when(bb % 2 == 1)
    def _():
        steps(h1_ref, h0_ref)


def _in_proj(x, norm_g, w_in, conv_w, conv_b, q_norm_g, k_norm_g):
    bsz = x.shape[0]
    n_tt = SEQ // TOK_TILE
    assert n_tt == N_STREAM
    o1 = N_STREAM * D_HYENA
    wht = w_in[:, :o1].T.astype(BF16)
    wa = w_in[:, o1:].astype(BF16)
    cw = jnp.concatenate([conv_w.reshape(3, 3, D_HYENA).transpose(1, 0, 2),
                          conv_b.reshape(3, 1, D_HYENA)], axis=1)
    cw = jnp.broadcast_to(cw[..., None], (3, 4, D_HYENA, LANES))
    head = jnp.arange(D_ATTN) // HEAD_DIM
    seg = jnp.where(head[:, None] == head[None, :], 1.0 / HEAD_DIM, 0.0).astype(BF16)
    half = HEAD_DIM // 2
    inv = ROPE_THETA ** (-jnp.arange(half, dtype=F32) / half)
    ang = jnp.arange(SEQ, dtype=F32)[:, None] * inv[None, :]
    cos = jnp.tile(jnp.cos(ang), (1, LANES // half))
    sin = jnp.tile(jnp.concatenate([-jnp.sin(ang), jnp.sin(ang)], axis=1), (1, LANES // HEAD_DIM))
    qg = jnp.tile(q_norm_g, D_ATTN // HEAD_DIM)[None]
    kg = jnp.tile(k_norm_g, D_KV // HEAD_DIM)[None]
    const = lambda shape: pl.BlockSpec(shape, lambda b, j: (0,) * len(shape))
    last = N_STREAM - 1
    out_b = lambda bb: jnp.maximum(bb - 1, 0)
    att_t = lambda bb, j: jnp.where(bb == 0, 0, jnp.clip(j - 1, 0, last))
    tok = lambda width: pl.BlockSpec((1, TOK_TILE, width), lambda bb, j: (out_b(bb), att_t(bb, j), 0))
    return pl.pallas_call(
        _in_proj_kernel,
        grid=(bsz + 1, N_STREAM + 1),
        in_specs=[
            pl.BlockSpec((1, TOK_TILE, D_MODEL),
                         lambda bb, j: (jnp.minimum(bb, bsz - 1), jnp.minimum(j, last), 0)),
            const((1, D_MODEL)),
            pl.BlockSpec((D_HYENA, D_MODEL), lambda b, j: (jnp.minimum(j, last), 0)),
            pl.BlockSpec((1, 4, D_HYENA, LANES), lambda b, j: (jnp.clip(j - 1, 0, 2), 0, 0, 0)),
            const(wa.shape),
            const(seg.shape),
            pl.BlockSpec((TOK_TILE, LANES), lambda bb, j: (att_t(bb, j), 0)),
            pl.BlockSpec((TOK_TILE, LANES), lambda bb, j: (att_t(bb, j), 0)),
            const((1, D_ATTN)),
            const((1, D_KV)),
        ],
        out_specs=[
            pl.BlockSpec((HY_TILES, 2, N_TBLK, 1, SUBLANES, LANES),
                         lambda bb, j: (jnp.where(bb == 0, 0, jnp.maximum(j - 1, 0)), 0, 0, out_b(bb), 0, 0)),
            tok(D_ATTN), tok(2 * D_KV), tok(4 * D_KV), tok(D_ATTN),
        ],
        out_shape=[
            jax.ShapeDtypeStruct((N_HY_TILES, 2, N_TBLK, bsz, SUBLANES, LANES), F32),
            jax.ShapeDtypeStruct((bsz, SEQ, D_ATTN), BF16),
            jax.ShapeDtypeStruct((bsz, SEQ, 2 * D_KV), BF16),
            jax.ShapeDtypeStruct((bsz, SEQ, 4 * D_KV), BF16),
            jax.ShapeDtypeStruct((bsz, SEQ, D_ATTN), BF16),
        ],
        scratch_shapes=[pltpu.VMEM((SEQ, D_MODEL), BF16), pltpu.VMEM((SEQ, D_MODEL), BF16),
                        pltpu.VMEM((D_HYENA, SEQ), F32), pltpu.VMEM((D_HYENA, SEQ), F32)],
        compiler_params=pltpu.CompilerParams(dimension_semantics=("arbitrary", "arbitrary"),
                                             vmem_limit_bytes=VMEM_LIMIT),
        name="in_proj",
    )(x, norm_g[None], wht, cw, wa, seg, cos, sin, qg, kg)


STRIP_ROWS = N_SHIFT * TBLK + LANES


def _hyena_kernel(bsz, v_ref, x1_ref, x2_ref, g0_ref, g1_ref, g0n_ref, g1n_ref,
                  o_ref, *strip_refs):
    ct = pl.program_id(0)
    rows = N_TBLK * bsz
    half_rows = rows * SUBLANES
    n_pairs = STRIP_ROWS // 16
    sets = [[strip_refs[0:2], strip_refs[2:4]], [strip_refs[4:6], strip_refs[6:8]]]

    row16 = lax.broadcasted_iota(jnp.int32, (16, LANES), 0) % SUBLANES
    lane16 = lax.broadcasted_iota(jnp.int32, (16, LANES), 1)
    n_rot = LANES // 16
    inside = [lane16 >= row16] + [lane16 - row16 + 16 * r < LANES for r in range(1, n_rot)]

    def build_strip(g_ref, c, strip_ref):
        cache = {}

        def window_pair(q):
            if ("w", q) not in cache:
                rows = [jnp.broadcast_to(g_ref[pl.ds((p * N_WINQ + q) * SUBLANES + c, 1), :],
                                         (SUBLANES, LANES)) for p in range(len(WIN_PHASES))]
                cache["w", q] = rows
            return cache["w", q]

        def rotated(q, r):
            if (q, r) not in cache:
                tiles = [pltpu.roll(w, (LANES - 16 * r) % LANES, 1, stride=1, stride_axis=0)
                         for w in window_pair(q)]
                cache[q, r] = jnp.concatenate(tiles, axis=0).astype(BF16)
            return cache[q, r]

        for k in range(n_pairs, 0, -1):
            q, r = k // n_rot + 1, k % n_rot
            pair = jnp.where(inside[r], rotated(q, r), rotated(q - 1 if r == 0 else q + 1, r))
            strip_ref[16 * (n_pairs - k):16 * (n_pairs - k) + 16, :] = pair

    def weights(strip_ref, e):
        return jnp.concatenate([strip_ref[TBLK * e + LANES:TBLK * e + LANES + TBLK, :],
                                strip_ref[TBLK * e:TBLK * e + TBLK, :]], axis=1)

    def load_rows(ref, c):
        lo = ref[pl.ds(c, rows, stride=SUBLANES), :]
        hi = ref[pl.ds(half_rows + c, rows, stride=SUBLANES), :]
        return jnp.concatenate([lo, hi], axis=1)

    def build_set(s, c0, g_refs):
        for i in range(2):
            for order in range(2):
                build_strip(g_refs[order], c0 + i, sets[s][i][order])

    def long_convs(us, strips):
        ubs = [u.astype(BF16) for u in us]
        n = len(us)
        dot = lambda i, lhs, e: jnp.dot(lhs, weights(strips[i], e), preferred_element_type=F32)
        acc_p = [None] * n
        acc_n = [None] * n
        for d in range(N_TBLK - 1, 0, -1):
            for i in range(n):
                res = dot(i, ubs[i][0:rows - bsz * d], N_TBLK - 1 - d)
                acc_p[i] = res if acc_p[i] is None else jnp.concatenate(
                    [res[:bsz], res[bsz:] + acc_p[i]], axis=0)
            for i in range(n):
                res = dot(i, ubs[i][bsz * d:rows], N_TBLK - 1 + d)
                acc_n[i] = res if acc_n[i] is None else jnp.concatenate(
                    [res[:-bsz] + acc_n[i], res[-bsz:]], axis=0)
        outs = []
        for i in range(n):
            res = dot(i, ubs[i], N_TBLK - 1)
            outs.append(jnp.concatenate(
                [res[:bsz] + acc_n[i][:bsz],
                 res[bsz:rows - bsz] + acc_n[i][bsz:] + acc_p[i][:rows - 2 * bsz],
                 res[rows - bsz:] + acc_p[i][rows - 2 * bsz:]], axis=0))
        return outs

    def process_pair(c0, strips):
        cs = (c0, c0 + 1)
        hv = [load_rows(v_ref, c) for c in cs]
        conv = long_convs(hv, [strips[i][0] for i in range(2)])
        z = [load_rows(x1_ref, c) * conv[i] for i, c in enumerate(cs)]
        conv = long_convs(z, [strips[i][1] for i in range(2)])
        for i, c in enumerate(cs):
            y = load_rows(x2_ref, c) * conv[i]
            o_ref[pl.ds(c, rows, stride=SUBLANES), :] = y[:, :LANES]
            o_ref[pl.ds(half_rows + c, rows, stride=SUBLANES), :] = y[:, LANES:]

    @pl.when(ct == 0)
    def _():
        build_set(0, 0, (g0_ref, g1_ref))

    n_cp = SUBLANES // 2
    for p in range(n_cp):
        if p + 1 < n_cp:
            build_set((p + 1) % 2, 2 * (p + 1), (g0_ref, g1_ref))
        else:
            build_set((p + 1) % 2, 0, (g0n_ref, g1n_ref))
        process_pair(2 * p, sets[p % 2])


def _hyena(hy, gm, bsz):
    n_tiles = gm.shape[0] // 2
    rows = N_TBLK * bsz
    prow = 2 * rows * SUBLANES
    hy = hy.reshape(hy.shape[0], prow, LANES)
    gspec = lambda off: pl.BlockSpec((None, N_WIN * SUBLANES, LANES), lambda i: (i + off, 0, 0))
    gnext = lambda off: pl.BlockSpec((None, N_WIN * SUBLANES, LANES),
                                     lambda i: (jnp.minimum(i + 1, n_tiles - 1) + off, 0, 0))
    uspec = lambda off: pl.BlockSpec((None, prow, LANES), lambda i: (i + off, 0, 0))
    strip = pltpu.VMEM((STRIP_ROWS, LANES), BF16)
    return pl.pallas_call(
        functools.partial(_hyena_kernel, bsz),
        grid=(n_tiles,),
        in_specs=[uspec(0), uspec(n_tiles), uspec(2 * n_tiles), gspec(0), gspec(n_tiles),
                  gnext(0), gnext(n_tiles)],
        out_specs=pl.BlockSpec((None, prow, LANES), lambda i: (i, 0, 0)),
        out_shape=jax.ShapeDtypeStruct((n_tiles, prow, LANES), F32),
        scratch_shapes=[strip] * 8,
        compiler_params=pltpu.CompilerParams(dimension_semantics=("arbitrary",),
                                             vmem_limit_bytes=VMEM_LIMIT),
        name="hyena",
    )(hy, hy, hy, gm, gm, gm, gm)


def _attn_kernel(q_ref, k_ref, v_ref, sink_ref, o_ref, bias_ref):
    qb = WINDOW
    span = 3 * WINDOW
    n_blk = SEQ // qb

    @pl.when(pl.program_id(0) == 0)
    def _():
        r = lax.broadcasted_iota(jnp.int32, (qb, span), 0)
        c = lax.broadcasted_iota(jnp.int32, (qb, span), 1)
        for j in range(3):
            bias_ref[j] = jnp.where(jnp.abs(c - r - j * WINDOW) <= WINDOW, 0.0, -jnp.inf)

    lane = lax.broadcasted_iota(jnp.int32, (2 * qb, LANES), 1)
    low = lane < HEAD_DIM
    row2 = lax.broadcasted_iota(jnp.int32, (2 * qb, 1), 0)
    sinks = [[jnp.where(row2 < qb, sink_ref[4 * kvh + par], sink_ref[4 * kvh + 2 + par])
              for par in range(2)] for kvh in range(N_KV_HEADS)]

    def block(i, carry):
        q0 = pl.multiple_of(i * qb, qb)
        ws = pl.multiple_of(jnp.clip(q0 - WINDOW, 0, SEQ - span), WINDOW)
        bias1 = bias_ref[(q0 - ws) // WINDOW]
        bias = jnp.concatenate([bias1, bias1], axis=0)
        scores = []
        for kvh in range(N_KV_HEADS):
            qs = jnp.concatenate(
                [q_ref[0, pl.ds(q0, qb), (2 * kvh) * LANES:(2 * kvh + 1) * LANES],
                 q_ref[0, pl.ds(q0, qb), (2 * kvh + 1) * LANES:(2 * kvh + 2) * LANES]], axis=0)
            zero = jnp.zeros_like(qs)
            for par in range(2):
                qm = jnp.where(low, qs, zero) if par == 0 else jnp.where(low, zero, qs)
                ks = (kvh + par) % 2
                km = k_ref[0, pl.ds(ws, span), ks * LANES:(ks + 1) * LANES]
                scores.append(_nt_dot(qm, km) + bias)
        for kvh in range(N_KV_HEADS):
            parts = []
            for par in range(2):
                s = scores[2 * kvh + par]
                vs = 2 * kvh + par
                vm = v_ref[0, pl.ds(ws, span), vs * LANES:(vs + 1) * LANES]
                sink = sinks[kvh][par]
                m = jnp.maximum(jnp.max(s, axis=-1, keepdims=True), sink)
                p = jnp.exp((s - m).astype(BF16))
                parts.append((jnp.dot(p, vm, preferred_element_type=F32), jnp.exp(sink - m)))
            (o_e, sink_e), (o_o, sink_o) = parts
            num = jnp.where(low, o_e, o_o)
            den = jnp.where(low, pltpu.roll(o_e, HEAD_DIM, 1) + sink_e,
                            pltpu.roll(o_o, HEAD_DIM, 1) + sink_o)
            o = num / den
            o_ref[0, pl.ds(q0, qb), (2 * kvh) * LANES:(2 * kvh + 1) * LANES] = o[:qb]
            o_ref[0, pl.ds(q0, qb), (2 * kvh + 1) * LANES:(2 * kvh + 2) * LANES] = o[qb:]
        return carry

    lax.fori_loop(0, n_blk, block, 0, unroll=4)


def _attention(q, k2, v2, sink):
    bsz = q.shape[0]
    return pl.pallas_call(
        _attn_kernel,
        grid=(bsz,),
        in_specs=[
            pl.BlockSpec((1, SEQ, D_ATTN), lambda b: (b, 0, 0)),
            pl.BlockSpec((1, SEQ, 2 * D_KV), lambda b: (b, 0, 0)),
            pl.BlockSpec((1, SEQ, 4 * D_KV), lambda b: (b, 0, 0)),
            pl.BlockSpec(memory_space=pltpu.SMEM),
        ],
        out_specs=pl.BlockSpec((1, SEQ, D_ATTN), lambda b: (b, 0, 0)),
        out_shape=jax.ShapeDtypeStruct((bsz, SEQ, D_ATTN), F32),
        scratch_shapes=[pltpu.VMEM((3, WINDOW, 3 * WINDOW), F32)],
        compiler_params=pltpu.CompilerParams(dimension_semantics=("arbitrary",),
                                             vmem_limit_bytes=VMEM_LIMIT),
        name="attention",
    )(q, k2, v2, sink)


def _out_proj_kernel(x_ref, yh_ref, gh_ref, ya_ref, ga_ref, hg_ref, ag_ref, woh_ref, woa_ref, o_ref):
    def chan_major(ref):
        parts = []
        for kk in range(TOK_TILE // LANES):
            blk, half = kk // 2, kk % 2
            parts.append(ref[:, half, blk, 0, :, :].reshape(D_HYENA, LANES))
        return jnp.concatenate(parts, axis=1)

    yh = chan_major(yh_ref)
    gh = chan_major(gh_ref)
    yh_n = yh * lax.rsqrt(jnp.mean(yh * yh, axis=0, keepdims=True) + EPS) * hg_ref[...]
    yh_g = (yh_n * (gh * jax.nn.sigmoid(gh))).astype(BF16)
    acc = lax.dot_general(yh_g, woh_ref[...], (((0,), (0,)), ((), ())), preferred_element_type=F32)

    ya = ya_ref[0]
    ga = ga_ref[0].astype(F32)
    ya_n = ya * lax.rsqrt(jnp.mean(ya * ya, axis=-1, keepdims=True) + EPS) * ag_ref[...]
    ya_g = (ya_n * (ga * jax.nn.sigmoid(ga))).astype(BF16)
    acc = acc + jnp.dot(ya_g, woa_ref[...], preferred_element_type=F32)
    o_ref[0] = x_ref[0] + acc


def _out_proj(x, yh, hy, ya, ga, hy_out_norm_g, attn_out_norm_g, w_out):
    bsz = x.shape[0]
    n_tt = SEQ // TOK_TILE
    yh = yh.reshape(HY_TILES, 2, N_TBLK, bsz, SUBLANES, LANES)
    woh = w_out[:D_HYENA].astype(BF16)
    woa = w_out[D_HYENA:].astype(BF16)
    const = lambda shape: pl.BlockSpec(shape, lambda b, t: (0,) * len(shape))
    packed = lambda tile_blk: pl.BlockSpec(
        (HY_TILES, 2, TOK_TILE // TBLK, 1, SUBLANES, LANES), lambda b, t: (tile_blk, 0, t, b, 0, 0))
    return pl.pallas_call(
        _out_proj_kernel,
        grid=(bsz, n_tt),
        in_specs=[
            pl.BlockSpec((1, TOK_TILE, D_MODEL), lambda b, t: (b, t, 0)),
            packed(0),
            packed(3),
            pl.BlockSpec((1, TOK_TILE, D_ATTN), lambda b, t: (b, t, 0)),
            pl.BlockSpec((1, TOK_TILE, D_ATTN), lambda b, t: (b, t, 0)),
            const((D_HYENA, 1)),
            const((1, D_ATTN)),
            const(woh.shape),
            const(woa.shape),
        ],
        out_specs=pl.BlockSpec((1, TOK_TILE, D_MODEL), lambda b, t: (b, t, 0)),
        out_shape=jax.ShapeDtypeStruct(x.shape, x.dtype),
        compiler_params=pltpu.CompilerParams(dimension_semantics=("arbitrary", "arbitrary"),
                                             vmem_limit_bytes=VMEM_LIMIT),
        name="out_proj",
    )(x, yh, hy, ya, ga, hy_out_norm_g[:, None], attn_out_norm_g[None], woh, woa)


def kernel(x, norm_g, w_in, conv_w, conv_b, filt_w1, filt_b1, filt_w2, filt_b2, filt_w3, filt_b3,
           filt_w4, filt_sin_freq, hyena_bias, q_norm_g, k_norm_g, attn_sink, hy_out_norm_g,
           attn_out_norm_g, w_out):
    bsz, seq, d_model = x.shape
    assert seq == SEQ and d_model == D_MODEL and bsz % SUBLANES == 0
    assert norm_g.shape[0] == 1, "one layer"
    gm = _filter_windows(filt_w1[0], filt_b1[0], filt_w2[0], filt_b2[0], filt_w3[0], filt_b3[0],
                         filt_w4[0], filt_sin_freq[0], hyena_bias[0])
    hy, q, k2, v2, ga = _in_proj(x, norm_g[0], w_in[0], conv_w[0], conv_b[0], q_norm_g[0], k_norm_g[0])
    yh = _hyena(hy, gm, bsz)
    ya = _attention(q, k2, v2, attn_sink[0])
    return _out_proj(x, yh, hy, ya, ga, hy_out_norm_g[0], attn_out_norm_g[0], w_out[0])
```

```python
import functools
import math

import jax
import jax.numpy as jnp
from jax import lax
from jax.experimental import pallas as pl
from jax.experimental.pallas import tpu as pltpu

F32 = jnp.float32
BF16 = jnp.bfloat16
HI = lax.Precision.HIGHEST

D_MODEL = 1024
SEQ = 2048
D_HYENA = 512
D_ATTN = 512
HEAD_DIM = 64
N_Q_HEADS = 8
N_KV_HEADS = 2
WINDOW = 128
ROPE_THETA = 10000.0
FILTER_HIDDEN = 64
N_BANDS = 16
DECAY_TARGET = 1e-2
FAST_DECAY_PCT = 0.3
SLOW_DECAY_PCT = 1.5
EPS = 1e-6

LANES = 128
SUBLANES = 8
TBLK = 256
N_TBLK = SEQ // TBLK
N_SHIFT = 2 * N_TBLK - 1
N_HY_T = 4 * D_HYENA
N_HY_TILES = N_HY_T // SUBLANES
HY_TILES = D_HYENA // SUBLANES
N_ATT_COLS = D_ATTN + 2 * HEAD_DIM * N_KV_HEADS + D_ATTN
D_KV = HEAD_DIM * N_KV_HEADS

FILT_LPAD = 128
N_POS = 34 * LANES
N_WINQ = 33
WIN_PHASES = (0, SUBLANES)
N_WIN = len(WIN_PHASES) * N_WINQ
FILT_CH = 128
TOK_TILE = 512
VMEM_LIMIT = 56 * 1024 * 1024


def _nt_dot(a, b, precision=None):
    return lax.dot_general(a, b, (((1,), (1,)), ((), ())), preferred_element_type=F32,
                           precision=precision)


def _filter_kernel(w1t_ref, w1c_ref, w1s_ref, b1_ref, w2_ref, b2_ref, w3_ref, b3_ref,
                   w4_ref, fr_ref, hb_ref, o_ref, hid_ref):
    centre = FILT_LPAD + SEQ
    first = jnp.logical_and(pl.program_id(0) == 0, pl.program_id(1) == 0)

    @pl.when(first)
    def _():
        off_c = lax.broadcasted_iota(jnp.int32, (N_POS, 1), 0) - centre
        idx_c = jnp.minimum(jnp.abs(off_c), SEQ - 1).astype(F32)
        t_c = idx_c / (SEQ - 1)
        w_c = (2.0 * math.pi) * idx_c / SEQ
        band = lax.broadcasted_iota(jnp.int32, (1, N_BANDS), 1).astype(F32)
        freqs = 1e-4 + band * ((N_BANDS - 1 - 1e-4) / (N_BANDS - 1))
        ang = w_c * freqs
        fr = fr_ref[...]
        pre = (t_c * w1t_ref[...]
               + jnp.dot(jnp.cos(ang), w1c_ref[...], preferred_element_type=F32, precision=HI)
               - jnp.dot(jnp.sin(ang), w1s_ref[...], preferred_element_type=F32, precision=HI)
               + b1_ref[...])
        h = jnp.sin(fr * pre)
        h = jnp.sin(fr * (jnp.dot(h, w2_ref[...], preferred_element_type=F32, precision=HI) + b2_ref[...]))
        h = jnp.sin(fr * (jnp.dot(h, w3_ref[...], preferred_element_type=F32, precision=HI) + b3_ref[...]))
        hid_ref[...] = h

    hid = hid_ref[...]
    fwd = _nt_dot(w4_ref[0, 0], hid, precision=HI)
    bwd = _nt_dot(w4_ref[0, 1], hid, precision=HI)
    off_r = lax.broadcasted_iota(jnp.int32, (1, N_POS), 1) - centre
    idx_r = jnp.minimum(jnp.abs(off_r), SEQ - 1).astype(F32)
    max_decay = math.log(DECAY_TARGET) / FAST_DECAY_PCT
    min_decay = math.log(DECAY_TARGET) / SLOW_DECAY_PCT
    chan = (pl.program_id(1) * FILT_CH
            + lax.broadcasted_iota(jnp.int32, (FILT_CH, 1), 0)).astype(F32)
    deltas = min_decay + chan * ((max_decay - min_decay) / (D_HYENA - 1))
    decay = jnp.exp(-(idx_r / (SEQ - 1)) * jnp.abs(deltas))
    decay = jnp.where(jnp.abs(off_r) <= SEQ - 1, decay, 0.0)
    val = jnp.where(off_r < 0, bwd, fwd) * decay
    val = val + jnp.where(off_r == 0, hb_ref[0], 0.0)
    for p, phase in enumerate(WIN_PHASES):
        shifted = val if phase == 0 else pltpu.roll(val, phase, 1)
        for w in range(N_WINQ):
            o_ref[:, p * N_WINQ + w, :, :] = (
                shifted[:, w * LANES:(w + 1) * LANES].reshape(FILT_CH // SUBLANES, SUBLANES, LANES))


def _filter_windows(w1, b1, w2, b2, w3, b3, w4, sin_freq, hyena_bias):
    full = lambda shape: pl.BlockSpec(shape, lambda o, j: (0,) * len(shape))
    w4r = w4.T.reshape(2, 2, D_HYENA, FILTER_HIDDEN)
    args = (w1[0:1], w1[1:1 + N_BANDS], w1[1 + N_BANDS:], b1[None], w2, b2[None], w3, b3[None])
    n_j = D_HYENA // FILT_CH
    tiles = FILT_CH // SUBLANES
    out = pl.pallas_call(
        _filter_kernel,
        grid=(2, n_j),
        in_specs=[full(a.shape) for a in args] + [
            pl.BlockSpec((1, 2, FILT_CH, FILTER_HIDDEN), lambda o, j: (o, 0, j, 0)),
            full((1, FILTER_HIDDEN)),
            pl.BlockSpec((1, FILT_CH, 1), lambda o, j: (o, j, 0)),
        ],
        out_specs=pl.BlockSpec((tiles, N_WIN, SUBLANES, LANES), lambda o, j: (o * n_j + j, 0, 0, 0)),
        out_shape=jax.ShapeDtypeStruct((2 * HY_TILES, N_WIN, SUBLANES, LANES), F32),
        scratch_shapes=[pltpu.VMEM((N_POS, FILTER_HIDDEN), F32)],
        compiler_params=pltpu.CompilerParams(dimension_semantics=("arbitrary", "arbitrary"),
                                             vmem_limit_bytes=VMEM_LIMIT),
        name="filter_windows",
    )(*args, w4r, sin_freq[None], hyena_bias[:, :, None])
    return out.reshape(2 * HY_TILES, N_WIN * SUBLANES, LANES)


N_STREAM = 4
CONV_SUB = 64


def _in_proj_kernel(x_ref, g_ref, wht_ref, cw_ref, wa_ref, seg_ref, cos_ref, sin_ref, qg_ref, kg_ref,
                    hy_ref, q_ref, k_ref, v_ref, ga_ref, h0_ref, h1_ref, r0_ref, r1_ref):
    bb = pl.program_id(0)
    j = pl.program_id(1)
    n_tt = SEQ // TOK_TILE
    n_chunk = SEQ // LANES
    lane = lax.broadcasted_iota(jnp.int32, (TOK_TILE, LANES), 1)
    lane_c = lax.broadcasted_iota(jnp.int32, (CONV_SUB, LANES), 1)

    def normalise(h_cur):
        x = x_ref[0]
        ms = jnp.mean(x * x, axis=-1, keepdims=True)
        t0 = pl.multiple_of(j * TOK_TILE, TOK_TILE)
        h_cur[pl.ds(t0, TOK_TILE), :] = (x * lax.rsqrt(ms + EPS) * g_ref[...]).astype(BF16)

    def project(h_prev, dst_ref):
        for t in range(n_tt):
            dst_ref[:, t * TOK_TILE:(t + 1) * TOK_TILE] = _nt_dot(
                wht_ref[...], h_prev[t * TOK_TILE:(t + 1) * TOK_TILE, :])

    def emit(src_ref, conv):
        for sub in range(D_HYENA // CONV_SUB):
            rows = slice(sub * CONV_SUB, (sub + 1) * CONV_SUB)
            tiles = slice(sub * CONV_SUB // SUBLANES, (sub + 1) * CONV_SUB // SUBLANES)
            if conv:
                w0, w1, w2, wb = (cw_ref[0, i, rows, :] for i in range(4))
                chunk = lambda k: src_ref[rows, k * LANES:(k + 1) * LANES]
                prev_r = None
                cur = chunk(0)
                cur_r, cur_l = pltpu.roll(cur, 1, 1), pltpu.roll(cur, LANES - 1, 1)
            for k in range(n_chunk):
                if conv:
                    if k + 1 < n_chunk:
                        nxt = chunk(k + 1)
                        nxt_r, nxt_l = pltpu.roll(nxt, 1, 1), pltpu.roll(nxt, LANES - 1, 1)
                    um = jnp.where(lane_c == 0, 0.0 if prev_r is None else prev_r, cur_r)
                    up = jnp.where(lane_c == LANES - 1, nxt_l if k + 1 < n_chunk else 0.0, cur_l)
                    out = w0 * um + w1 * cur + w2 * up + wb
                    prev_r = cur_r
                    if k + 1 < n_chunk:
                        cur, cur_r, cur_l = nxt, nxt_r, nxt_l
                else:
                    out = src_ref[rows, k * LANES:(k + 1) * LANES]
                blk, half = k // 2, k % 2
                hy_ref[tiles, half, blk, 0, :, :] = out.reshape(CONV_SUB // SUBLANES, SUBLANES, LANES)

    def attention_cols(h_prev, t):
        att = jnp.dot(h_prev[t * TOK_TILE:(t + 1) * TOK_TILE, :], wa_ref[...],
                      preferred_element_type=F32)
        cos = cos_ref[...]
        sin = sin_ref[...]
        first_half = (lane % HEAD_DIM) < (HEAD_DIM // 2)

        def norm_rope(t, gain, seg):
            ms_h = jnp.dot((t * t).astype(BF16), seg, preferred_element_type=F32)
            tn = t * lax.rsqrt(ms_h + EPS) * gain
            outs = []
            for i in range(t.shape[1] // LANES):
                c = tn[:, i * LANES:(i + 1) * LANES]
                swapped = jnp.where(first_half, pltpu.roll(c, LANES - HEAD_DIM // 2, 1),
                                    pltpu.roll(c, HEAD_DIM // 2, 1))
                outs.append(c * cos + swapped * sin)
            return outs[0] if len(outs) == 1 else jnp.concatenate(outs, axis=1)

        q = norm_rope(att[:, :D_ATTN], qg_ref[...], seg_ref[...])
        k = norm_rope(att[:, D_ATTN:D_ATTN + D_KV], kg_ref[...], seg_ref[:D_KV, :D_KV])
        v = att[:, D_ATTN + D_KV:D_ATTN + 2 * D_KV]
        q_ref[0] = (q * (HEAD_DIM ** -0.5)).astype(BF16)
        k_ref[0] = jnp.concatenate([k, pltpu.roll(k, HEAD_DIM, 1)], axis=1).astype(BF16)
        v_sw = pltpu.roll(v, HEAD_DIM, 1)
        low = lane < HEAD_DIM
        v_ref[0] = jnp.concatenate([jnp.where(low, v, 1.0), jnp.where(low, 1.0, v_sw),
                                    jnp.where(low, v_sw, 1.0), jnp.where(low, 1.0, v)], axis=1).astype(BF16)
        ga_ref[0] = att[:, D_ATTN + 2 * D_KV:].astype(BF16)

    def steps(h_cur, h_prev):
        @pl.when(jnp.logical_and(bb == 0, j < N_STREAM))
        def _():
            normalise(h_cur)

        @pl.when(jnp.logical_and(bb > 0, j == 0))
        def _():
            normalise(h_cur)
            project(h_prev, r0_ref)

        for step in range(1, N_STREAM):
            @pl.when(jnp.logical_and(bb > 0, j == step))
            def _(step=step):
                attention_cols(h_prev, step - 1)
                normalise(h_cur)
                project(h_prev, r1_ref if step % 2 else r0_ref)
                emit(r0_ref if step % 2 else r1_ref, True)

        @pl.when(jnp.logical_and(bb > 0, j == N_STREAM))
        def _():
            attention_cols(h_prev, N_STREAM - 1)
            emit(r1_ref, False)

    @pl.when(bb % 2 == 0)
    def _():
        steps(h0_ref, h1_ref)

    @pl.when(bb % 2 == 1)
    def _():
        steps(h1_ref, h0_ref)


def _in_proj(x, norm_g, w_in, conv_w, conv_b, q_norm_g, k_norm_g):
    bsz = x.shape[0]
    n_tt = SEQ // TOK_TILE
    assert n_tt == N_STREAM
    o1 = N_STREAM * D_HYENA
    wht = w_in[:, :o1].T.astype(BF16)
    wa = w_in[:, o1:].astype(BF16)
    cw = jnp.concatenate([conv_w.reshape(3, 3, D_HYENA).transpose(1, 0, 2),
                          conv_b.reshape(3, 1, D_HYENA)], axis=1)
    cw = jnp.broadcast_to(cw[..., None], (3, 4, D_HYENA, LANES))
    head = jnp.arange(D_ATTN) // HEAD_DIM
    seg = jnp.where(head[:, None] == head[None, :], 1.0 / HEAD_DIM, 0.0).astype(BF16)
    half = HEAD_DIM // 2
    inv = ROPE_THETA ** (-jnp.arange(half, dtype=F32) / half)
    ang = jnp.arange(SEQ, dtype=F32)[:, None] * inv[None, :]
    cos = jnp.tile(jnp.cos(ang), (1, LANES // half))
    sin = jnp.tile(jnp.concatenate([-jnp.sin(ang), jnp.sin(ang)], axis=1), (1, LANES // HEAD_DIM))
    qg = jnp.tile(q_norm_g, D_ATTN // HEAD_DIM)[None]
    kg = jnp.tile(k_norm_g, D_KV // HEAD_DIM)[None]
    const = lambda shape: pl.BlockSpec(shape, lambda b, j: (0,) * len(shape))
    last = N_STREAM - 1
    out_b = lambda bb: jnp.maximum(bb - 1, 0)
    att_t = lambda bb, j: jnp.where(bb == 0, 0, jnp.clip(j - 1, 0, last))
    tok = lambda width: pl.BlockSpec((1, TOK_TILE, width), lambda bb, j: (out_b(bb), att_t(bb, j), 0))
    return pl.pallas_call(
        _in_proj_kernel,
        grid=(bsz + 1, N_STREAM + 1),
        in_specs=[
            pl.BlockSpec((1, TOK_TILE, D_MODEL),
                         lambda bb, j: (jnp.minimum(bb, bsz - 1), jnp.minimum(j, last), 0)),
            const((1, D_MODEL)),
            pl.BlockSpec((D_HYENA, D_MODEL), lambda b, j: (jnp.minimum(j, last), 0)),
            pl.BlockSpec((1, 4, D_HYENA, LANES), lambda b, j: (jnp.clip(j - 1, 0, 2), 0, 0, 0)),
            const(wa.shape),
            const(seg.shape),
            pl.BlockSpec((TOK_TILE, LANES), lambda bb, j: (att_t(bb, j), 0)),
            pl.BlockSpec((TOK_TILE, LANES), lambda bb, j: (att_t(bb, j), 0)),
            const((1, D_ATTN)),
            const((1, D_KV)),
        ],
        out_specs=[
            pl.BlockSpec((HY_TILES, 2, N_TBLK, 1, SUBLANES, LANES),
                         lambda bb, j: (jnp.where(bb == 0, 0, jnp.maximum(j - 1, 0)), 0, 0, out_b(bb), 0, 0)),
            tok(D_ATTN), tok(2 * D_KV), tok(4 * D_KV), tok(D_ATTN),
        ],
        out_shape=[
            jax.ShapeDtypeStruct((N_HY_TILES, 2, N_TBLK, bsz, SUBLANES, LANES), F32),
            jax.ShapeDtypeStruct((bsz, SEQ, D_ATTN), BF16),
            jax.ShapeDtypeStruct((bsz, SEQ, 2 * D_KV), BF16),
            jax.ShapeDtypeStruct((bsz, SEQ, 4 * D_KV), BF16),
            jax.ShapeDtypeStruct((bsz, SEQ, D_ATTN), BF16),
        ],
        scratch_shapes=[pltpu.VMEM((SEQ, D_MODEL), BF16), pltpu.VMEM((SEQ, D_MODEL), BF16),
                        pltpu.VMEM((D_HYENA, SEQ), F32), pltpu.VMEM((D_HYENA, SEQ), F32)],
        compiler_params=pltpu.CompilerParams(dimension_semantics=("arbitrary", "arbitrary"),
                                             vmem_limit_bytes=VMEM_LIMIT),
        name="in_proj",
    )(x, norm_g[None], wht, cw, wa, seg, cos, sin, qg, kg)


STRIP_ROWS = N_SHIFT * TBLK + LANES


def _hyena_kernel(bsz, v_ref, x1_ref, x2_ref, g0_ref, g1_ref, g0n_ref, g1n_ref,
                  o_ref, *strip_refs):
    ct = pl.program_id(0)
    rows = N_TBLK * bsz
    half_rows = rows * SUBLANES
    n_pairs = STRIP_ROWS // 16
    sets = [[strip_refs[0:2], strip_refs[2:4]], [strip_refs[4:6], strip_refs[6:8]]]

    row16 = lax.broadcasted_iota(jnp.int32, (16, LANES), 0) % SUBLANES
    lane16 = lax.broadcasted_iota(jnp.int32, (16, LANES), 1)
    n_rot = LANES // 16
    inside = [lane16 >= row16] + [lane16 - row16 + 16 * r < LANES for r in range(1, n_rot)]

    def build_strip(g_ref, c, strip_ref):
        cache = {}

        def window_pair(q):
            if ("w", q) not in cache:
                rows = [jnp.broadcast_to(g_ref[pl.ds((p * N_WINQ + q) * SUBLANES + c, 1), :],
                                         (SUBLANES, LANES)) for p in range(len(WIN_PHASES))]
                cache["w", q] = rows
            return cache["w", q]

        def rotated(q, r):
            if (q, r) not in cache:
                tiles = [pltpu.roll(w, (LANES - 16 * r) % LANES, 1, stride=1, stride_axis=0)
                         for w in window_pair(q)]
                cache[q, r] = jnp.concatenate(tiles, axis=0).astype(BF16)
            return cache[q, r]

        for k in range(n_pairs, 0, -1):
            q, r = k // n_rot + 1, k % n_rot
            pair = jnp.where(inside[r], rotated(q, r), rotated(q - 1 if r == 0 else q + 1, r))
            strip_ref[16 * (n_pairs - k):16 * (n_pairs - k) + 16, :] = pair

    def weights(strip_ref, e):
        return jnp.concatenate([strip_ref[TBLK * e + LANES:TBLK * e + LANES + TBLK, :],
                                strip_ref[TBLK * e:TBLK * e + TBLK, :]], axis=1)

    def load_rows(ref, c):
        lo = ref[pl.ds(c, rows, stride=SUBLANES), :]
        hi = ref[pl.ds(half_rows + c, rows, stride=SUBLANES), :]
        return jnp.concatenate([lo, hi], axis=1)

    def build_set(s, c0, g_refs):
        for i in range(2):
            for order in range(2):
                build_strip(g_refs[order], c0 + i, sets[s][i][order])

    def long_convs(us, strips):
        ubs = [u.astype(BF16) for u in us]
        n = len(us)
        dot = lambda i, lhs, e: jnp.dot(lhs, weights(strips[i], e), preferred_element_type=F32)
        acc_p = [None] * n
        acc_n = [None] * n
        for d in range(N_TBLK - 1, 0, -1):
            for i in range(n):
                res = dot(i, ubs[i][0:rows - bsz * d], N_TBLK - 1 - d)
                acc_p[i] = res if acc_p[i] is None else jnp.concatenate(
                    [res[:bsz], res[bsz:] + acc_p[i]], axis=0)
            for i in range(n):
                res = dot(i, ubs[i][bsz * d:rows], N_TBLK - 1 + d)
                acc_n[i] = res if acc_n[i] is None else jnp.concatenate(
                    [res[:-bsz] + acc_n[i], res[-bsz:]], axis=0)
        outs = []
        for i in range(n):
            res = dot(i, ubs[i], N_TBLK - 1)
            outs.append(jnp.concatenate(
                [res[:bsz] + acc_n[i][:bsz],
                 res[bsz:rows - bsz] + acc_n[i][bsz:] + acc_p[i][:rows - 2 * bsz],
                 res[rows - bsz:] + acc_p[i][rows - 2 * bsz:]], axis=0))
        return outs

    def process_pair(c0, strips):
        cs = (c0, c0 + 1)
        hv = [load_rows(v_ref, c) for c in cs]
        conv = long_convs(hv, [strips[i][0] for i in range(2)])
        z = [load_rows(x1_ref, c) * conv[i] for i, c in enumerate(cs)]
        conv = long_convs(z, [strips[i][1] for i in range(2)])
        for i, c in enumerate(cs):
            y = load_rows(x2_ref, c) * conv[i]
            o_ref[pl.ds(c, rows, stride=SUBLANES), :] = y[:, :LANES]
            o_ref[pl.ds(half_rows + c, rows, stride=SUBLANES), :] = y[:, LANES:]

    @pl.when(ct == 0)
    def _():
        build_set(0, 0, (g0_ref, g1_ref))

    n_cp = SUBLANES // 2
    for p in range(n_cp):
        if p + 1 < n_cp:
            build_set((p + 1) % 2, 2 * (p + 1), (g0_ref, g1_ref))
        else:
            build_set((p + 1) % 2, 0, (g0n_ref, g1n_ref))
        process_pair(2 * p, sets[p % 2])


def _hyena(hy, gm, bsz):
    n_tiles = gm.shape[0] // 2
    rows = N_TBLK * bsz
    prow = 2 * rows * SUBLANES
    hy = hy.reshape(hy.shape[0], prow, LANES)
    gspec = lambda off: pl.BlockSpec((None, N_WIN * SUBLANES, LANES), lambda i: (i + off, 0, 0))
    gnext = lambda off: pl.BlockSpec((None, N_WIN * SUBLANES, LANES),
                                     lambda i: (jnp.minimum(i + 1, n_tiles - 1) + off, 0, 0))
    uspec = lambda off: pl.BlockSpec((None, prow, LANES), lambda i: (i + off, 0, 0))
    strip = pltpu.VMEM((STRIP_ROWS, LANES), BF16)
    return pl.pallas_call(
        functools.partial(_hyena_kernel, bsz),
        grid=(n_tiles,),
        in_specs=[uspec(0), uspec(n_tiles), uspec(2 * n_tiles), gspec(0), gspec(n_tiles),
                  gnext(0), gnext(n_tiles)],
        out_specs=pl.BlockSpec((None, prow, LANES), lambda i: (i, 0, 0)),
        out_shape=jax.ShapeDtypeStruct((n_tiles, prow, LANES), F32),
        scratch_shapes=[strip] * 8,
        compiler_params=pltpu.CompilerParams(dimension_semantics=("arbitrary",),
                                             vmem_limit_bytes=VMEM_LIMIT),
        name="hyena",
    )(hy, hy, hy, gm, gm, gm, gm)


ATT_QB = WINDOW
ATT_SPAN = 3 * WINDOW


def _attn_out_kernel(x_ref, yh_ref, gh_ref, q_ref, k_ref, v_ref, ga_ref, sink_ref, hg_ref, ag_ref,
                     woh_ref, woa_ref, o_ref, bias_ref, ya_ref):
    qb, span = ATT_QB, ATT_SPAN
    t = pl.program_id(1)

    @pl.when(jnp.logical_and(pl.program_id(0) == 0, t == 0))
    def _():
        r = lax.broadcasted_iota(jnp.int32, (qb, span), 0)
        c = lax.broadcasted_iota(jnp.int32, (qb, span), 1)
        for j in range(3):
            bias_ref[j] = jnp.where(jnp.abs(c - r - j * WINDOW) <= WINDOW, 0.0, -jnp.inf)

    lane = lax.broadcasted_iota(jnp.int32, (2 * qb, LANES), 1)
    low = lane < HEAD_DIM
    row2 = lax.broadcasted_iota(jnp.int32, (2 * qb, 1), 0)
    sinks = [[jnp.where(row2 < qb, sink_ref[4 * kvh + par], sink_ref[4 * kvh + 2 + par])
              for par in range(2)] for kvh in range(N_KV_HEADS)]

    for i in range(TOK_TILE // qb):
        r0 = i * qb
        q0 = pl.multiple_of(t * TOK_TILE + r0, qb)
        ws = pl.multiple_of(jnp.clip(q0 - WINDOW, 0, SEQ - span), WINDOW)
        bias1 = bias_ref[(q0 - ws) // WINDOW]
        bias = jnp.concatenate([bias1, bias1], axis=0)
        scores = []
        for kvh in range(N_KV_HEADS):
            qs = jnp.concatenate(
                [q_ref[0, r0:r0 + qb, (2 * kvh) * LANES:(2 * kvh + 1) * LANES],
                 q_ref[0, r0:r0 + qb, (2 * kvh + 1) * LANES:(2 * kvh + 2) * LANES]], axis=0)
            zero = jnp.zeros_like(qs)
            for par in range(2):
                qm = jnp.where(low, qs, zero) if par == 0 else jnp.where(low, zero, qs)
                ks = (kvh + par) % 2
                km = k_ref[0, pl.ds(ws, span), ks * LANES:(ks + 1) * LANES]
                scores.append(_nt_dot(qm, km) + bias)
        for kvh in range(N_KV_HEADS):
            parts = []
            for par in range(2):
                s = scores[2 * kvh + par]
                vs = 2 * kvh + par
                vm = v_ref[0, pl.ds(ws, span), vs * LANES:(vs + 1) * LANES]
                sink = sinks[kvh][par]
                m = jnp.maximum(jnp.max(s, axis=-1, keepdims=True), sink)
                p = jnp.exp((s - m).astype(BF16))
                parts.append((jnp.dot(p, vm, preferred_element_type=F32), jnp.exp(sink - m)))
            (o_e, sink_e), (o_o, sink_o) = parts
            num = jnp.where(low, o_e, o_o)
            den = jnp.where(low, pltpu.roll(o_e, HEAD_DIM, 1) + sink_e,
                            pltpu.roll(o_o, HEAD_DIM, 1) + sink_o)
            o = num / den
            ya_ref[r0:r0 + qb, (2 * kvh) * LANES:(2 * kvh + 1) * LANES] = o[:qb]
            ya_ref[r0:r0 + qb, (2 * kvh + 1) * LANES:(2 * kvh + 2) * LANES] = o[qb:]

    def chan_major(ref):
        parts = []
        for kk in range(TOK_TILE // LANES):
            blk, half = kk // 2, kk % 2
            parts.append(ref[:, half, blk, 0, :, :].reshape(D_HYENA, LANES))
        return jnp.concatenate(parts, axis=1)

    yh = chan_major(yh_ref)
    gh = chan_major(gh_ref)
    yh_n = yh * lax.rsqrt(jnp.mean(yh * yh, axis=0, keepdims=True) + EPS) * hg_ref[...]
    yh_g = (yh_n * (gh * jax.nn.sigmoid(gh))).astype(BF16)
    acc = lax.dot_general(yh_g, woh_ref[...], (((0,), (0,)), ((), ())), preferred_element_type=F32)

    ya = ya_ref[...]
    ga = ga_ref[0].astype(F32)
    ya_n = ya * lax.rsqrt(jnp.mean(ya * ya, axis=-1, keepdims=True) + EPS) * ag_ref[...]
    ya_g = (ya_n * (ga * jax.nn.sigmoid(ga))).astype(BF16)
    acc = acc + jnp.dot(ya_g, woa_ref[...], preferred_element_type=F32)
    o_ref[0] = x_ref[0] + acc


def _attn_out(x, yh, hy, q, k2, v4, ga, sink, hy_out_norm_g, attn_out_norm_g, w_out):
    bsz = x.shape[0]
    n_tt = SEQ // TOK_TILE
    yh = yh.reshape(HY_TILES, 2, N_TBLK, bsz, SUBLANES, LANES)
    woh = w_out[:D_HYENA].astype(BF16)
    woa = w_out[D_HYENA:].astype(BF16)
    const = lambda shape: pl.BlockSpec(shape, lambda b, t: (0,) * len(shape))
    tok = lambda width: pl.BlockSpec((1, TOK_TILE, width), lambda b, t: (b, t, 0))
    seq = lambda width: pl.BlockSpec((1, SEQ, width), lambda b, t: (b, 0, 0))
    packed = lambda tile_blk: pl.BlockSpec(
        (HY_TILES, 2, TOK_TILE // TBLK, 1, SUBLANES, LANES), lambda b, t: (tile_blk, 0, t, b, 0, 0))
    return pl.pallas_call(
        _attn_out_kernel,
        grid=(bsz, n_tt),
        in_specs=[
            tok(D_MODEL), packed(0), packed(3), tok(D_ATTN), seq(2 * D_KV), seq(4 * D_KV), tok(D_ATTN),
            pl.BlockSpec(memory_space=pltpu.SMEM),
            const((D_HYENA, 1)), const((1, D_ATTN)), const(woh.shape), const(woa.shape),
        ],
        out_specs=tok(D_MODEL),
        out_shape=jax.ShapeDtypeStruct(x.shape, x.dtype),
        scratch_shapes=[pltpu.VMEM((3, ATT_QB, ATT_SPAN), F32), pltpu.VMEM((TOK_TILE, D_ATTN), F32)],
        compiler_params=pltpu.CompilerParams(dimension_semantics=("arbitrary", "arbitrary"),
                                             vmem_limit_bytes=VMEM_LIMIT),
        name="attn_out",
    )(x, yh, hy, q, k2, v4, ga, sink, hy_out_norm_g[:, None], attn_out_norm_g[None], woh, woa)


def kernel(x, norm_g, w_in, conv_w, conv_b, filt_w1, filt_b1, filt_w2, filt_b2, filt_w3, filt_b3,
           filt_w4, filt_sin_freq, hyena_bias, q_norm_g, k_norm_g, attn_sink, hy_out_norm_g,
           attn_out_norm_g, w_out):
    bsz, seq, d_model = x.shape
    assert seq == SEQ and d_model == D_MODEL and bsz % SUBLANES == 0
    assert norm_g.shape[0] == 1, "one layer"
    gm = _filter_windows(filt_w1[0], filt_b1[0], filt_w2[0], filt_b2[0], filt_w3[0], filt_b3[0],
                         filt_w4[0], filt_sin_freq[0], hyena_bias[0])
    hy, q, k2, v2, ga = _in_proj(x, norm_g[0], w_in[0], conv_w[0], conv_b[0], q_norm_g[0], k_norm_g[0])
    yh = _hyena(hy, gm, bsz)
    return _attn_out(x, yh, hy, q, k2, v2, ga, attn_sink[0], hy_out_norm_g[0], attn_out_norm_g[0],
                     w_out[0])
```

```python
import functools
import math

import jax
import jax.numpy as jnp
from jax import lax
from jax.experimental import pallas as pl
from jax.experimental.pallas import tpu as pltpu

F32 = jnp.float32
BF16 = jnp.bfloat16
HI = lax.Precision.HIGHEST

D_MODEL = 1024
SEQ = 2048
D_HYENA = 512
D_ATTN = 512
HEAD_DIM = 64
N_Q_HEADS = 8
N_KV_HEADS = 2
WINDOW = 128
ROPE_THETA = 10000.0
FILTER_HIDDEN = 64
N_BANDS = 16
DECAY_TARGET = 1e-2
FAST_DECAY_PCT = 0.3
SLOW_DECAY_PCT = 1.5
EPS = 1e-6

LANES = 128
SUBLANES = 8
TBLK = 256
N_TBLK = SEQ // TBLK
N_SHIFT = 2 * N_TBLK - 1
N_HY_T = 4 * D_HYENA
N_HY_TILES = N_HY_T // SUBLANES
HY_TILES = D_HYENA // SUBLANES
N_ATT_COLS = D_ATTN + 2 * HEAD_DIM * N_KV_HEADS + D_ATTN
D_KV = HEAD_DIM * N_KV_HEADS

FILT_LPAD = 128
N_POS = 34 * LANES
N_WINQ = 33
WIN_PHASES = (0, SUBLANES)
N_WIN = len(WIN_PHASES) * N_WINQ
FILT_CH = 128
TOK_TILE = 512
VMEM_LIMIT = 56 * 1024 * 1024


def _nt_dot(a, b, precision=None):
    return lax.dot_general(a, b, (((1,), (1,)), ((), ())), preferred_element_type=F32,
                           precision=precision)


def _filter_kernel(w1t_ref, w1c_ref, w1s_ref, b1_ref, w2_ref, b2_ref, w3_ref, b3_ref,
                   w4_ref, fr_ref, hb_ref, o_ref, hid_ref):
    centre = FILT_LPAD + SEQ
    first = jnp.logical_and(pl.program_id(0) == 0, pl.program_id(1) == 0)

    @pl.when(first)
    def _():
        off_c = lax.broadcasted_iota(jnp.int32, (N_POS, 1), 0) - centre
        idx_c = jnp.minimum(jnp.abs(off_c), SEQ - 1).astype(F32)
        t_c = idx_c / (SEQ - 1)
        w_c = (2.0 * math.pi) * idx_c / SEQ
        band = lax.broadcasted_iota(jnp.int32, (1, N_BANDS), 1).astype(F32)
        freqs = 1e-4 + band * ((N_BANDS - 1 - 1e-4) / (N_BANDS - 1))
        ang = w_c * freqs
        fr = fr_ref[...]
        pre = (t_c * w1t_ref[...]
               + jnp.dot(jnp.cos(ang), w1c_ref[...], preferred_element_type=F32, precision=HI)
               - jnp.dot(jnp.sin(ang), w1s_ref[...], preferred_element_type=F32, precision=HI)
               + b1_ref[...])
        h = jnp.sin(fr * pre)
        h = jnp.sin(fr * (jnp.dot(h, w2_ref[...], preferred_element_type=F32, precision=HI) + b2_ref[...]))
        h = jnp.sin(fr * (jnp.dot(h, w3_ref[...], preferred_element_type=F32, precision=HI) + b3_ref[...]))
        hid_ref[...] = h

    hid = hid_ref[...]
    fwd = _nt_dot(w4_ref[0, 0], hid, precision=HI)
    bwd = _nt_dot(w4_ref[0, 1], hid, precision=HI)
    off_r = lax.broadcasted_iota(jnp.int32, (1, N_POS), 1) - centre
    idx_r = jnp.minimum(jnp.abs(off_r), SEQ - 1).astype(F32)
    max_decay = math.log(DECAY_TARGET) / FAST_DECAY_PCT
    min_decay = math.log(DECAY_TARGET) / SLOW_DECAY_PCT
    chan = (pl.program_id(1) * FILT_CH
            + lax.broadcasted_iota(jnp.int32, (FILT_CH, 1), 0)).astype(F32)
    deltas = min_decay + chan * ((max_decay - min_decay) / (D_HYENA - 1))
    decay = jnp.exp(-(idx_r / (SEQ - 1)) * jnp.abs(deltas))
    decay = jnp.where(jnp.abs(off_r) <= SEQ - 1, decay, 0.0)
    val = jnp.where(off_r < 0, bwd, fwd) * decay
    val = val + jnp.where(off_r == 0, hb_ref[0], 0.0)
    for p, phase in enumerate(WIN_PHASES):
        shifted = val if phase == 0 else pltpu.roll(val, phase, 1)
        for w in range(N_WINQ):
            o_ref[:, p * N_WINQ + w, :, :] = (
                shifted[:, w * LANES:(w + 1) * LANES].reshape(FILT_CH // SUBLANES, SUBLANES, LANES))


def _filter_windows(w1, b1, w2, b2, w3, b3, w4, sin_freq, hyena_bias):
    full = lambda shape: pl.BlockSpec(shape, lambda o, j: (0,) * len(shape))
    w4r = w4.T.reshape(2, 2, D_HYENA, FILTER_HIDDEN)
    args = (w1[0:1], w1[1:1 + N_BANDS], w1[1 + N_BANDS:], b1[None], w2, b2[None], w3, b3[None])
    n_j = D_HYENA // FILT_CH
    tiles = FILT_CH // SUBLANES
    out = pl.pallas_call(
        _filter_kernel,
        grid=(2, n_j),
        in_specs=[full(a.shape) for a in args] + [
            pl.BlockSpec((1, 2, FILT_CH, FILTER_HIDDEN), lambda o, j: (o, 0, j, 0)),
            full((1, FILTER_HIDDEN)),
            pl.BlockSpec((1, FILT_CH, 1), lambda o, j: (o, j, 0)),
        ],
        out_specs=pl.BlockSpec((tiles, N_WIN, SUBLANES, LANES), lambda o, j: (o * n_j + j, 0, 0, 0)),
        out_shape=jax.ShapeDtypeStruct((2 * HY_TILES, N_WIN, SUBLANES, LANES), F32),
        scratch_shapes=[pltpu.VMEM((N_POS, FILTER_HIDDEN), F32)],
        compiler_params=pltpu.CompilerParams(dimension_semantics=("arbitrary", "arbitrary"),
                                             vmem_limit_bytes=VMEM_LIMIT),
        name="filter_windows",
    )(*args, w4r, sin_freq[None], hyena_bias[:, :, None])
    return out.reshape(2 * HY_TILES, N_WIN * SUBLANES, LANES)


N_STREAM = 4
SEG_W = 256
CONV_SUB = 64


def _in_proj_kernel(x_ref, g_ref, wht_ref, cw_ref, wa_ref, seg_ref, cos_ref, sin_ref, qg_ref, kg_ref,
                    hy_ref, q_ref, k_ref, v_ref, ga_ref, h0_ref, h1_ref, r0_ref, r1_ref):
    bb = pl.program_id(0)
    j = pl.program_id(1)
    n_tt = SEQ // TOK_TILE
    n_chunk = SEQ // LANES
    lane = lax.broadcasted_iota(jnp.int32, (TOK_TILE, LANES), 1)
    lane_c = lax.broadcasted_iota(jnp.int32, (CONV_SUB, LANES), 1)

    def normalise(h_cur):
        x = x_ref[0]
        ms = jnp.mean(x * x, axis=-1, keepdims=True)
        t0 = pl.multiple_of(j * TOK_TILE, TOK_TILE)
        h_cur[pl.ds(t0, TOK_TILE), :] = (x * lax.rsqrt(ms + EPS) * g_ref[...]).astype(BF16)

    def project(h_prev, dst_ref):
        for t in range(n_tt):
            dst_ref[:, t * TOK_TILE:(t + 1) * TOK_TILE] = _nt_dot(
                wht_ref[...], h_prev[t * TOK_TILE:(t + 1) * TOK_TILE, :])

    def emit(src_ref, conv):
        for sub in range(D_HYENA // CONV_SUB):
            rows = slice(sub * CONV_SUB, (sub + 1) * CONV_SUB)
            tiles = slice(sub * CONV_SUB // SUBLANES, (sub + 1) * CONV_SUB // SUBLANES)
            if conv:
                w0, w1, w2, wb = (cw_ref[0, i, rows, :] for i in range(4))
                chunk = lambda k: src_ref[rows, k * LANES:(k + 1) * LANES]
                prev_r = None
                cur = chunk(0)
                cur_r, cur_l = pltpu.roll(cur, 1, 1), pltpu.roll(cur, LANES - 1, 1)
            for k in range(n_chunk):
                if conv:
                    if k + 1 < n_chunk:
                        nxt = chunk(k + 1)
                        nxt_r, nxt_l = pltpu.roll(nxt, 1, 1), pltpu.roll(nxt, LANES - 1, 1)
                    um = jnp.where(lane_c == 0, 0.0 if prev_r is None else prev_r, cur_r)
                    up = jnp.where(lane_c == LANES - 1, nxt_l if k + 1 < n_chunk else 0.0, cur_l)
                    out = w0 * um + w1 * cur + w2 * up + wb
                    prev_r = cur_r
                    if k + 1 < n_chunk:
                        cur, cur_r, cur_l = nxt, nxt_r, nxt_l
                else:
                    out = src_ref[rows, k * LANES:(k + 1) * LANES]
                blk, half = k // 2, k % 2
                hy_ref[tiles, half, blk, 0, :, :] = out.reshape(CONV_SUB // SUBLANES, SUBLANES, LANES)

    def attention_cols(h_prev, t):
        att = jnp.dot(h_prev[t * TOK_TILE:(t + 1) * TOK_TILE, :], wa_ref[...],
                      preferred_element_type=F32)
        cos = cos_ref[...]
        sin = sin_ref[...]
        first_half = (lane % HEAD_DIM) < (HEAD_DIM // 2)

        def norm_rope(t, gain):
            sq = (t * t).astype(BF16)
            w = min(t.shape[1], SEG_W)
            seg = seg_ref[:w, :w]
            slabs = [jnp.dot(sq[:, i:i + w], seg, preferred_element_type=F32)
                     for i in range(0, t.shape[1], w)]
            ms_h = slabs[0] if len(slabs) == 1 else jnp.concatenate(slabs, axis=1)
            tn = t * lax.rsqrt(ms_h + EPS) * gain
            outs = []
            for i in range(t.shape[1] // LANES):
                c = tn[:, i * LANES:(i + 1) * LANES]
                swapped = jnp.where(first_half, pltpu.roll(c, LANES - HEAD_DIM // 2, 1),
                                    pltpu.roll(c, HEAD_DIM // 2, 1))
                outs.append(c * cos + swapped * sin)
            return outs[0] if len(outs) == 1 else jnp.concatenate(outs, axis=1)

        q = norm_rope(att[:, :D_ATTN], qg_ref[...])
        k = norm_rope(att[:, D_ATTN:D_ATTN + D_KV], kg_ref[...])
        v = att[:, D_ATTN + D_KV:D_ATTN + 2 * D_KV]
        q_ref[0] = (q * (HEAD_DIM ** -0.5)).astype(BF16)
        k_ref[0] = jnp.concatenate([k, pltpu.roll(k, HEAD_DIM, 1)], axis=1).astype(BF16)
        v_sw = pltpu.roll(v, HEAD_DIM, 1)
        low = lane < HEAD_DIM
        v_ref[0] = jnp.concatenate([jnp.where(low, v, 1.0), jnp.where(low, 1.0, v_sw),
                                    jnp.where(low, v_sw, 1.0), jnp.where(low, 1.0, v)], axis=1).astype(BF16)
        ga_ref[0] = att[:, D_ATTN + 2 * D_KV:].astype(BF16)

    def steps(h_cur, h_prev):
        @pl.when(jnp.logical_and(bb == 0, j < N_STREAM))
        def _():
            normalise(h_cur)

        @pl.when(jnp.logical_and(bb > 0, j == 0))
        def _():
            normalise(h_cur)
            project(h_prev, r0_ref)

        for step in range(1, N_STREAM):
            @pl.when(jnp.logical_and(bb > 0, j == step))
            def _(step=step):
                attention_cols(h_prev, step - 1)
                normalise(h_cur)
                project(h_prev, r1_ref if step % 2 else r0_ref)
                emit(r0_ref if step % 2 else r1_ref, True)

        @pl.when(jnp.logical_and(bb > 0, j == N_STREAM))
        def _():
            attention_cols(h_prev, N_STREAM - 1)
            emit(r1_ref, False)

    @pl.when(bb % 2 == 0)
    def _():
        steps(h0_ref, h1_ref)

    @pl.when(bb % 2 == 1)
    def _():
        steps(h1_ref, h0_ref)


def _in_proj(x, norm_g, w_in, conv_w, conv_b, q_norm_g, k_norm_g):
    bsz = x.shape[0]
    n_tt = SEQ // TOK_TILE
    assert n_tt == N_STREAM
    o1 = N_STREAM * D_HYENA
    wht = w_in[:, :o1].T.astype(BF16)
    wa = w_in[:, o1:].astype(BF16)
    cw = jnp.concatenate([conv_w.reshape(3, 3, D_HYENA).transpose(1, 0, 2),
                          conv_b.reshape(3, 1, D_HYENA)], axis=1)
    cw = jnp.broadcast_to(cw[..., None], (3, 4, D_HYENA, LANES))
    head = jnp.arange(SEG_W) // HEAD_DIM
    seg = jnp.where(head[:, None] == head[None, :], 1.0 / HEAD_DIM, 0.0).astype(BF16)
    half = HEAD_DIM // 2
    inv = ROPE_THETA ** (-jnp.arange(half, dtype=F32) / half)
    ang = jnp.arange(SEQ, dtype=F32)[:, None] * inv[None, :]
    cos = jnp.tile(jnp.cos(ang), (1, LANES // half))
    sin = jnp.tile(jnp.concatenate([-jnp.sin(ang), jnp.sin(ang)], axis=1), (1, LANES // HEAD_DIM))
    qg = jnp.tile(q_norm_g, D_ATTN // HEAD_DIM)[None]
    kg = jnp.tile(k_norm_g, D_KV // HEAD_DIM)[None]
    const = lambda shape: pl.BlockSpec(shape, lambda b, j: (0,) * len(shape))
    last = N_STREAM - 1
    out_b = lambda bb: jnp.maximum(bb - 1, 0)
    att_t = lambda bb, j: jnp.where(bb == 0, 0, jnp.clip(j - 1, 0, last))
    tok = lambda width: pl.BlockSpec((1, TOK_TILE, width), lambda bb, j: (out_b(bb), att_t(bb, j), 0))
    return pl.pallas_call(
        _in_proj_kernel,
        grid=(bsz + 1, N_STREAM + 1),
        in_specs=[
            pl.BlockSpec((1, TOK_TILE, D_MODEL),
                         lambda bb, j: (jnp.minimum(bb, bsz - 1), jnp.minimum(j, last), 0)),
            const((1, D_MODEL)),
            pl.BlockSpec((D_HYENA, D_MODEL), lambda b, j: (jnp.minimum(j, last), 0)),
            pl.BlockSpec((1, 4, D_HYENA, LANES), lambda b, j: (jnp.clip(j - 1, 0, 2), 0, 0, 0)),
            const(wa.shape),
            const(seg.shape),
            pl.BlockSpec((TOK_TILE, LANES), lambda bb, j: (att_t(bb, j), 0)),
            pl.BlockSpec((TOK_TILE, LANES), lambda bb, j: (att_t(bb, j), 0)),
            const((1, D_ATTN)),
            const((1, D_KV)),
        ],
        out_specs=[
            pl.BlockSpec((HY_TILES, 2, N_TBLK, 1, SUBLANES, LANES),
                         lambda bb, j: (jnp.where(bb == 0, 0, jnp.maximum(j - 1, 0)), 0, 0, out_b(bb), 0, 0)),
            tok(D_ATTN), tok(2 * D_KV), tok(4 * D_KV), tok(D_ATTN),
        ],
        out_shape=[
            jax.ShapeDtypeStruct((N_HY_TILES, 2, N_TBLK, bsz, SUBLANES, LANES), F32),
            jax.ShapeDtypeStruct((bsz, SEQ, D_ATTN), BF16),
            jax.ShapeDtypeStruct((bsz, SEQ, 2 * D_KV), BF16),
            jax.ShapeDtypeStruct((bsz, SEQ, 4 * D_KV), BF16),
            jax.ShapeDtypeStruct((bsz, SEQ, D_ATTN), BF16),
        ],
        scratch_shapes=[pltpu.VMEM((SEQ, D_MODEL), BF16), pltpu.VMEM((SEQ, D_MODEL), BF16),
                        pltpu.VMEM((D_HYENA, SEQ), F32), pltpu.VMEM((D_HYENA, SEQ), F32)],
        compiler_params=pltpu.CompilerParams(dimension_semantics=("arbitrary", "arbitrary"),
                                             vmem_limit_bytes=VMEM_LIMIT),
        name="in_proj",
    )(x, norm_g[None], wht, cw, wa, seg, cos, sin, qg, kg)


STRIP_ROWS = N_SHIFT * TBLK + LANES


def _hyena_kernel(bsz, v_ref, x1_ref, x2_ref, g0_ref, g1_ref, g0n_ref, g1n_ref,
                  o_ref, *strip_refs):
    ct = pl.program_id(0)
    rows = N_TBLK * bsz
    half_rows = rows * SUBLANES
    n_pairs = STRIP_ROWS // 16
    sets = [[strip_refs[0:2], strip_refs[2:4]], [strip_refs[4:6], strip_refs[6:8]]]

    row16 = lax.broadcasted_iota(jnp.int32, (16, LANES), 0) % SUBLANES
    lane16 = lax.broadcasted_iota(jnp.int32, (16, LANES), 1)
    n_rot = LANES // 16
    inside = [lane16 >= row16] + [lane16 - row16 + 16 * r < LANES for r in range(1, n_rot)]

    def build_strip(g_ref, c, strip_ref):
        cache = {}

        def window_pair(q):
            if ("w", q) not in cache:
                rows = [jnp.broadcast_to(g_ref[pl.ds((p * N_WINQ + q) * SUBLANES + c, 1), :],
                                         (SUBLANES, LANES)) for p in range(len(WIN_PHASES))]
                cache["w", q] = rows
            return cache["w", q]

        def rotated(q, r):
            if (q, r) not in cache:
                tiles = [pltpu.roll(w, (LANES - 16 * r) % LANES, 1, stride=1, stride_axis=0)
                         for w in window_pair(q)]
                cache[q, r] = jnp.concatenate(tiles, axis=0).astype(BF16)
            return cache[q, r]

        for k in range(n_pairs, 0, -1):
            q, r = k // n_rot + 1, k % n_rot
            pair = jnp.where(inside[r], rotated(q, r), rotated(q - 1 if r == 0 else q + 1, r))
            strip_ref[16 * (n_pairs - k):16 * (n_pairs - k) + 16, :] = pair

    def weights(strip_ref, e):
        return jnp.concatenate([strip_ref[TBLK * e + LANES:TBLK * e + LANES + TBLK, :],
                                strip_ref[TBLK * e:TBLK * e + TBLK, :]], axis=1)

    def load_rows(ref, c):
        lo = ref[pl.ds(c, rows, stride=SUBLANES), :]
        hi = ref[pl.ds(half_rows + c, rows, stride=SUBLANES), :]
        return jnp.concatenate([lo, hi], axis=1)

    def build_set(s, c0, g_refs):
        for i in range(2):
            for order in range(2):
                build_strip(g_refs[order], c0 + i, sets[s][i][order])

    def long_convs(us, strips):
        ubs = [u.astype(BF16) for u in us]
        n = len(us)
        dot = lambda i, lhs, e: jnp.dot(lhs, weights(strips[i], e), preferred_element_type=F32)
        acc_p = [None] * n
        acc_n = [None] * n
        for d in range(N_TBLK - 1, 0, -1):
            for i in range(n):
                res = dot(i, ubs[i][0:rows - bsz * d], N_TBLK - 1 - d)
                acc_p[i] = res if acc_p[i] is None else jnp.concatenate(
                    [res[:bsz], res[bsz:] + acc_p[i]], axis=0)
            for i in range(n):
                res = dot(i, ubs[i][bsz * d:rows], N_TBLK - 1 + d)
                acc_n[i] = res if acc_n[i] is None else jnp.concatenate(
                    [res[:-bsz] + acc_n[i], res[-bsz:]], axis=0)
        outs = []
        for i in range(n):
            res = dot(i, ubs[i], N_TBLK - 1)
            outs.append(jnp.concatenate(
                [res[:bsz] + acc_n[i][:bsz],
                 res[bsz:rows - bsz] + acc_n[i][bsz:] + acc_p[i][:rows - 2 * bsz],
                 res[rows - bsz:] + acc_p[i][rows - 2 * bsz:]], axis=0))
        return outs

    def process_pair(c0, strips):
        cs = (c0, c0 + 1)
        hv = [load_rows(v_ref, c) for c in cs]
        conv = long_convs(hv, [strips[i][0] for i in range(2)])
        z = [load_rows(x1_ref, c) * conv[i] for i, c in enumerate(cs)]
        conv = long_convs(z, [strips[i][1] for i in range(2)])
        for i, c in enumerate(cs):
            y = load_rows(x2_ref, c) * conv[i]
            o_ref[pl.ds(c, rows, stride=SUBLANES), :] = y[:, :LANES]
            o_ref[pl.ds(half_rows + c, rows, stride=SUBLANES), :] = y[:, LANES:]

    @pl.when(ct == 0)
    def _():
        build_set(0, 0, (g0_ref, g1_ref))

    n_cp = SUBLANES // 2
    for p in range(n_cp):
        if p + 1 < n_cp:
            build_set((p + 1) % 2, 2 * (p + 1), (g0_ref, g1_ref))
        else:
            build_set((p + 1) % 2, 0, (g0n_ref, g1n_ref))
        process_pair(2 * p, sets[p % 2])


def _hyena(hy, gm, bsz):
    n_tiles = gm.shape[0] // 2
    rows = N_TBLK * bsz
    prow = 2 * rows * SUBLANES
    hy = hy.reshape(hy.shape[0], prow, LANES)
    gspec = lambda off: pl.BlockSpec((None, N_WIN * SUBLANES, LANES), lambda i: (i + off, 0, 0))
    gnext = lambda off: pl.BlockSpec((None, N_WIN * SUBLANES, LANES),
                                     lambda i: (jnp.minimum(i + 1, n_tiles - 1) + off, 0, 0))
    uspec = lambda off: pl.BlockSpec((None, prow, LANES), lambda i: (i + off, 0, 0))
    strip = pltpu.VMEM((STRIP_ROWS, LANES), BF16)
    return pl.pallas_call(
        functools.partial(_hyena_kernel, bsz),
        grid=(n_tiles,),
        in_specs=[uspec(0), uspec(n_tiles), uspec(2 * n_tiles), gspec(0), gspec(n_tiles),
                  gnext(0), gnext(n_tiles)],
        out_specs=pl.BlockSpec((None, prow, LANES), lambda i: (i, 0, 0)),
        out_shape=jax.ShapeDtypeStruct((n_tiles, prow, LANES), F32),
        scratch_shapes=[strip] * 8,
        compiler_params=pltpu.CompilerParams(dimension_semantics=("arbitrary",),
                                             vmem_limit_bytes=VMEM_LIMIT),
        name="hyena",
    )(hy, hy, hy, gm, gm, gm, gm)


ATT_QB = WINDOW
ATT_SPAN = 3 * WINDOW


def _attn_out_kernel(x_ref, yh_ref, gh_ref, q_ref, k_ref, v_ref, ga_ref, sink_ref, hg_ref, ag_ref,
                     woh_ref, woa_ref, o_ref, bias_ref, ya_ref):
    qb, span = ATT_QB, ATT_SPAN
    t = pl.program_id(1)

    @pl.when(jnp.logical_and(pl.program_id(0) == 0, t == 0))
    def _():
        r = lax.broadcasted_iota(jnp.int32, (qb, span), 0)
        c = lax.broadcasted_iota(jnp.int32, (qb, span), 1)
        for j in range(3):
            bias_ref[j] = jnp.where(jnp.abs(c - r - j * WINDOW) <= WINDOW, 0.0, -jnp.inf)

    lane = lax.broadcasted_iota(jnp.int32, (2 * qb, LANES), 1)
    low = lane < HEAD_DIM
    row2 = lax.broadcasted_iota(jnp.int32, (2 * qb, 1), 0)
    sinks = [[jnp.where(row2 < qb, sink_ref[4 * kvh + par], sink_ref[4 * kvh + 2 + par])
              for par in range(2)] for kvh in range(N_KV_HEADS)]

    for i in range(TOK_TILE // qb):
        r0 = i * qb
        q0 = pl.multiple_of(t * TOK_TILE + r0, qb)
        ws = pl.multiple_of(jnp.clip(q0 - WINDOW, 0, SEQ - span), WINDOW)
        bias1 = bias_ref[(q0 - ws) // WINDOW]
        bias = jnp.concatenate([bias1, bias1], axis=0)
        scores = []
        for kvh in range(N_KV_HEADS):
            qs = jnp.concatenate(
                [q_ref[0, r0:r0 + qb, (2 * kvh) * LANES:(2 * kvh + 1) * LANES],
                 q_ref[0, r0:r0 + qb, (2 * kvh + 1) * LANES:(2 * kvh + 2) * LANES]], axis=0)
            zero = jnp.zeros_like(qs)
            for par in range(2):
                qm = jnp.where(low, qs, zero) if par == 0 else jnp.where(low, zero, qs)
                ks = (kvh + par) % 2
                km = k_ref[0, pl.ds(ws, span), ks * LANES:(ks + 1) * LANES]
                scores.append(_nt_dot(qm, km) + bias)
        for kvh in range(N_KV_HEADS):
            parts = []
            for par in range(2):
                s = scores[2 * kvh + par]
                vs = 2 * kvh + par
                vm = v_ref[0, pl.ds(ws, span), vs * LANES:(vs + 1) * LANES]
                sink = sinks[kvh][par]
                m = jnp.maximum(jnp.max(s, axis=-1, keepdims=True), sink)
                p = jnp.exp((s - m).astype(BF16))
                parts.append((jnp.dot(p, vm, preferred_element_type=F32), jnp.exp(sink - m)))
            (o_e, sink_e), (o_o, sink_o) = parts
            num = jnp.where(low, o_e, o_o)
            den = jnp.where(low, pltpu.roll(o_e, HEAD_DIM, 1) + sink_e,
                            pltpu.roll(o_o, HEAD_DIM, 1) + sink_o)
            o = num / den
            ya_ref[r0:r0 + qb, (2 * kvh) * LANES:(2 * kvh + 1) * LANES] = o[:qb]
            ya_ref[r0:r0 + qb, (2 * kvh + 1) * LANES:(2 * kvh + 2) * LANES] = o[qb:]

    def chan_major(ref):
        parts = []
        for kk in range(TOK_TILE // LANES):
            blk, half = kk // 2, kk % 2
            parts.append(ref[:, half, blk, 0, :, :].reshape(D_HYENA, LANES))
        return jnp.concatenate(parts, axis=1)

    yh = chan_major(yh_ref)
    gh = chan_major(gh_ref)
    yh_n = yh * lax.rsqrt(jnp.mean(yh * yh, axis=0, keepdims=True) + EPS) * hg_ref[...]
    yh_g = (yh_n * (gh * jax.nn.sigmoid(gh))).astype(BF16)
    acc = lax.dot_general(yh_g, woh_ref[...], (((0,), (0,)), ((), ())), preferred_element_type=F32)

    ya = ya_ref[...]
    ga = ga_ref[0].astype(F32)
    ya_n = ya * lax.rsqrt(jnp.mean(ya * ya, axis=-1, keepdims=True) + EPS) * ag_ref[...]
    ya_g = (ya_n * (ga * jax.nn.sigmoid(ga))).astype(BF16)
    acc = acc + jnp.dot(ya_g, woa_ref[...], preferred_element_type=F32)
    o_ref[0] = x_ref[0] + acc


def _attn_out(x, yh, hy, q, k2, v4, ga, sink, hy_out_norm_g, attn_out_norm_g, w_out):
    bsz = x.shape[0]
    n_tt = SEQ // TOK_TILE
    yh = yh.reshape(HY_TILES, 2, N_TBLK, bsz, SUBLANES, LANES)
    woh = w_out[:D_HYENA].astype(BF16)
    woa = w_out[D_HYENA:].astype(BF16)
    const = lambda shape: pl.BlockSpec(shape, lambda b, t: (0,) * len(shape))
    tok = lambda width: pl.BlockSpec((1, TOK_TILE, width), lambda b, t: (b, t, 0))
    seq = lambda width: pl.BlockSpec((1, SEQ, width), lambda b, t: (b, 0, 0))
    packed = lambda tile_blk: pl.BlockSpec(
        (HY_TILES, 2, TOK_TILE // TBLK, 1, SUBLANES, LANES), lambda b, t: (tile_blk, 0, t, b, 0, 0))
    return pl.pallas_call(
        _attn_out_kernel,
        grid=(bsz, n_tt),
        in_specs=[
            tok(D_MODEL), packed(0), packed(3), tok(D_ATTN), seq(2 * D_KV), seq(4 * D_KV), tok(D_ATTN),
            pl.BlockSpec(memory_space=pltpu.SMEM),
            const((D_HYENA, 1)), const((1, D_ATTN)), const(woh.shape), const(woa.shape),
        ],
        out_specs=tok(D_MODEL),
        out_shape=jax.ShapeDtypeStruct(x.shape, x.dtype),
        scratch_shapes=[pltpu.VMEM((3, ATT_QB, ATT_SPAN), F32), pltpu.VMEM((TOK_TILE, D_ATTN), F32)],
        compiler_params=pltpu.CompilerParams(dimension_semantics=("arbitrary", "arbitrary"),
                                             vmem_limit_bytes=VMEM_LIMIT),
        name="attn_out",
    )(x, yh, hy, q, k2, v4, ga, sink, hy_out_norm_g[:, None], attn_out_norm_g[None], woh, woa)


def kernel(x, norm_g, w_in, conv_w, conv_b, filt_w1, filt_b1, filt_w2, filt_b2, filt_w3, filt_b3,
           filt_w4, filt_sin_freq, hyena_bias, q_norm_g, k_norm_g, attn_sink, hy_out_norm_g,
           attn_out_norm_g, w_out):
    bsz, seq, d_model = x.shape
    assert seq == SEQ and d_model == D_MODEL and bsz % SUBLANES == 0
    assert norm_g.shape[0] == 1, "one layer"
    gm = _filter_windows(filt_w1[0], filt_b1[0], filt_w2[0], filt_b2[0], filt_w3[0], filt_b3[0],
                         filt_w4[0], filt_sin_freq[0], hyena_bias[0])
    hy, q, k2, v2, ga = _in_proj(x, norm_g[0], w_in[0], conv_w[0], conv_b[0], q_norm_g[0], k_norm_g[0])
    yh = _hyena(hy, gm, bsz)
    return _attn_out(x, yh, hy, q, k2, v2, ga, attn_sink[0], hy_out_norm_g[0], attn_out_norm_g[0],
                     w_out[0])
```

```python
import functools
import math

import jax
import jax.numpy as jnp
from jax import lax
from jax.experimental import pallas as pl
from jax.experimental.pallas import tpu as pltpu

F32 = jnp.float32
BF16 = jnp.bfloat16
HI = lax.Precision.HIGHEST

D_MODEL = 1024
SEQ = 2048
D_HYENA = 512
D_ATTN = 512
HEAD_DIM = 64
N_KV_HEADS = 2
WINDOW = 128
ROPE_THETA = 10000.0
FILTER_HIDDEN = 64
N_BANDS = 16
DECAY_TARGET = 1e-2
FAST_DECAY_PCT = 0.3
SLOW_DECAY_PCT = 1.5
EPS = 1e-6

LANES = 128
SUBLANES = 8
PACKED_ROWS = 16
TBLK = 256
N_TBLK = SEQ // TBLK
N_SHIFT = 2 * N_TBLK - 1
N_HY_T = 4 * D_HYENA
N_HY_TILES = N_HY_T // SUBLANES
HY_TILES = D_HYENA // SUBLANES
D_KV = HEAD_DIM * N_KV_HEADS

FILT_LPAD = LANES
N_WINQ = (FILT_LPAD + 2 * SEQ) // LANES
N_POS = (N_WINQ + 1) * LANES
WIN_PHASES = (0, SUBLANES)
N_WIN = len(WIN_PHASES) * N_WINQ
FILT_CH = 128
TOK_TILE = 512
VMEM_LIMIT = 56 * 1024 * 1024


def _nt_dot(a, b, precision=None):
    return lax.dot_general(a, b, (((1,), (1,)), ((), ())), preferred_element_type=F32,
                           precision=precision)


def _filter_kernel(w1t_ref, w1c_ref, w1s_ref, b1_ref, w2_ref, b2_ref, w3_ref, b3_ref,
                   w4_ref, fr_ref, hb_ref, o_ref, hid_ref):
    centre = FILT_LPAD + SEQ
    first = jnp.logical_and(pl.program_id(0) == 0, pl.program_id(1) == 0)

    @pl.when(first)
    def _():
        off_c = lax.broadcasted_iota(jnp.int32, (N_POS, 1), 0) - centre
        idx_c = jnp.minimum(jnp.abs(off_c), SEQ - 1).astype(F32)
        t_c = idx_c / (SEQ - 1)
        w_c = (2.0 * math.pi) * idx_c / SEQ
        band = lax.broadcasted_iota(jnp.int32, (1, N_BANDS), 1).astype(F32)
        freqs = 1e-4 + band * ((N_BANDS - 1 - 1e-4) / (N_BANDS - 1))
        ang = w_c * freqs
        fr = fr_ref[...]
        pre = (t_c * w1t_ref[...]
               + jnp.dot(jnp.cos(ang), w1c_ref[...], preferred_element_type=F32, precision=HI)
               - jnp.dot(jnp.sin(ang), w1s_ref[...], preferred_element_type=F32, precision=HI)
               + b1_ref[...])
        h = jnp.sin(fr * pre)
        h = jnp.sin(fr * (jnp.dot(h, w2_ref[...], preferred_element_type=F32, precision=HI) + b2_ref[...]))
        h = jnp.sin(fr * (jnp.dot(h, w3_ref[...], preferred_element_type=F32, precision=HI) + b3_ref[...]))
        hid_ref[...] = h

    hid = hid_ref[...]
    fwd = _nt_dot(w4_ref[0, 0], hid, precision=HI)
    bwd = _nt_dot(w4_ref[0, 1], hid, precision=HI)
    off_r = lax.broadcasted_iota(jnp.int32, (1, N_POS), 1) - centre
    idx_r = jnp.minimum(jnp.abs(off_r), SEQ - 1).astype(F32)
    max_decay = math.log(DECAY_TARGET) / FAST_DECAY_PCT
    min_decay = math.log(DECAY_TARGET) / SLOW_DECAY_PCT
    chan = (pl.program_id(1) * FILT_CH
            + lax.broadcasted_iota(jnp.int32, (FILT_CH, 1), 0)).astype(F32)
    deltas = min_decay + chan * ((max_decay - min_decay) / (D_HYENA - 1))
    decay = jnp.exp(-(idx_r / (SEQ - 1)) * jnp.abs(deltas))
    decay = jnp.where(jnp.abs(off_r) <= SEQ - 1, decay, 0.0)
    val = jnp.where(off_r < 0, bwd, fwd) * decay
    val = val + jnp.where(off_r == 0, hb_ref[0], 0.0)
    for p, phase in enumerate(WIN_PHASES):
        shifted = val if phase == 0 else pltpu.roll(val, phase, 1)
        for w in range(N_WINQ):
            o_ref[:, p * N_WINQ + w, :, :] = (
                shifted[:, w * LANES:(w + 1) * LANES].reshape(FILT_CH // SUBLANES, SUBLANES, LANES))


def _filter_windows(w1, b1, w2, b2, w3, b3, w4, sin_freq, hyena_bias):
    full = lambda shape: pl.BlockSpec(shape, lambda o, j: (0,) * len(shape))
    w4r = w4.T.reshape(2, 2, D_HYENA, FILTER_HIDDEN)
    args = (w1[0:1], w1[1:1 + N_BANDS], w1[1 + N_BANDS:], b1[None], w2, b2[None], w3, b3[None])
    n_j = D_HYENA // FILT_CH
    tiles = FILT_CH // SUBLANES
    out = pl.pallas_call(
        _filter_kernel,
        grid=(2, n_j),
        in_specs=[full(a.shape) for a in args] + [
            pl.BlockSpec((1, 2, FILT_CH, FILTER_HIDDEN), lambda o, j: (o, 0, j, 0)),
            full((1, FILTER_HIDDEN)),
            pl.BlockSpec((1, FILT_CH, 1), lambda o, j: (o, j, 0)),
        ],
        out_specs=pl.BlockSpec((tiles, N_WIN, SUBLANES, LANES), lambda o, j: (o * n_j + j, 0, 0, 0)),
        out_shape=jax.ShapeDtypeStruct((2 * HY_TILES, N_WIN, SUBLANES, LANES), F32),
        scratch_shapes=[pltpu.VMEM((N_POS, FILTER_HIDDEN), F32)],
        compiler_params=pltpu.CompilerParams(dimension_semantics=("arbitrary", "arbitrary"),
                                             vmem_limit_bytes=VMEM_LIMIT),
        name="filter_windows",
    )(*args, w4r, sin_freq[None], hyena_bias[:, :, None])
    return out.reshape(2 * HY_TILES, N_WIN * SUBLANES, LANES)


N_STREAM = 4
SEG_W = 256
CONV_SUB = 64


def _in_proj_kernel(x_ref, g_ref, wht_ref, cw_ref, wa_ref, seg_ref, cos_ref, sin_ref, qg_ref, kg_ref,
                    hy_ref, q_ref, k_ref, v_ref, ga_ref, h0_ref, h1_ref, r0_ref, r1_ref):
    bb = pl.program_id(0)
    j = pl.program_id(1)
    n_tt = SEQ // TOK_TILE
    n_chunk = SEQ // LANES
    lane = lax.broadcasted_iota(jnp.int32, (TOK_TILE, LANES), 1)
    lane_c = lax.broadcasted_iota(jnp.int32, (CONV_SUB, LANES), 1)

    def normalise(h_cur):
        x = x_ref[0]
        ms = jnp.mean(x * x, axis=-1, keepdims=True)
        t0 = pl.multiple_of(j * TOK_TILE, TOK_TILE)
        h_cur[pl.ds(t0, TOK_TILE), :] = (x * lax.rsqrt(ms + EPS) * g_ref[...]).astype(BF16)

    def project(h_prev, dst_ref):
        for t in range(n_tt):
            dst_ref[:, t * TOK_TILE:(t + 1) * TOK_TILE] = _nt_dot(
                wht_ref[...], h_prev[t * TOK_TILE:(t + 1) * TOK_TILE, :])

    def emit(src_ref, conv):
        for sub in range(D_HYENA // CONV_SUB):
            rows = slice(sub * CONV_SUB, (sub + 1) * CONV_SUB)
            tiles = slice(sub * CONV_SUB // SUBLANES, (sub + 1) * CONV_SUB // SUBLANES)
            if conv:
                w0, w1, w2, wb = (cw_ref[0, i, rows, :] for i in range(4))
                chunk = lambda k: src_ref[rows, k * LANES:(k + 1) * LANES]
                prev_r = None
                cur = chunk(0)
                cur_r, cur_l = pltpu.roll(cur, 1, 1), pltpu.roll(cur, LANES - 1, 1)
            for k in range(n_chunk):
                if conv:
                    if k + 1 < n_chunk:
                        nxt = chunk(k + 1)
                        nxt_r, nxt_l = pltpu.roll(nxt, 1, 1), pltpu.roll(nxt, LANES - 1, 1)
                    um = jnp.where(lane_c == 0, 0.0 if prev_r is None else prev_r, cur_r)
                    up = jnp.where(lane_c == LANES - 1, nxt_l if k + 1 < n_chunk else 0.0, cur_l)
                    out = w0 * um + w1 * cur + w2 * up + wb
                    prev_r = cur_r
                    if k + 1 < n_chunk:
                        cur, cur_r, cur_l = nxt, nxt_r, nxt_l
                else:
                    out = src_ref[rows, k * LANES:(k + 1) * LANES]
                blk, half = k // 2, k % 2
                hy_ref[tiles, half, blk, 0, :, :] = out.reshape(CONV_SUB // SUBLANES, SUBLANES, LANES)

    def attention_cols(h_prev, t):
        att = jnp.dot(h_prev[t * TOK_TILE:(t + 1) * TOK_TILE, :], wa_ref[...],
                      preferred_element_type=F32)
        cos = cos_ref[...]
        sin = sin_ref[...]
        first_half = (lane % HEAD_DIM) < (HEAD_DIM // 2)

        def norm_rope(t, gain):
            sq = (t * t).astype(BF16)
            w = min(t.shape[1], SEG_W)
            seg = seg_ref[:w, :w]
            slabs = [jnp.dot(sq[:, i:i + w], seg, preferred_element_type=F32)
                     for i in range(0, t.shape[1], w)]
            ms_h = slabs[0] if len(slabs) == 1 else jnp.concatenate(slabs, axis=1)
            tn = t * lax.rsqrt(ms_h + EPS) * gain
            outs = []
            for i in range(t.shape[1] // LANES):
                c = tn[:, i * LANES:(i + 1) * LANES]
                swapped = jnp.where(first_half, pltpu.roll(c, LANES - HEAD_DIM // 2, 1),
                                    pltpu.roll(c, HEAD_DIM // 2, 1))
                outs.append(c * cos + swapped * sin)
            return outs[0] if len(outs) == 1 else jnp.concatenate(outs, axis=1)

        q = norm_rope(att[:, :D_ATTN], qg_ref[...])
        k = norm_rope(att[:, D_ATTN:D_ATTN + D_KV], kg_ref[...])
        v = att[:, D_ATTN + D_KV:D_ATTN + 2 * D_KV]
        q_ref[0] = (q * (HEAD_DIM ** -0.5)).astype(BF16)
        k_ref[0] = jnp.concatenate([k, pltpu.roll(k, HEAD_DIM, 1)], axis=1).astype(BF16)
        v_sw = pltpu.roll(v, HEAD_DIM, 1)
        low = lane < HEAD_DIM
        v_ref[0] = jnp.concatenate([jnp.where(low, v, 1.0), jnp.where(low, 1.0, v_sw),
                                    jnp.where(low, v_sw, 1.0), jnp.where(low, 1.0, v)], axis=1).astype(BF16)
        ga_ref[0] = att[:, D_ATTN + 2 * D_KV:].astype(BF16)

    def steps(h_cur, h_prev):
        @pl.when(jnp.logical_and(bb == 0, j < N_STREAM))
        def _():
            normalise(h_cur)

        @pl.when(jnp.logical_and(bb > 0, j == 0))
        def _():
            normalise(h_cur)
            project(h_prev, r0_ref)

        for step in range(1, N_STREAM):
            @pl.when(jnp.logical_and(bb > 0, j == step))
            def _(step=step):
                attention_cols(h_prev, step - 1)
                normalise(h_cur)
                project(h_prev, r1_ref if step % 2 else r0_ref)
                emit(r0_ref if step % 2 else r1_ref, True)

        @pl.when(jnp.logical_and(bb > 0, j == N_STREAM))
        def _():
            attention_cols(h_prev, N_STREAM - 1)
            emit(r1_ref, False)

    @pl.when(bb % 2 == 0)
    def _():
        steps(h0_ref, h1_ref)

    @pl.when(bb % 2 == 1)
    def _():
        steps(h1_ref, h0_ref)


def _in_proj(x, norm_g, w_in, conv_w, conv_b, q_norm_g, k_norm_g):
    bsz = x.shape[0]
    n_tt = SEQ // TOK_TILE
    assert n_tt == N_STREAM
    o1 = N_STREAM * D_HYENA
    wht = w_in[:, :o1].T.astype(BF16)
    wa = w_in[:, o1:].astype(BF16)
    cw = jnp.concatenate([conv_w.reshape(3, 3, D_HYENA).transpose(1, 0, 2),
                          conv_b.reshape(3, 1, D_HYENA)], axis=1)
    cw = jnp.broadcast_to(cw[..., None], (3, 4, D_HYENA, LANES))
    head = jnp.arange(SEG_W) // HEAD_DIM
    seg = jnp.where(head[:, None] == head[None, :], 1.0 / HEAD_DIM, 0.0).astype(BF16)
    half = HEAD_DIM // 2
    inv = ROPE_THETA ** (-jnp.arange(half, dtype=F32) / half)
    ang = jnp.arange(SEQ, dtype=F32)[:, None] * inv[None, :]
    cos = jnp.tile(jnp.cos(ang), (1, LANES // half))
    sin = jnp.tile(jnp.concatenate([-jnp.sin(ang), jnp.sin(ang)], axis=1), (1, LANES // HEAD_DIM))
    qg = jnp.tile(q_norm_g, D_ATTN // HEAD_DIM)[None]
    kg = jnp.tile(k_norm_g, D_KV // HEAD_DIM)[None]
    const = lambda shape: pl.BlockSpec(shape, lambda b, j: (0,) * len(shape))
    last = N_STREAM - 1
    out_b = lambda bb: jnp.maximum(bb - 1, 0)
    att_t = lambda bb, j: jnp.where(bb == 0, 0, jnp.clip(j - 1, 0, last))
    tok = lambda width: pl.BlockSpec((1, TOK_TILE, width), lambda bb, j: (out_b(bb), att_t(bb, j), 0))
    return pl.pallas_call(
        _in_proj_kernel,
        grid=(bsz + 1, N_STREAM + 1),
        in_specs=[
            pl.BlockSpec((1, TOK_TILE, D_MODEL),
                         lambda bb, j: (jnp.minimum(bb, bsz - 1), jnp.minimum(j, last), 0)),
            const((1, D_MODEL)),
            pl.BlockSpec((D_HYENA, D_MODEL), lambda b, j: (jnp.minimum(j, last), 0)),
            pl.BlockSpec((1, 4, D_HYENA, LANES), lambda b, j: (jnp.clip(j - 1, 0, 2), 0, 0, 0)),
            const(wa.shape),
            const(seg.shape),
            pl.BlockSpec((TOK_TILE, LANES), lambda bb, j: (att_t(bb, j), 0)),
            pl.BlockSpec((TOK_TILE, LANES), lambda bb, j: (att_t(bb, j), 0)),
            const((1, D_ATTN)),
            const((1, D_KV)),
        ],
        out_specs=[
            pl.BlockSpec((HY_TILES, 2, N_TBLK, 1, SUBLANES, LANES),
                         lambda bb, j: (jnp.where(bb == 0, 0, jnp.maximum(j - 1, 0)), 0, 0, out_b(bb), 0, 0)),
            tok(D_ATTN), tok(2 * D_KV), tok(4 * D_KV), tok(D_ATTN),
        ],
        out_shape=[
            jax.ShapeDtypeStruct((N_HY_TILES, 2, N_TBLK, bsz, SUBLANES, LANES), F32),
            jax.ShapeDtypeStruct((bsz, SEQ, D_ATTN), BF16),
            jax.ShapeDtypeStruct((bsz, SEQ, 2 * D_KV), BF16),
            jax.ShapeDtypeStruct((bsz, SEQ, 4 * D_KV), BF16),
            jax.ShapeDtypeStruct((bsz, SEQ, D_ATTN), BF16),
        ],
        scratch_shapes=[pltpu.VMEM((SEQ, D_MODEL), BF16), pltpu.VMEM((SEQ, D_MODEL), BF16),
                        pltpu.VMEM((D_HYENA, SEQ), F32), pltpu.VMEM((D_HYENA, SEQ), F32)],
        compiler_params=pltpu.CompilerParams(dimension_semantics=("arbitrary", "arbitrary"),
                                             vmem_limit_bytes=VMEM_LIMIT),
        name="in_proj",
    )(x, norm_g[None], wht, cw, wa, seg, cos, sin, qg, kg)


STRIP_ROWS = N_SHIFT * TBLK + LANES


def _hyena_kernel(bsz, v_ref, x1_ref, x2_ref, g0_ref, g1_ref, g0n_ref, g1n_ref,
                  o_ref, *strip_refs):
    ct = pl.program_id(0)
    rows = N_TBLK * bsz
    half_rows = rows * SUBLANES
    pk = PACKED_ROWS
    n_pairs = STRIP_ROWS // pk
    sets = [[strip_refs[0:2], strip_refs[2:4]], [strip_refs[4:6], strip_refs[6:8]]]
    acc_refs = strip_refs[8:10]

    row_s = lax.broadcasted_iota(jnp.int32, (pk, LANES), 0) % SUBLANES
    lane_s = lax.broadcasted_iota(jnp.int32, (pk, LANES), 1)
    n_rot = LANES // pk
    inside = [lane_s >= row_s] + [lane_s - row_s + pk * r < LANES for r in range(1, n_rot)]

    def build_strip(g_ref, c, strip_ref):
        cache = {}

        def window_pair(q):
            if ("w", q) not in cache:
                rows = [jnp.broadcast_to(g_ref[pl.ds((p * N_WINQ + q) * SUBLANES + c, 1), :],
                                         (SUBLANES, LANES)) for p in range(len(WIN_PHASES))]
                cache["w", q] = rows
            return cache["w", q]

        def rotated(q, r):
            if (q, r) not in cache:
                tiles = [pltpu.roll(w, (LANES - pk * r) % LANES, 1, stride=1, stride_axis=0)
                         for w in window_pair(q)]
                cache[q, r] = jnp.concatenate(tiles, axis=0).astype(BF16)
            return cache[q, r]

        for k in range(n_pairs, 0, -1):
            q, r = k // n_rot + 1, k % n_rot
            pair = jnp.where(inside[r], rotated(q, r), rotated(q - 1 if r == 0 else q + 1, r))
            strip_ref[pk * (n_pairs - k):pk * (n_pairs - k) + pk, :] = pair

    def weights(strip_ref, e):
        return jnp.concatenate([strip_ref[TBLK * e + LANES:TBLK * e + LANES + TBLK, :],
                                strip_ref[TBLK * e:TBLK * e + TBLK, :]], axis=1)

    def load_rows(ref, c):
        lo = ref[pl.ds(c, rows, stride=SUBLANES), :]
        hi = ref[pl.ds(half_rows + c, rows, stride=SUBLANES), :]
        return jnp.concatenate([lo, hi], axis=1)

    def build_set(s, c0, g_refs):
        for i in range(2):
            for order in range(2):
                build_strip(g_refs[order], c0 + i, sets[s][i][order])

    def long_convs(us, strips):
        ubs = [u.astype(BF16) for u in us]
        n = len(us)
        dot = lambda i, lhs, e: jnp.dot(lhs, weights(strips[i], e), preferred_element_type=F32)
        for i in range(n):
            acc_refs[i][...] = dot(i, ubs[i], N_TBLK - 1)
        for d in range(1, N_TBLK):
            for i in range(n):
                acc_refs[i][bsz * d:rows, :] += dot(i, ubs[i][0:rows - bsz * d], N_TBLK - 1 - d)
            for i in range(n):
                acc_refs[i][0:rows - bsz * d, :] += dot(i, ubs[i][bsz * d:rows], N_TBLK - 1 + d)
        return [acc_refs[i][...] for i in range(n)]

    def process_pair(c0, strips):
        cs = (c0, c0 + 1)
        hv = [load_rows(v_ref, c) for c in cs]
        conv = long_convs(hv, [strips[i][0] for i in range(2)])
        z = [load_rows(x1_ref, c) * conv[i] for i, c in enumerate(cs)]
        conv = long_convs(z, [strips[i][1] for i in range(2)])
        for i, c in enumerate(cs):
            y = load_rows(x2_ref, c) * conv[i]
            o_ref[pl.ds(c, rows, stride=SUBLANES), :] = y[:, :LANES]
            o_ref[pl.ds(half_rows + c, rows, stride=SUBLANES), :] = y[:, LANES:]

    @pl.when(ct == 0)
    def _():
        build_set(0, 0, (g0_ref, g1_ref))

    n_cp = SUBLANES // 2
    for p in range(n_cp):
        if p + 1 < n_cp:
            build_set((p + 1) % 2, 2 * (p + 1), (g0_ref, g1_ref))
        else:
            build_set((p + 1) % 2, 0, (g0n_ref, g1n_ref))
        process_pair(2 * p, sets[p % 2])


def _hyena(hy, gm, bsz):
    n_tiles = gm.shape[0] // 2
    rows = N_TBLK * bsz
    prow = 2 * rows * SUBLANES
    hy = hy.reshape(hy.shape[0], prow, LANES)
    gspec = lambda off: pl.BlockSpec((None, N_WIN * SUBLANES, LANES), lambda i: (i + off, 0, 0))
    gnext = lambda off: pl.BlockSpec((None, N_WIN * SUBLANES, LANES),
                                     lambda i: (jnp.minimum(i + 1, n_tiles - 1) + off, 0, 0))
    uspec = lambda off: pl.BlockSpec((None, prow, LANES), lambda i: (i + off, 0, 0))
    strip = pltpu.VMEM((STRIP_ROWS, LANES), BF16)
    return pl.pallas_call(
        functools.partial(_hyena_kernel, bsz),
        grid=(n_tiles,),
        in_specs=[uspec(0), uspec(n_tiles), uspec(2 * n_tiles), gspec(0), gspec(n_tiles),
                  gnext(0), gnext(n_tiles)],
        out_specs=pl.BlockSpec((None, prow, LANES), lambda i: (i, 0, 0)),
        out_shape=jax.ShapeDtypeStruct((n_tiles, prow, LANES), F32),
        scratch_shapes=[strip] * 8 + [pltpu.VMEM((rows, TBLK), F32)] * 2,
        compiler_params=pltpu.CompilerParams(dimension_semantics=("arbitrary",),
                                             vmem_limit_bytes=VMEM_LIMIT),
        name="hyena",
    )(hy, hy, hy, gm, gm, gm, gm)


ATT_QB = WINDOW
ATT_SPAN = 3 * WINDOW


def _attn_out_kernel(x_ref, yh_ref, gh_ref, q_ref, k_ref, v_ref, ga_ref, sink_ref, hg_ref, ag_ref,
                     woh_ref, woa_ref, o_ref, bias_ref, ya_ref):
    qb, span = ATT_QB, ATT_SPAN
    t = pl.program_id(1)

    @pl.when(jnp.logical_and(pl.program_id(0) == 0, t == 0))
    def _():
        r = lax.broadcasted_iota(jnp.int32, (qb, span), 0)
        c = lax.broadcasted_iota(jnp.int32, (qb, span), 1)
        for j in range(3):
            bias_ref[j] = jnp.where(jnp.abs(c - r - j * WINDOW) <= WINDOW, 0.0, -jnp.inf)

    lane = lax.broadcasted_iota(jnp.int32, (2 * qb, LANES), 1)
    low = lane < HEAD_DIM
    row2 = lax.broadcasted_iota(jnp.int32, (2 * qb, 1), 0)
    sinks = [[jnp.where(row2 < qb, sink_ref[4 * kvh + par], sink_ref[4 * kvh + 2 + par])
              for par in range(2)] for kvh in range(N_KV_HEADS)]

    for i in range(TOK_TILE // qb):
        r0 = i * qb
        q0 = pl.multiple_of(t * TOK_TILE + r0, qb)
        ws = pl.multiple_of(jnp.clip(q0 - WINDOW, 0, SEQ - span), WINDOW)
        bias1 = bias_ref[(q0 - ws) // WINDOW]
        bias = jnp.concatenate([bias1, bias1], axis=0)
        scores = []
        for kvh in range(N_KV_HEADS):
            qs = jnp.concatenate(
                [q_ref[0, r0:r0 + qb, (2 * kvh) * LANES:(2 * kvh + 1) * LANES],
                 q_ref[0, r0:r0 + qb, (2 * kvh + 1) * LANES:(2 * kvh + 2) * LANES]], axis=0)
            zero = jnp.zeros_like(qs)
            for par in range(2):
                qm = jnp.where(low, qs, zero) if par == 0 else jnp.where(low, zero, qs)
                ks = (kvh + par) % 2
                km = k_ref[0, pl.ds(ws, span), ks * LANES:(ks + 1) * LANES]
                scores.append(_nt_dot(qm, km) + bias)
        for kvh in range(N_KV_HEADS):
            parts = []
            for par in range(2):
                s = scores[2 * kvh + par]
                vs = 2 * kvh + par
                vm = v_ref[0, pl.ds(ws, span), vs * LANES:(vs + 1) * LANES]
                sink = sinks[kvh][par]
                m = jnp.maximum(jnp.max(s, axis=-1, keepdims=True), sink)
                p = jnp.exp((s - m).astype(BF16))
                parts.append((jnp.dot(p, vm, preferred_element_type=F32), jnp.exp(sink - m)))
            (o_e, sink_e), (o_o, sink_o) = parts
            num = jnp.where(low, o_e, o_o)
            den = jnp.where(low, pltpu.roll(o_e, HEAD_DIM, 1) + sink_e,
                            pltpu.roll(o_o, HEAD_DIM, 1) + sink_o)
            o = num / den
            ya_ref[r0:r0 + qb, (2 * kvh) * LANES:(2 * kvh + 1) * LANES] = o[:qb]
            ya_ref[r0:r0 + qb, (2 * kvh + 1) * LANES:(2 * kvh + 2) * LANES] = o[qb:]

    def chan_major(ref):
        parts = []
        for kk in range(TOK_TILE // LANES):
            blk, half = kk // 2, kk % 2
            parts.append(ref[:, half, blk, 0, :, :].reshape(D_HYENA, LANES))
        return jnp.concatenate(parts, axis=1)

    yh = chan_major(yh_ref)
    gh = chan_major(gh_ref)
    yh_n = yh * lax.rsqrt(jnp.mean(yh * yh, axis=0, keepdims=True) + EPS) * hg_ref[...]
    yh_g = (yh_n * (gh * jax.nn.sigmoid(gh))).astype(BF16)
    acc = lax.dot_general(yh_g, woh_ref[...], (((0,), (0,)), ((), ())), preferred_element_type=F32)

    ya = ya_ref[...]
    ga = ga_ref[0].astype(F32)
    ya_n = ya * lax.rsqrt(jnp.mean(ya * ya, axis=-1, keepdims=True) + EPS) * ag_ref[...]
    ya_g = (ya_n * (ga * jax.nn.sigmoid(ga))).astype(BF16)
    acc = acc + jnp.dot(ya_g, woa_ref[...], preferred_element_type=F32)
    o_ref[0] = x_ref[0] + acc


def _attn_out(x, yh, hy, q, k2, v4, ga, sink, hy_out_norm_g, attn_out_norm_g, w_out):
    bsz = x.shape[0]
    n_tt = SEQ // TOK_TILE
    yh = yh.reshape(HY_TILES, 2, N_TBLK, bsz, SUBLANES, LANES)
    woh = w_out[:D_HYENA].astype(BF16)
    woa = w_out[D_HYENA:].astype(BF16)
    const = lambda shape: pl.BlockSpec(shape, lambda b, t: (0,) * len(shape))
    tok = lambda width: pl.BlockSpec((1, TOK_TILE, width), lambda b, t: (b, t, 0))
    seq = lambda width: pl.BlockSpec((1, SEQ, width), lambda b, t: (b, 0, 0))
    packed = lambda tile_blk: pl.BlockSpec(
        (HY_TILES, 2, TOK_TILE // TBLK, 1, SUBLANES, LANES), lambda b, t: (tile_blk, 0, t, b, 0, 0))
    return pl.pallas_call(
        _attn_out_kernel,
        grid=(bsz, n_tt),
        in_specs=[
            tok(D_MODEL), packed(0), packed(3), tok(D_ATTN), seq(2 * D_KV), seq(4 * D_KV), tok(D_ATTN),
            pl.BlockSpec(memory_space=pltpu.SMEM),
            const((D_HYENA, 1)), const((1, D_ATTN)), const(woh.shape), const(woa.shape),
        ],
        out_specs=tok(D_MODEL),
        out_shape=jax.ShapeDtypeStruct(x.shape, x.dtype),
        scratch_shapes=[pltpu.VMEM((3, ATT_QB, ATT_SPAN), F32), pltpu.VMEM((TOK_TILE, D_ATTN), F32)],
        compiler_params=pltpu.CompilerParams(dimension_semantics=("arbitrary", "arbitrary"),
                                             vmem_limit_bytes=VMEM_LIMIT),
        name="attn_out",
    )(x, yh, hy, q, k2, v4, ga, sink, hy_out_norm_g[:, None], attn_out_norm_g[None], woh, woa)


def kernel(x, norm_g, w_in, conv_w, conv_b, filt_w1, filt_b1, filt_w2, filt_b2, filt_w3, filt_b3,
           filt_w4, filt_sin_freq, hyena_bias, q_norm_g, k_norm_g, attn_sink, hy_out_norm_g,
           attn_out_norm_g, w_out):
    bsz, seq, d_model = x.shape
    assert seq == SEQ and d_model == D_MODEL and bsz % SUBLANES == 0
    assert norm_g.shape[0] == 1, "one layer"
    gm = _filter_windows(filt_w1[0], filt_b1[0], filt_w2[0], filt_b2[0], filt_w3[0], filt_b3[0],
                         filt_w4[0], filt_sin_freq[0], hyena_bias[0])
    hy, q, k2, v2, ga = _in_proj(x, norm_g[0], w_in[0], conv_w[0], conv_b[0], q_norm_g[0], k_norm_g[0])
    yh = _hyena(hy, gm, bsz)
    return _attn_out(x, yh, hy, q, k2, v2, ga, attn_sink[0], hy_out_norm_g[0], attn_out_norm_g[0],
                     w_out[0])
```

```python
import functools
import math

import jax
import jax.numpy as jnp
from jax import lax
from jax.experimental import pallas as pl
from jax.experimental.pallas import tpu as pltpu

F32 = jnp.float32
BF16 = jnp.bfloat16
HI = lax.Precision.HIGHEST

D_MODEL = 1024
SEQ = 2048
D_HYENA = 512
D_ATTN = 512
HEAD_DIM = 64
N_KV_HEADS = 2
WINDOW = 128
ROPE_THETA = 10000.0
FILTER_HIDDEN = 64
N_BANDS = 16
DECAY_TARGET = 1e-2
FAST_DECAY_PCT = 0.3
SLOW_DECAY_PCT = 1.5
EPS = 1e-6

LANES = 128
SUBLANES = 8
PACKED_ROWS = 16
TBLK = 256
N_TBLK = SEQ // TBLK
N_SHIFT = 2 * N_TBLK - 1
N_HY_T = 4 * D_HYENA
N_HY_TILES = N_HY_T // SUBLANES
HY_TILES = D_HYENA // SUBLANES
D_KV = HEAD_DIM * N_KV_HEADS

FILT_LPAD = LANES
N_WINQ = (FILT_LPAD + 2 * SEQ) // LANES
N_POS = (N_WINQ + 1) * LANES
WIN_PHASES = (0, SUBLANES)
N_WIN = len(WIN_PHASES) * N_WINQ
FILT_CH = 128
TOK_TILE = 512
VMEM_LIMIT = 56 * 1024 * 1024


def _nt_dot(a, b, precision=None):
    return lax.dot_general(a, b, (((1,), (1,)), ((), ())), preferred_element_type=F32,
                           precision=precision)


def _filter_kernel(w1t_ref, w1c_ref, w1s_ref, b1_ref, w2_ref, b2_ref, w3_ref, b3_ref,
                   w4_ref, fr_ref, hb_ref, o_ref, hid_ref):
    centre = FILT_LPAD + SEQ
    first = jnp.logical_and(pl.program_id(0) == 0, pl.program_id(1) == 0)

    @pl.when(first)
    def _():
        off_c = lax.broadcasted_iota(jnp.int32, (N_POS, 1), 0) - centre
        idx_c = jnp.minimum(jnp.abs(off_c), SEQ - 1).astype(F32)
        t_c = idx_c / (SEQ - 1)
        w_c = (2.0 * math.pi) * idx_c / SEQ
        band = lax.broadcasted_iota(jnp.int32, (1, N_BANDS), 1).astype(F32)
        freqs = 1e-4 + band * ((N_BANDS - 1 - 1e-4) / (N_BANDS - 1))
        ang = w_c * freqs
        fr = fr_ref[...]
        pre = (t_c * w1t_ref[...]
               + jnp.dot(jnp.cos(ang), w1c_ref[...], preferred_element_type=F32, precision=HI)
               - jnp.dot(jnp.sin(ang), w1s_ref[...], preferred_element_type=F32, precision=HI)
               + b1_ref[...])
        h = jnp.sin(fr * pre)
        h = jnp.sin(fr * (jnp.dot(h, w2_ref[...], preferred_element_type=F32, precision=HI) + b2_ref[...]))
        h = jnp.sin(fr * (jnp.dot(h, w3_ref[...], preferred_element_type=F32, precision=HI) + b3_ref[...]))
        hid_ref[...] = h

    hid = hid_ref[...]
    fwd = _nt_dot(w4_ref[0, 0], hid, precision=HI)
    bwd = _nt_dot(w4_ref[0, 1], hid, precision=HI)
    off_r = lax.broadcasted_iota(jnp.int32, (1, N_POS), 1) - centre
    idx_r = jnp.minimum(jnp.abs(off_r), SEQ - 1).astype(F32)
    max_decay = math.log(DECAY_TARGET) / FAST_DECAY_PCT
    min_decay = math.log(DECAY_TARGET) / SLOW_DECAY_PCT
    chan = (pl.program_id(1) * FILT_CH
            + lax.broadcasted_iota(jnp.int32, (FILT_CH, 1), 0)).astype(F32)
    deltas = min_decay + chan * ((max_decay - min_decay) / (D_HYENA - 1))
    decay = jnp.exp(-(idx_r / (SEQ - 1)) * jnp.abs(deltas))
    decay = jnp.where(jnp.abs(off_r) <= SEQ - 1, decay, 0.0)
    val = jnp.where(off_r < 0, bwd, fwd) * decay
    val = val + jnp.where(off_r == 0, hb_ref[0], 0.0)
    for p, phase in enumerate(WIN_PHASES):
        shifted = val if phase == 0 else pltpu.roll(val, phase, 1)
        for w in range(N_WINQ):
            o_ref[:, p * N_WINQ + w, :, :] = (
                shifted[:, w * LANES:(w + 1) * LANES].reshape(FILT_CH // SUBLANES, SUBLANES, LANES))


def _filter_windows(w1, b1, w2, b2, w3, b3, w4, sin_freq, hyena_bias):
    full = lambda shape: pl.BlockSpec(shape, lambda o, j: (0,) * len(shape))
    w4r = w4.T.reshape(2, 2, D_HYENA, FILTER_HIDDEN)
    args = (w1[0:1], w1[1:1 + N_BANDS], w1[1 + N_BANDS:], b1[None], w2, b2[None], w3, b3[None])
    n_j = D_HYENA // FILT_CH
    tiles = FILT_CH // SUBLANES
    out = pl.pallas_call(
        _filter_kernel,
        grid=(2, n_j),
        in_specs=[full(a.shape) for a in args] + [
            pl.BlockSpec((1, 2, FILT_CH, FILTER_HIDDEN), lambda o, j: (o, 0, j, 0)),
            full((1, FILTER_HIDDEN)),
            pl.BlockSpec((1, FILT_CH, 1), lambda o, j: (o, j, 0)),
        ],
        out_specs=pl.BlockSpec((tiles, N_WIN, SUBLANES, LANES), lambda o, j: (o * n_j + j, 0, 0, 0)),
        out_shape=jax.ShapeDtypeStruct((2 * HY_TILES, N_WIN, SUBLANES, LANES), F32),
        scratch_shapes=[pltpu.VMEM((N_POS, FILTER_HIDDEN), F32)],
        compiler_params=pltpu.CompilerParams(dimension_semantics=("arbitrary", "arbitrary"),
                                             vmem_limit_bytes=VMEM_LIMIT),
        name="filter_windows",
    )(*args, w4r, sin_freq[None], hyena_bias[:, :, None])
    return out.reshape(2 * HY_TILES, N_WIN * SUBLANES, LANES)


N_STREAM = 4
SEG_W = 256
CONV_SUB = 64


def _in_proj_kernel(x_ref, g_ref, wht_ref, cw_ref, wa_ref, seg_ref, cos_ref, sin_ref, qg_ref, kg_ref,
                    hy_ref, q_ref, k_ref, v_ref, ga_ref, h0_ref, h1_ref, r0_ref, r1_ref):
    bb = pl.program_id(0)
    j = pl.program_id(1)
    n_tt = SEQ // TOK_TILE
    n_chunk = SEQ // LANES
    lane = lax.broadcasted_iota(jnp.int32, (TOK_TILE, LANES), 1)
    lane_c = lax.broadcasted_iota(jnp.int32, (CONV_SUB, LANES), 1)

    def normalise(h_cur):
        x = x_ref[0]
        ms = jnp.mean(x * x, axis=-1, keepdims=True)
        t0 = pl.multiple_of(j * TOK_TILE, TOK_TILE)
        h_cur[pl.ds(t0, TOK_TILE), :] = (x * lax.rsqrt(ms + EPS) * g_ref[...]).astype(BF16)

    def project(h_prev, dst_ref):
        for t in range(n_tt):
            dst_ref[:, t * TOK_TILE:(t + 1) * TOK_TILE] = _nt_dot(
                wht_ref[...], h_prev[t * TOK_TILE:(t + 1) * TOK_TILE, :])

    def emit(src_ref, conv):
        for sub in range(D_HYENA // CONV_SUB):
            rows = slice(sub * CONV_SUB, (sub + 1) * CONV_SUB)
            tiles = slice(sub * CONV_SUB // SUBLANES, (sub + 1) * CONV_SUB // SUBLANES)
            if conv:
                w0, w1, w2, wb = (cw_ref[0, i, rows, :] for i in range(4))
                chunk = lambda k: src_ref[rows, k * LANES:(k + 1) * LANES]
                prev_r = None
                cur = chunk(0)
                cur_r, cur_l = pltpu.roll(cur, 1, 1), pltpu.roll(cur, LANES - 1, 1)
            for k in range(n_chunk):
                if conv:
                    if k + 1 < n_chunk:
                        nxt = chunk(k + 1)
                        nxt_r, nxt_l = pltpu.roll(nxt, 1, 1), pltpu.roll(nxt, LANES - 1, 1)
                    um = jnp.where(lane_c == 0, 0.0 if prev_r is None else prev_r, cur_r)
                    up = jnp.where(lane_c == LANES - 1, nxt_l if k + 1 < n_chunk else 0.0, cur_l)
                    out = w0 * um + w1 * cur + w2 * up + wb
                    prev_r = cur_r
                    if k + 1 < n_chunk:
                        cur, cur_r, cur_l = nxt, nxt_r, nxt_l
                else:
                    out = src_ref[rows, k * LANES:(k + 1) * LANES]
                blk, half = k // 2, k % 2
                hy_ref[tiles, half, blk, 0, :, :] = out.reshape(CONV_SUB // SUBLANES, SUBLANES, LANES)

    def attention_cols(h_prev, t):
        att = jnp.dot(h_prev[t * TOK_TILE:(t + 1) * TOK_TILE, :], wa_ref[...],
                      preferred_element_type=F32)
        cos = cos_ref[...]
        sin = sin_ref[...]
        first_half = (lane % HEAD_DIM) < (HEAD_DIM // 2)

        def norm_rope(t, gain):
            sq = (t * t).astype(BF16)
            w = min(t.shape[1], SEG_W)
            seg = seg_ref[:w, :w]
            slabs = [jnp.dot(sq[:, i:i + w], seg, preferred_element_type=F32)
                     for i in range(0, t.shape[1], w)]
            ms_h = slabs[0] if len(slabs) == 1 else jnp.concatenate(slabs, axis=1)
            tn = t * lax.rsqrt(ms_h + EPS) * gain
            outs = []
            for i in range(t.shape[1] // LANES):
                c = tn[:, i * LANES:(i + 1) * LANES]
                swapped = jnp.where(first_half, pltpu.roll(c, LANES - HEAD_DIM // 2, 1),
                                    pltpu.roll(c, HEAD_DIM // 2, 1))
                outs.append(c * cos + swapped * sin)
            return outs[0] if len(outs) == 1 else jnp.concatenate(outs, axis=1)

        q = norm_rope(att[:, :D_ATTN], qg_ref[...])
        k = norm_rope(att[:, D_ATTN:D_ATTN + D_KV], kg_ref[...])
        v = att[:, D_ATTN + D_KV:D_ATTN + 2 * D_KV]
        q_ref[0] = (q * (HEAD_DIM ** -0.5)).astype(BF16)
        k_ref[0] = jnp.concatenate([k, pltpu.roll(k, HEAD_DIM, 1)], axis=1).astype(BF16)
        v_sw = pltpu.roll(v, HEAD_DIM, 1)
        low = lane < HEAD_DIM
        v_ref[0] = jnp.concatenate([jnp.where(low, v, 1.0), jnp.where(low, 1.0, v_sw),
                                    jnp.where(low, v_sw, 1.0), jnp.where(low, 1.0, v)], axis=1).astype(BF16)
        ga_ref[0] = att[:, D_ATTN + 2 * D_KV:].astype(BF16)

    def steps(h_cur, h_prev):
        @pl.when(jnp.logical_and(bb == 0, j < N_STREAM))
        def _():
            normalise(h_cur)

        @pl.when(jnp.logical_and(bb > 0, j == 0))
        def _():
            normalise(h_cur)
            project(h_prev, r0_ref)

        for step in range(1, N_STREAM):
            @pl.when(jnp.logical_and(bb > 0, j == step))
            def _(step=step):
                attention_cols(h_prev, step - 1)
                normalise(h_cur)
                project(h_prev, r1_ref if step % 2 else r0_ref)
                emit(r0_ref if step % 2 else r1_ref, True)

        @pl.when(jnp.logical_and(bb > 0, j == N_STREAM))
        def _():
            attention_cols(h_prev, N_STREAM - 1)
            emit(r1_ref, False)

    @pl.when(bb % 2 == 0)
    def _():
        steps(h0_ref, h1_ref)

    @pl.when(bb % 2 == 1)
    def _():
        steps(h1_ref, h0_ref)


def _in_proj(x, norm_g, w_in, conv_w, conv_b, q_norm_g, k_norm_g):
    bsz = x.shape[0]
    n_tt = SEQ // TOK_TILE
    assert n_tt == N_STREAM
    o1 = N_STREAM * D_HYENA
    wht = w_in[:, :o1].T.astype(BF16)
    wa = w_in[:, o1:].astype(BF16)
    cw = jnp.concatenate([conv_w.reshape(3, 3, D_HYENA).transpose(1, 0, 2),
                          conv_b.reshape(3, 1, D_HYENA)], axis=1)
    cw = jnp.broadcast_to(cw[..., None], (3, 4, D_HYENA, LANES))
    head = jnp.arange(SEG_W) // HEAD_DIM
    seg = jnp.where(head[:, None] == head[None, :], 1.0 / HEAD_DIM, 0.0).astype(BF16)
    half = HEAD_DIM // 2
    inv = ROPE_THETA ** (-jnp.arange(half, dtype=F32) / half)
    ang = jnp.arange(SEQ, dtype=F32)[:, None] * inv[None, :]
    cos = jnp.tile(jnp.cos(ang), (1, LANES // half))
    sin = jnp.tile(jnp.concatenate([-jnp.sin(ang), jnp.sin(ang)], axis=1), (1, LANES // HEAD_DIM))
    qg = jnp.tile(q_norm_g, D_ATTN // HEAD_DIM)[None]
    kg = jnp.tile(k_norm_g, D_KV // HEAD_DIM)[None]
    const = lambda shape: pl.BlockSpec(shape, lambda b, j: (0,) * len(shape))
    last = N_STREAM - 1
    out_b = lambda bb: jnp.maximum(bb - 1, 0)
    att_t = lambda bb, j: jnp.where(bb == 0, 0, jnp.clip(j - 1, 0, last))
    tok = lambda width: pl.BlockSpec((1, TOK_TILE, width), lambda bb, j: (out_b(bb), att_t(bb, j), 0))
    return pl.pallas_call(
        _in_proj_kernel,
        grid=(bsz + 1, N_STREAM + 1),
        in_specs=[
            pl.BlockSpec((1, TOK_TILE, D_MODEL),
                         lambda bb, j: (jnp.minimum(bb, bsz - 1), jnp.minimum(j, last), 0)),
            const((1, D_MODEL)),
            pl.BlockSpec((D_HYENA, D_MODEL), lambda b, j: (jnp.minimum(j, last), 0)),
            pl.BlockSpec((1, 4, D_HYENA, LANES), lambda b, j: (jnp.clip(j - 1, 0, 2), 0, 0, 0)),
            const(wa.shape),
            const(seg.shape),
            pl.BlockSpec((TOK_TILE, LANES), lambda bb, j: (att_t(bb, j), 0)),
            pl.BlockSpec((TOK_TILE, LANES), lambda bb, j: (att_t(bb, j), 0)),
            const((1, D_ATTN)),
            const((1, D_KV)),
        ],
        out_specs=[
            pl.BlockSpec((HY_TILES, 2, N_TBLK, 1, SUBLANES, LANES),
                         lambda bb, j: (jnp.where(bb == 0, 0, jnp.maximum(j - 1, 0)), 0, 0, out_b(bb), 0, 0)),
            tok(D_ATTN), tok(2 * D_KV), tok(4 * D_KV), tok(D_ATTN),
        ],
        out_shape=[
            jax.ShapeDtypeStruct((N_HY_TILES, 2, N_TBLK, bsz, SUBLANES, LANES), F32),
            jax.ShapeDtypeStruct((bsz, SEQ, D_ATTN), BF16),
            jax.ShapeDtypeStruct((bsz, SEQ, 2 * D_KV), BF16),
            jax.ShapeDtypeStruct((bsz, SEQ, 4 * D_KV), BF16),
            jax.ShapeDtypeStruct((bsz, SEQ, D_ATTN), BF16),
        ],
        scratch_shapes=[pltpu.VMEM((SEQ, D_MODEL), BF16), pltpu.VMEM((SEQ, D_MODEL), BF16),
                        pltpu.VMEM((D_HYENA, SEQ), F32), pltpu.VMEM((D_HYENA, SEQ), F32)],
        compiler_params=pltpu.CompilerParams(dimension_semantics=("arbitrary", "arbitrary"),
                                             vmem_limit_bytes=VMEM_LIMIT),
        name="in_proj",
    )(x, norm_g[None], wht, cw, wa, seg, cos, sin, qg, kg)


STRIP_ROWS = N_SHIFT * TBLK + LANES


def _hyena_kernel(bsz, v_ref, x1_ref, x2_ref, g0_ref, g1_ref, g0n_ref, g1n_ref,
                  o_ref, *strip_refs):
    ct = pl.program_id(0)
    rows = N_TBLK * bsz
    half_rows = rows * SUBLANES
    pk = PACKED_ROWS
    n_pairs = STRIP_ROWS // pk
    sets = [[strip_refs[0:2], strip_refs[2:4]], [strip_refs[4:6], strip_refs[6:8]]]
    acc_refs = strip_refs[8:10]

    row_s = lax.broadcasted_iota(jnp.int32, (pk, LANES), 0) % SUBLANES
    lane_s = lax.broadcasted_iota(jnp.int32, (pk, LANES), 1)
    n_rot = LANES // pk
    inside = [lane_s >= row_s] + [lane_s - row_s + pk * r < LANES for r in range(1, n_rot)]

    def build_strip(g_ref, c, strip_ref):
        cache = {}

        def window_pair(q):
            if ("w", q) not in cache:
                rows = [jnp.broadcast_to(g_ref[pl.ds((p * N_WINQ + q) * SUBLANES + c, 1), :],
                                         (SUBLANES, LANES)) for p in range(len(WIN_PHASES))]
                cache["w", q] = rows
            return cache["w", q]

        def rotated(q, r):
            if (q, r) not in cache:
                tiles = [pltpu.roll(w, (LANES - pk * r) % LANES, 1, stride=1, stride_axis=0)
                         for w in window_pair(q)]
                cache[q, r] = jnp.concatenate(tiles, axis=0).astype(BF16)
            return cache[q, r]

        for k in range(n_pairs, 0, -1):
            q, r = k // n_rot + 1, k % n_rot
            pair = jnp.where(inside[r], rotated(q, r), rotated(q - 1 if r == 0 else q + 1, r))
            strip_ref[pk * (n_pairs - k):pk * (n_pairs - k) + pk, :] = pair

    def weights(strip_ref, e):
        return jnp.concatenate([strip_ref[TBLK * e + LANES:TBLK * e + LANES + TBLK, :],
                                strip_ref[TBLK * e:TBLK * e + TBLK, :]], axis=1)

    def load_rows(ref, c):
        lo = ref[pl.ds(c, rows, stride=SUBLANES), :]
        hi = ref[pl.ds(half_rows + c, rows, stride=SUBLANES), :]
        return jnp.concatenate([lo, hi], axis=1)

    def build_set(s, c0, g_refs):
        for i in range(2):
            for order in range(2):
                build_strip(g_refs[order], c0 + i, sets[s][i][order])

    def long_convs(us, strips):
        ubs = [u.astype(BF16) for u in us]
        n = len(us)
        dot = lambda i, lhs, e: jnp.dot(lhs, weights(strips[i], e), preferred_element_type=F32)
        for i in range(n):
            acc_refs[i][...] = dot(i, ubs[i], N_TBLK - 1)
        for d in range(1, N_TBLK):
            for i in range(n):
                acc_refs[i][bsz * d:rows, :] += dot(i, ubs[i][0:rows - bsz * d], N_TBLK - 1 - d)
            for i in range(n):
                acc_refs[i][0:rows - bsz * d, :] += dot(i, ubs[i][bsz * d:rows], N_TBLK - 1 + d)
        return [acc_refs[i][...] for i in range(n)]

    def process_pair(c0, strips):
        cs = (c0, c0 + 1)
        hv = [load_rows(v_ref, c) for c in cs]
        conv = long_convs(hv, [strips[i][0] for i in range(2)])
        z = [load_rows(x1_ref, c) * conv[i] for i, c in enumerate(cs)]
        conv = long_convs(z, [strips[i][1] for i in range(2)])
        for i, c in enumerate(cs):
            y = load_rows(x2_ref, c) * conv[i]
            o_ref[pl.ds(c, rows, stride=SUBLANES), :] = y[:, :LANES]
            o_ref[pl.ds(half_rows + c, rows, stride=SUBLANES), :] = y[:, LANES:]

    @pl.when(ct == 0)
    def _():
        build_set(0, 0, (g0_ref, g1_ref))

    n_cp = SUBLANES // 2
    for p in range(n_cp):
        if p + 1 < n_cp:
            build_set((p + 1) % 2, 2 * (p + 1), (g0_ref, g1_ref))
        else:
            build_set((p + 1) % 2, 0, (g0n_ref, g1n_ref))
        process_pair(2 * p, sets[p % 2])


def _hyena(hy, gm, bsz):
    n_tiles = gm.shape[0] // 2
    rows = N_TBLK * bsz
    prow = 2 * rows * SUBLANES
    hy = hy.reshape(hy.shape[0], prow, LANES)
    gspec = lambda off: pl.BlockSpec((None, N_WIN * SUBLANES, LANES), lambda i: (i + off, 0, 0))
    gnext = lambda off: pl.BlockSpec((None, N_WIN * SUBLANES, LANES),
                                     lambda i: (jnp.minimum(i + 1, n_tiles - 1) + off, 0, 0))
    uspec = lambda off: pl.BlockSpec((None, prow, LANES), lambda i: (i + off, 0, 0))
    strip = pltpu.VMEM((STRIP_ROWS, LANES), BF16)
    return pl.pallas_call(
        functools.partial(_hyena_kernel, bsz),
        grid=(n_tiles,),
        in_specs=[uspec(0), uspec(n_tiles), uspec(2 * n_tiles), gspec(0), gspec(n_tiles),
                  gnext(0), gnext(n_tiles)],
        out_specs=pl.BlockSpec((None, prow, LANES), lambda i: (i, 0, 0)),
        out_shape=jax.ShapeDtypeStruct((n_tiles, prow, LANES), F32),
        scratch_shapes=[strip] * 8 + [pltpu.VMEM((rows, TBLK), F32)] * 2,
        compiler_params=pltpu.CompilerParams(dimension_semantics=("arbitrary",),
                                             vmem_limit_bytes=VMEM_LIMIT),
        name="hyena",
    )(hy, hy, hy, gm, gm, gm, gm)


ATT_QB = WINDOW
ATT_SPAN = 3 * WINDOW
ATT_TILE = 1024


def _attn_out_kernel(x_ref, yh_ref, gh_ref, q_ref, k_ref, v_ref, ga_ref, sink_ref, hg_ref, ag_ref,
                     woh_ref, woa_ref, o_ref, bias_ref, ya_ref):
    qb, span = ATT_QB, ATT_SPAN
    t = pl.program_id(1)

    @pl.when(jnp.logical_and(pl.program_id(0) == 0, t == 0))
    def _():
        r = lax.broadcasted_iota(jnp.int32, (qb, span), 0)
        c = lax.broadcasted_iota(jnp.int32, (qb, span), 1)
        for j in range(3):
            bias_ref[j] = jnp.where(jnp.abs(c - r - j * WINDOW) <= WINDOW, 0.0, -jnp.inf)

    lane = lax.broadcasted_iota(jnp.int32, (2 * qb, LANES), 1)
    low = lane < HEAD_DIM
    row2 = lax.broadcasted_iota(jnp.int32, (2 * qb, 1), 0)
    sinks = [[jnp.where(row2 < qb, sink_ref[4 * kvh + par], sink_ref[4 * kvh + 2 + par])
              for par in range(2)] for kvh in range(N_KV_HEADS)]

    for i in range(ATT_TILE // qb):
        r0 = i * qb
        q0 = pl.multiple_of(t * ATT_TILE + r0, qb)
        ws = pl.multiple_of(jnp.clip(q0 - WINDOW, 0, SEQ - span), WINDOW)
        bias1 = bias_ref[(q0 - ws) // WINDOW]
        bias = jnp.concatenate([bias1, bias1], axis=0)
        scores = []
        for kvh in range(N_KV_HEADS):
            qs = jnp.concatenate(
                [q_ref[0, r0:r0 + qb, (2 * kvh) * LANES:(2 * kvh + 1) * LANES],
                 q_ref[0, r0:r0 + qb, (2 * kvh + 1) * LANES:(2 * kvh + 2) * LANES]], axis=0)
            zero = jnp.zeros_like(qs)
            for par in range(2):
                qm = jnp.where(low, qs, zero) if par == 0 else jnp.where(low, zero, qs)
                ks = (kvh + par) % 2
                km = k_ref[0, pl.ds(ws, span), ks * LANES:(ks + 1) * LANES]
                scores.append(_nt_dot(qm, km) + bias)
        for kvh in range(N_KV_HEADS):
            parts = []
            for par in range(2):
                s = scores[2 * kvh + par]
                vs = 2 * kvh + par
                vm = v_ref[0, pl.ds(ws, span), vs * LANES:(vs + 1) * LANES]
                sink = sinks[kvh][par]
                m = jnp.maximum(jnp.max(s, axis=-1, keepdims=True), sink)
                p = jnp.exp((s - m).astype(BF16))
                parts.append((jnp.dot(p, vm, preferred_element_type=F32), jnp.exp(sink - m)))
            (o_e, sink_e), (o_o, sink_o) = parts
            num = jnp.where(low, o_e, o_o)
            den = jnp.where(low, pltpu.roll(o_e, HEAD_DIM, 1) + sink_e,
                            pltpu.roll(o_o, HEAD_DIM, 1) + sink_o)
            o = num / den
            ya_ref[r0:r0 + qb, (2 * kvh) * LANES:(2 * kvh + 1) * LANES] = o[:qb]
            ya_ref[r0:r0 + qb, (2 * kvh + 1) * LANES:(2 * kvh + 2) * LANES] = o[qb:]

    def chan_major(ref):
        parts = []
        for kk in range(ATT_TILE // LANES):
            blk, half = kk // 2, kk % 2
            parts.append(ref[:, half, blk, 0, :, :].reshape(D_HYENA, LANES))
        return jnp.concatenate(parts, axis=1)

    yh = chan_major(yh_ref)
    gh = chan_major(gh_ref)
    yh_n = yh * lax.rsqrt(jnp.mean(yh * yh, axis=0, keepdims=True) + EPS) * hg_ref[...]
    yh_g = (yh_n * (gh * jax.nn.sigmoid(gh))).astype(BF16)
    acc = lax.dot_general(yh_g, woh_ref[...], (((0,), (0,)), ((), ())), preferred_element_type=F32)

    ya = ya_ref[...]
    ga = ga_ref[0].astype(F32)
    ya_n = ya * lax.rsqrt(jnp.mean(ya * ya, axis=-1, keepdims=True) + EPS) * ag_ref[...]
    ya_g = (ya_n * (ga * jax.nn.sigmoid(ga))).astype(BF16)
    acc = acc + jnp.dot(ya_g, woa_ref[...], preferred_element_type=F32)
    o_ref[0] = x_ref[0] + acc


def _attn_out(x, yh, hy, q, k2, v4, ga, sink, hy_out_norm_g, attn_out_norm_g, w_out):
    bsz = x.shape[0]
    n_tt = SEQ // ATT_TILE
    yh = yh.reshape(HY_TILES, 2, N_TBLK, bsz, SUBLANES, LANES)
    woh = w_out[:D_HYENA].astype(BF16)
    woa = w_out[D_HYENA:].astype(BF16)
    const = lambda shape: pl.BlockSpec(shape, lambda b, t: (0,) * len(shape))
    tok = lambda width: pl.BlockSpec((1, ATT_TILE, width), lambda b, t: (b, t, 0))
    seq = lambda width: pl.BlockSpec((1, SEQ, width), lambda b, t: (b, 0, 0))
    packed = lambda tile_blk: pl.BlockSpec(
        (HY_TILES, 2, ATT_TILE // TBLK, 1, SUBLANES, LANES), lambda b, t: (tile_blk, 0, t, b, 0, 0))
    return pl.pallas_call(
        _attn_out_kernel,
        grid=(bsz, n_tt),
        in_specs=[
            tok(D_MODEL), packed(0), packed(3), tok(D_ATTN), seq(2 * D_KV), seq(4 * D_KV), tok(D_ATTN),
            pl.BlockSpec(memory_space=pltpu.SMEM),
            const((D_HYENA, 1)), const((1, D_ATTN)), const(woh.shape), const(woa.shape),
        ],
        out_specs=tok(D_MODEL),
        out_shape=jax.ShapeDtypeStruct(x.shape, x.dtype),
        scratch_shapes=[pltpu.VMEM((3, ATT_QB, ATT_SPAN), F32), pltpu.VMEM((ATT_TILE, D_ATTN), F32)],
        compiler_params=pltpu.CompilerParams(dimension_semantics=("arbitrary", "arbitrary"),
                                             vmem_limit_bytes=VMEM_LIMIT),
        name="attn_out",
    )(x, yh, hy, q, k2, v4, ga, sink, hy_out_norm_g[:, None], attn_out_norm_g[None], woh, woa)


def kernel(x, norm_g, w_in, conv_w, conv_b, filt_w1, filt_b1, filt_w2, filt_b2, filt_w3, filt_b3,
           filt_w4, filt_sin_freq, hyena_bias, q_norm_g, k_norm_g, attn_sink, hy_out_norm_g,
           attn_out_norm_g, w_out):
    bsz, seq, d_model = x.shape
    assert seq == SEQ and d_model == D_MODEL and bsz % SUBLANES == 0
    assert norm_g.shape[0] == 1, "one layer"
    gm = _filter_windows(filt_w1[0], filt_b1[0], filt_w2[0], filt_b2[0], filt_w3[0], filt_b3[0],
                         filt_w4[0], filt_sin_freq[0], hyena_bias[0])
    hy, q, k2, v2, ga = _in_proj(x, norm_g[0], w_in[0], conv_w[0], conv_b[0], q_norm_g[0], k_norm_g[0])
    yh = _hyena(hy, gm, bsz)
    return _attn_out(x, yh, hy, q, k2, v2, ga, attn_sink[0], hy_out_norm_g[0], attn_out_norm_g[0],
                     w_out[0])
```

```python
import functools
import math

import jax
import jax.numpy as jnp
from jax import lax
from jax.experimental import pallas as pl
from jax.experimental.pallas import tpu as pltpu

F32 = jnp.float32
BF16 = jnp.bfloat16
HI = lax.Precision.HIGHEST

D_MODEL = 1024
SEQ = 2048
D_HYENA = 512
D_ATTN = 512
HEAD_DIM = 64
N_KV_HEADS = 2
WINDOW = 128
ROPE_THETA = 10000.0
FILTER_HIDDEN = 64
N_BANDS = 16
DECAY_TARGET = 1e-2
FAST_DECAY_PCT = 0.3
SLOW_DECAY_PCT = 1.5
EPS = 1e-6

LANES = 128
SUBLANES = 8
PACKED_ROWS = 16
TBLK = 256
N_TBLK = SEQ // TBLK
N_SHIFT = 2 * N_TBLK - 1
N_HY_T = 4 * D_HYENA
N_HY_TILES = N_HY_T // SUBLANES
HY_TILES = D_HYENA // SUBLANES
D_KV = HEAD_DIM * N_KV_HEADS

FILT_LPAD = LANES
N_WINQ = (FILT_LPAD + 2 * SEQ) // LANES
N_POS = (N_WINQ + 1) * LANES
WIN_PHASES = (0, SUBLANES)
N_WIN = len(WIN_PHASES) * N_WINQ
FILT_CH = 128
TOK_TILE = 512
VMEM_LIMIT = 56 * 1024 * 1024


def _nt_dot(a, b, precision=None):
    return lax.dot_general(a, b, (((1,), (1,)), ((), ())), preferred_element_type=F32,
                           precision=precision)


def _filter_kernel(w1t_ref, w1c_ref, w1s_ref, b1_ref, w2_ref, b2_ref, w3_ref, b3_ref,
                   w4_ref, fr_ref, hb_ref, o_ref, hid_ref):
    centre = FILT_LPAD + SEQ
    first = jnp.logical_and(pl.program_id(0) == 0, pl.program_id(1) == 0)

    @pl.when(first)
    def _():
        off_c = lax.broadcasted_iota(jnp.int32, (N_POS, 1), 0) - centre
        idx_c = jnp.minimum(jnp.abs(off_c), SEQ - 1).astype(F32)
        t_c = idx_c / (SEQ - 1)
        w_c = (2.0 * math.pi) * idx_c / SEQ
        band = lax.broadcasted_iota(jnp.int32, (1, N_BANDS), 1).astype(F32)
        freqs = 1e-4 + band * ((N_BANDS - 1 - 1e-4) / (N_BANDS - 1))
        ang = w_c * freqs
        fr = fr_ref[...]
        pre = (t_c * w1t_ref[...]
               + jnp.dot(jnp.cos(ang), w1c_ref[...], preferred_element_type=F32, precision=HI)
               - jnp.dot(jnp.sin(ang), w1s_ref[...], preferred_element_type=F32, precision=HI)
               + b1_ref[...])
        h = jnp.sin(fr * pre)
        h = jnp.sin(fr * (jnp.dot(h, w2_ref[...], preferred_element_type=F32, precision=HI) + b2_ref[...]))
        h = jnp.sin(fr * (jnp.dot(h, w3_ref[...], preferred_element_type=F32, precision=HI) + b3_ref[...]))
        hid_ref[...] = h

    hid = hid_ref[...]
    fwd = _nt_dot(w4_ref[0, 0], hid, precision=HI)
    bwd = _nt_dot(w4_ref[0, 1], hid, precision=HI)
    off_r = lax.broadcasted_iota(jnp.int32, (1, N_POS), 1) - centre
    idx_r = jnp.minimum(jnp.abs(off_r), SEQ - 1).astype(F32)
    max_decay = math.log(DECAY_TARGET) / FAST_DECAY_PCT
    min_decay = math.log(DECAY_TARGET) / SLOW_DECAY_PCT
    chan = (pl.program_id(1) * FILT_CH
            + lax.broadcasted_iota(jnp.int32, (FILT_CH, 1), 0)).astype(F32)
    deltas = min_decay + chan * ((max_decay - min_decay) / (D_HYENA - 1))
    decay = jnp.exp(-(idx_r / (SEQ - 1)) * jnp.abs(deltas))
    decay = jnp.where(jnp.abs(off_r) <= SEQ - 1, decay, 0.0)
    val = jnp.where(off_r < 0, bwd, fwd) * decay
    val = val + jnp.where(off_r == 0, hb_ref[0], 0.0)
    for p, phase in enumerate(WIN_PHASES):
        shifted = val if phase == 0 else pltpu.roll(val, phase, 1)
        for w in range(N_WINQ):
            o_ref[:, p * N_WINQ + w, :, :] = (
                shifted[:, w * LANES:(w + 1) * LANES].reshape(FILT_CH // SUBLANES, SUBLANES, LANES))


def _filter_windows(w1, b1, w2, b2, w3, b3, w4, sin_freq, hyena_bias):
    full = lambda shape: pl.BlockSpec(shape, lambda o, j: (0,) * len(shape))
    w4r = w4.T.reshape(2, 2, D_HYENA, FILTER_HIDDEN)
    args = (w1[0:1], w1[1:1 + N_BANDS], w1[1 + N_BANDS:], b1[None], w2, b2[None], w3, b3[None])
    n_j = D_HYENA // FILT_CH
    tiles = FILT_CH // SUBLANES
    out = pl.pallas_call(
        _filter_kernel,
        grid=(2, n_j),
        in_specs=[full(a.shape) for a in args] + [
            pl.BlockSpec((1, 2, FILT_CH, FILTER_HIDDEN), lambda o, j: (o, 0, j, 0)),
            full((1, FILTER_HIDDEN)),
            pl.BlockSpec((1, FILT_CH, 1), lambda o, j: (o, j, 0)),
        ],
        out_specs=pl.BlockSpec((tiles, N_WIN, SUBLANES, LANES), lambda o, j: (o * n_j + j, 0, 0, 0)),
        out_shape=jax.ShapeDtypeStruct((2 * HY_TILES, N_WIN, SUBLANES, LANES), F32),
        scratch_shapes=[pltpu.VMEM((N_POS, FILTER_HIDDEN), F32)],
        compiler_params=pltpu.CompilerParams(dimension_semantics=("arbitrary", "arbitrary"),
                                             vmem_limit_bytes=VMEM_LIMIT),
        name="filter_windows",
    )(*args, w4r, sin_freq[None], hyena_bias[:, :, None])
    return out.reshape(2 * HY_TILES, N_WIN * SUBLANES, LANES)


N_STREAM = 4
SEG_W = 256
CONV_SUB = 32


def _in_proj_kernel(x_ref, g_ref, wht_ref, cw_ref, wa_ref, seg_ref, cos_ref, sin_ref, qg_ref, kg_ref,
                    hy_ref, q_ref, k_ref, v_ref, ga_ref, h0_ref, h1_ref, r0_ref, r1_ref):
    bb = pl.program_id(0)
    j = pl.program_id(1)
    n_tt = SEQ // TOK_TILE
    n_chunk = SEQ // LANES
    lane = lax.broadcasted_iota(jnp.int32, (TOK_TILE, LANES), 1)
    lane_c = lax.broadcasted_iota(jnp.int32, (CONV_SUB, LANES), 1)

    def normalise(h_cur):
        x = x_ref[0]
        ms = jnp.mean(x * x, axis=-1, keepdims=True)
        t0 = pl.multiple_of(j * TOK_TILE, TOK_TILE)
        h_cur[pl.ds(t0, TOK_TILE), :] = (x * lax.rsqrt(ms + EPS) * g_ref[...]).astype(BF16)

    def project(h_prev, dst_ref):
        for t in range(n_tt):
            dst_ref[:, t * TOK_TILE:(t + 1) * TOK_TILE] = _nt_dot(
                wht_ref[...], h_prev[t * TOK_TILE:(t + 1) * TOK_TILE, :])

    def emit(src_ref, conv):
        for sub in range(D_HYENA // CONV_SUB):
            rows = slice(sub * CONV_SUB, (sub + 1) * CONV_SUB)
            tiles = slice(sub * CONV_SUB // SUBLANES, (sub + 1) * CONV_SUB // SUBLANES)
            if conv:
                w0, w1, w2, wb = (cw_ref[0, i, rows, :] for i in range(4))
                chunk = lambda k: src_ref[rows, k * LANES:(k + 1) * LANES]
                prev_r = None
                cur = chunk(0)
                cur_r, cur_l = pltpu.roll(cur, 1, 1), pltpu.roll(cur, LANES - 1, 1)
            for k in range(n_chunk):
                if conv:
                    if k + 1 < n_chunk:
                        nxt = chunk(k + 1)
                        nxt_r, nxt_l = pltpu.roll(nxt, 1, 1), pltpu.roll(nxt, LANES - 1, 1)
                    um = jnp.where(lane_c == 0, 0.0 if prev_r is None else prev_r, cur_r)
                    up = jnp.where(lane_c == LANES - 1, nxt_l if k + 1 < n_chunk else 0.0, cur_l)
                    out = w0 * um + w1 * cur + w2 * up + wb
                    prev_r = cur_r
                    if k + 1 < n_chunk:
                        cur, cur_r, cur_l = nxt, nxt_r, nxt_l
                else:
                    out = src_ref[rows, k * LANES:(k + 1) * LANES]
                blk, half = k // 2, k % 2
                hy_ref[tiles, half, blk, 0, :, :] = out.reshape(CONV_SUB // SUBLANES, SUBLANES, LANES)

    def attention_cols(h_prev, t):
        att = jnp.dot(h_prev[t * TOK_TILE:(t + 1) * TOK_TILE, :], wa_ref[...],
                      preferred_element_type=F32)
        cos = cos_ref[...]
        sin = sin_ref[...]
        first_half = (lane % HEAD_DIM) < (HEAD_DIM // 2)

        def norm_rope(t, gain):
            sq = (t * t).astype(BF16)
            w = min(t.shape[1], SEG_W)
            seg = seg_ref[:w, :w]
            slabs = [jnp.dot(sq[:, i:i + w], seg, preferred_element_type=F32)
                     for i in range(0, t.shape[1], w)]
            ms_h = slabs[0] if len(slabs) == 1 else jnp.concatenate(slabs, axis=1)
            tn = t * lax.rsqrt(ms_h + EPS) * gain
            outs = []
            for i in range(t.shape[1] // LANES):
                c = tn[:, i * LANES:(i + 1) * LANES]
                swapped = jnp.where(first_half, pltpu.roll(c, LANES - HEAD_DIM // 2, 1),
                                    pltpu.roll(c, HEAD_DIM // 2, 1))
                outs.append(c * cos + swapped * sin)
            return outs[0] if len(outs) == 1 else jnp.concatenate(outs, axis=1)

        q = norm_rope(att[:, :D_ATTN], qg_ref[...])
        k = norm_rope(att[:, D_ATTN:D_ATTN + D_KV], kg_ref[...])
        v = att[:, D_ATTN + D_KV:D_ATTN + 2 * D_KV]
        q_ref[0] = (q * (HEAD_DIM ** -0.5)).astype(BF16)
        k_ref[0] = jnp.concatenate([k, pltpu.roll(k, HEAD_DIM, 1)], axis=1).astype(BF16)
        v_sw = pltpu.roll(v, HEAD_DIM, 1)
        low = lane < HEAD_DIM
        v_ref[0] = jnp.concatenate([jnp.where(low, v, 1.0), jnp.where(low, 1.0, v_sw),
                                    jnp.where(low, v_sw, 1.0), jnp.where(low, 1.0, v)], axis=1).astype(BF16)
        ga_ref[0] = att[:, D_ATTN + 2 * D_KV:].astype(BF16)

    def steps(h_cur, h_prev):
        @pl.when(jnp.logical_and(bb == 0, j < N_STREAM))
        def _():
            normalise(h_cur)

        @pl.when(jnp.logical_and(bb > 0, j == 0))
        def _():
            normalise(h_cur)
            project(h_prev, r0_ref)

        for step in range(1, N_STREAM):
            @pl.when(jnp.logical_and(bb > 0, j == step))
            def _(step=step):
                attention_cols(h_prev, step - 1)
                normalise(h_cur)
                project(h_prev, r1_ref if step % 2 else r0_ref)
                emit(r0_ref if step % 2 else r1_ref, True)

        @pl.when(jnp.logical_and(bb > 0, j == N_STREAM))
        def _():
            attention_cols(h_prev, N_STREAM - 1)
            emit(r1_ref, False)

    @pl.when(bb % 2 == 0)
    def _():
        steps(h0_ref, h1_ref)

    @pl.when(bb % 2 == 1)
    def _():
        steps(h1_ref, h0_ref)


def _in_proj(x, norm_g, w_in, conv_w, conv_b, q_norm_g, k_norm_g):
    bsz = x.shape[0]
    n_tt = SEQ // TOK_TILE
    assert n_tt == N_STREAM
    o1 = N_STREAM * D_HYENA
    wht = w_in[:, :o1].T.astype(BF16)
    wa = w_in[:, o1:].astype(BF16)
    cw = jnp.concatenate([conv_w.reshape(3, 3, D_HYENA).transpose(1, 0, 2),
                          conv_b.reshape(3, 1, D_HYENA)], axis=1)
    cw = jnp.broadcast_to(cw[..., None], (3, 4, D_HYENA, LANES))
    head = jnp.arange(SEG_W) // HEAD_DIM
    seg = jnp.where(head[:, None] == head[None, :], 1.0 / HEAD_DIM, 0.0).astype(BF16)
    half = HEAD_DIM // 2
    inv = ROPE_THETA ** (-jnp.arange(half, dtype=F32) / half)
    ang = jnp.arange(SEQ, dtype=F32)[:, None] * inv[None, :]
    cos = jnp.tile(jnp.cos(ang), (1, LANES // half))
    sin = jnp.tile(jnp.concatenate([-jnp.sin(ang), jnp.sin(ang)], axis=1), (1, LANES // HEAD_DIM))
    qg = jnp.tile(q_norm_g, D_ATTN // HEAD_DIM)[None]
    kg = jnp.tile(k_norm_g, D_KV // HEAD_DIM)[None]
    const = lambda shape: pl.BlockSpec(shape, lambda b, j: (0,) * len(shape))
    last = N_STREAM - 1
    out_b = lambda bb: jnp.maximum(bb - 1, 0)
    att_t = lambda bb, j: jnp.where(bb == 0, 0, jnp.clip(j - 1, 0, last))
    tok = lambda width: pl.BlockSpec((1, TOK_TILE, width), lambda bb, j: (out_b(bb), att_t(bb, j), 0))
    return pl.pallas_call(
        _in_proj_kernel,
        grid=(bsz + 1, N_STREAM + 1),
        in_specs=[
            pl.BlockSpec((1, TOK_TILE, D_MODEL),
                         lambda bb, j: (jnp.minimum(bb, bsz - 1), jnp.minimum(j, last), 0)),
            const((1, D_MODEL)),
            pl.BlockSpec((D_HYENA, D_MODEL), lambda b, j: (jnp.minimum(j, last), 0)),
            pl.BlockSpec((1, 4, D_HYENA, LANES), lambda b, j: (jnp.clip(j - 1, 0, 2), 0, 0, 0)),
            const(wa.shape),
            const(seg.shape),
            pl.BlockSpec((TOK_TILE, LANES), lambda bb, j: (att_t(bb, j), 0)),
            pl.BlockSpec((TOK_TILE, LANES), lambda bb, j: (att_t(bb, j), 0)),
            const((1, D_ATTN)),
            const((1, D_KV)),
        ],
        out_specs=[
            pl.BlockSpec((HY_TILES, 2, N_TBLK, 1, SUBLANES, LANES),
                         lambda bb, j: (jnp.where(bb == 0, 0, jnp.maximum(j - 1, 0)), 0, 0, out_b(bb), 0, 0)),
            tok(D_ATTN), tok(2 * D_KV), tok(4 * D_KV), tok(D_ATTN),
        ],
        out_shape=[
            jax.ShapeDtypeStruct((N_HY_TILES, 2, N_TBLK, bsz, SUBLANES, LANES), F32),
            jax.ShapeDtypeStruct((bsz, SEQ, D_ATTN), BF16),
            jax.ShapeDtypeStruct((bsz, SEQ, 2 * D_KV), BF16),
            jax.ShapeDtypeStruct((bsz, SEQ, 4 * D_KV), BF16),
            jax.ShapeDtypeStruct((bsz, SEQ, D_ATTN), BF16),
        ],
        scratch_shapes=[pltpu.VMEM((SEQ, D_MODEL), BF16), pltpu.VMEM((SEQ, D_MODEL), BF16),
                        pltpu.VMEM((D_HYENA, SEQ), F32), pltpu.VMEM((D_HYENA, SEQ), F32)],
        compiler_params=pltpu.CompilerParams(dimension_semantics=("arbitrary", "arbitrary"),
                                             vmem_limit_bytes=VMEM_LIMIT),
        name="in_proj",
    )(x, norm_g[None], wht, cw, wa, seg, cos, sin, qg, kg)


STRIP_ROWS = N_SHIFT * TBLK + LANES


def _hyena_kernel(bsz, v_ref, x1_ref, x2_ref, g0_ref, g1_ref, g0n_ref, g1n_ref,
                  o_ref, *strip_refs):
    ct = pl.program_id(0)
    rows = N_TBLK * bsz
    half_rows = rows * SUBLANES
    pk = PACKED_ROWS
    n_pairs = STRIP_ROWS // pk
    sets = [[strip_refs[0:2], strip_refs[2:4]], [strip_refs[4:6], strip_refs[6:8]]]
    acc_refs = strip_refs[8:10]

    row_s = lax.broadcasted_iota(jnp.int32, (pk, LANES), 0) % SUBLANES
    lane_s = lax.broadcasted_iota(jnp.int32, (pk, LANES), 1)
    n_rot = LANES // pk
    inside = [lane_s >= row_s] + [lane_s - row_s + pk * r < LANES for r in range(1, n_rot)]

    def build_strip(g_ref, c, strip_ref):
        cache = {}

        def window_pair(q):
            if ("w", q) not in cache:
                rows = [jnp.broadcast_to(g_ref[pl.ds((p * N_WINQ + q) * SUBLANES + c, 1), :],
                                         (SUBLANES, LANES)) for p in range(len(WIN_PHASES))]
                cache["w", q] = rows
            return cache["w", q]

        def rotated(q, r):
            if (q, r) not in cache:
                tiles = [pltpu.roll(w, (LANES - pk * r) % LANES, 1, stride=1, stride_axis=0)
                         for w in window_pair(q)]
                cache[q, r] = jnp.concatenate(tiles, axis=0).astype(BF16)
            return cache[q, r]

        for k in range(n_pairs, 0, -1):
            q, r = k // n_rot + 1, k % n_rot
            pair = jnp.where(inside[r], rotated(q, r), rotated(q - 1 if r == 0 else q + 1, r))
            strip_ref[pk * (n_pairs - k):pk * (n_pairs - k) + pk, :] = pair

    def weights(strip_ref, e):
        return jnp.concatenate([strip_ref[TBLK * e + LANES:TBLK * e + LANES + TBLK, :],
                                strip_ref[TBLK * e:TBLK * e + TBLK, :]], axis=1)

    def load_rows(ref, c):
        lo = ref[pl.ds(c, rows, stride=SUBLANES), :]
        hi = ref[pl.ds(half_rows + c, rows, stride=SUBLANES), :]
        return jnp.concatenate([lo, hi], axis=1)

    def build_set(s, c0, g_refs):
        for i in range(2):
            for order in range(2):
                build_strip(g_refs[order], c0 + i, sets[s][i][order])

    def long_convs(us, strips):
        ubs = [u.astype(BF16) for u in us]
        n = len(us)
        dot = lambda i, lhs, e: jnp.dot(lhs, weights(strips[i], e), preferred_element_type=F32)
        for i in range(n):
            acc_refs[i][...] = dot(i, ubs[i], N_TBLK - 1)
        for d in range(1, N_TBLK):
            for i in range(n):
                acc_refs[i][bsz * d:rows, :] += dot(i, ubs[i][0:rows - bsz * d], N_TBLK - 1 - d)
            for i in range(n):
                acc_refs[i][0:rows - bsz * d, :] += dot(i, ubs[i][bsz * d:rows], N_TBLK - 1 + d)
        return [acc_refs[i][...] for i in range(n)]

    def process_pair(c0, strips):
        cs = (c0, c0 + 1)
        hv = [load_rows(v_ref, c) for c in cs]
        conv = long_convs(hv, [strips[i][0] for i in range(2)])
        z = [load_rows(x1_ref, c) * conv[i] for i, c in enumerate(cs)]
        conv = long_convs(z, [strips[i][1] for i in range(2)])
        for i, c in enumerate(cs):
            y = load_rows(x2_ref, c) * conv[i]
            o_ref[pl.ds(c, rows, stride=SUBLANES), :] = y[:, :LANES]
            o_ref[pl.ds(half_rows + c, rows, stride=SUBLANES), :] = y[:, LANES:]

    @pl.when(ct == 0)
    def _():
        build_set(0, 0, (g0_ref, g1_ref))

    n_cp = SUBLANES // 2
    for p in range(n_cp):
        if p + 1 < n_cp:
            build_set((p + 1) % 2, 2 * (p + 1), (g0_ref, g1_ref))
        else:
            build_set((p + 1) % 2, 0, (g0n_ref, g1n_ref))
        process_pair(2 * p, sets[p % 2])


def _hyena(hy, gm, bsz):
    n_tiles = gm.shape[0] // 2
    rows = N_TBLK * bsz
    prow = 2 * rows * SUBLANES
    hy = hy.reshape(hy.shape[0], prow, LANES)
    gspec = lambda off: pl.BlockSpec((None, N_WIN * SUBLANES, LANES), lambda i: (i + off, 0, 0))
    gnext = lambda off: pl.BlockSpec((None, N_WIN * SUBLANES, LANES),
                                     lambda i: (jnp.minimum(i + 1, n_tiles - 1) + off, 0, 0))
    uspec = lambda off: pl.BlockSpec((None, prow, LANES), lambda i: (i + off, 0, 0))
    strip = pltpu.VMEM((STRIP_ROWS, LANES), BF16)
    return pl.pallas_call(
        functools.partial(_hyena_kernel, bsz),
        grid=(n_tiles,),
        in_specs=[uspec(0), uspec(n_tiles), uspec(2 * n_tiles), gspec(0), gspec(n_tiles),
                  gnext(0), gnext(n_tiles)],
        out_specs=pl.BlockSpec((None, prow, LANES), lambda i: (i, 0, 0)),
        out_shape=jax.ShapeDtypeStruct((n_tiles, prow, LANES), F32),
        scratch_shapes=[strip] * 8 + [pltpu.VMEM((rows, TBLK), F32)] * 2,
        compiler_params=pltpu.CompilerParams(dimension_semantics=("arbitrary",),
                                             vmem_limit_bytes=VMEM_LIMIT),
        name="hyena",
    )(hy, hy, hy, gm, gm, gm, gm)


ATT_QB = WINDOW
ATT_SPAN = 3 * WINDOW
ATT_TILE = 1024


def _attn_out_kernel(x_ref, yh_ref, gh_ref, q_ref, k_ref, v_ref, ga_ref, sink_ref, hg_ref, ag_ref,
                     woh_ref, woa_ref, o_ref, bias_ref, ya_ref):
    qb, span = ATT_QB, ATT_SPAN
    t = pl.program_id(1)

    @pl.when(jnp.logical_and(pl.program_id(0) == 0, t == 0))
    def _():
        r = lax.broadcasted_iota(jnp.int32, (qb, span), 0)
        c = lax.broadcasted_iota(jnp.int32, (qb, span), 1)
        for j in range(3):
            bias_ref[j] = jnp.where(jnp.abs(c - r - j * WINDOW) <= WINDOW, 0.0, -jnp.inf)

    lane = lax.broadcasted_iota(jnp.int32, (2 * qb, LANES), 1)
    low = lane < HEAD_DIM
    row2 = lax.broadcasted_iota(jnp.int32, (2 * qb, 1), 0)
    sinks = [[jnp.where(row2 < qb, sink_ref[4 * kvh + par], sink_ref[4 * kvh + 2 + par])
              for par in range(2)] for kvh in range(N_KV_HEADS)]

    for i in range(ATT_TILE // qb):
        r0 = i * qb
        q0 = pl.multiple_of(t * ATT_TILE + r0, qb)
        ws = pl.multiple_of(jnp.clip(q0 - WINDOW, 0, SEQ - span), WINDOW)
        bias1 = bias_ref[(q0 - ws) // WINDOW]
        bias = jnp.concatenate([bias1, bias1], axis=0)
        scores = []
        for kvh in range(N_KV_HEADS):
            qs = jnp.concatenate(
                [q_ref[0, r0:r0 + qb, (2 * kvh) * LANES:(2 * kvh + 1) * LANES],
                 q_ref[0, r0:r0 + qb, (2 * kvh + 1) * LANES:(2 * kvh + 2) * LANES]], axis=0)
            zero = jnp.zeros_like(qs)
            for par in range(2):
                qm = jnp.where(low, qs, zero) if par == 0 else jnp.where(low, zero, qs)
                ks = (kvh + par) % 2
                km = k_ref[0, pl.ds(ws, span), ks * LANES:(ks + 1) * LANES]
                scores.append(_nt_dot(qm, km) + bias)
        for kvh in range(N_KV_HEADS):
            parts = []
            for par in range(2):
                s = scores[2 * kvh + par]
                vs = 2 * kvh + par
                vm = v_ref[0, pl.ds(ws, span), vs * LANES:(vs + 1) * LANES]
                sink = sinks[kvh][par]
                m = jnp.maximum(jnp.max(s, axis=-1, keepdims=True), sink)
                p = jnp.exp((s - m).astype(BF16))
                parts.append((jnp.dot(p, vm, preferred_element_type=F32), jnp.exp(sink - m)))
            (o_e, sink_e), (o_o, sink_o) = parts
            num = jnp.where(low, o_e, o_o)
            den = jnp.where(low, pltpu.roll(o_e, HEAD_DIM, 1) + sink_e,
                            pltpu.roll(o_o, HEAD_DIM, 1) + sink_o)
            o = num / den
            ya_ref[r0:r0 + qb, (2 * kvh) * LANES:(2 * kvh + 1) * LANES] = o[:qb]
            ya_ref[r0:r0 + qb, (2 * kvh + 1) * LANES:(2 * kvh + 2) * LANES] = o[qb:]

    def chan_major(ref):
        parts = []
        for kk in range(ATT_TILE // LANES):
            blk, half = kk // 2, kk % 2
            parts.append(ref[:, half, blk, 0, :, :].reshape(D_HYENA, LANES))
        return jnp.concatenate(parts, axis=1)

    yh = chan_major(yh_ref)
    gh = chan_major(gh_ref)
    yh_n = yh * lax.rsqrt(jnp.mean(yh * yh, axis=0, keepdims=True) + EPS) * hg_ref[...]
    yh_g = (yh_n * (gh * jax.nn.sigmoid(gh))).astype(BF16)
    acc = lax.dot_general(yh_g, woh_ref[...], (((0,), (0,)), ((), ())), preferred_element_type=F32)

    ya = ya_ref[...]
    ga = ga_ref[0].astype(F32)
    ya_n = ya * lax.rsqrt(jnp.mean(ya * ya, axis=-1, keepdims=True) + EPS) * ag_ref[...]
    ya_g = (ya_n * (ga * jax.nn.sigmoid(ga))).astype(BF16)
    acc = acc + jnp.dot(ya_g, woa_ref[...], preferred_element_type=F32)
    o_ref[0] = x_ref[0] + acc


def _attn_out(x, yh, hy, q, k2, v4, ga, sink, hy_out_norm_g, attn_out_norm_g, w_out):
    bsz = x.shape[0]
    n_tt = SEQ // ATT_TILE
    yh = yh.reshape(HY_TILES, 2, N_TBLK, bsz, SUBLANES, LANES)
    woh = w_out[:D_HYENA].astype(BF16)
    woa = w_out[D_HYENA:].astype(BF16)
    const = lambda shape: pl.BlockSpec(shape, lambda b, t: (0,) * len(shape))
    tok = lambda width: pl.BlockSpec((1, ATT_TILE, width), lambda b, t: (b, t, 0))
    seq = lambda width: pl.BlockSpec((1, SEQ, width), lambda b, t: (b, 0, 0))
    packed = lambda tile_blk: pl.BlockSpec(
        (HY_TILES, 2, ATT_TILE // TBLK, 1, SUBLANES, LANES), lambda b, t: (tile_blk, 0, t, b, 0, 0))
    return pl.pallas_call(
        _attn_out_kernel,
        grid=(bsz, n_tt),
        in_specs=[
            tok(D_MODEL), packed(0), packed(3), tok(D_ATTN), seq(2 * D_KV), seq(4 * D_KV), tok(D_ATTN),
            pl.BlockSpec(memory_space=pltpu.SMEM),
            const((D_HYENA, 1)), const((1, D_ATTN)), const(woh.shape), const(woa.shape),
        ],
        out_specs=tok(D_MODEL),
        out_shape=jax.ShapeDtypeStruct(x.shape, x.dtype),
        scratch_shapes=[pltpu.VMEM((3, ATT_QB, ATT_SPAN), F32), pltpu.VMEM((ATT_TILE, D_ATTN), F32)],
        compiler_params=pltpu.CompilerParams(dimension_semantics=("arbitrary", "arbitrary"),
                                             vmem_limit_bytes=VMEM_LIMIT),
        name="attn_out",
    )(x, yh, hy, q, k2, v4, ga, sink, hy_out_norm_g[:, None], attn_out_norm_g[None], woh, woa)


def kernel(x, norm_g, w_in, conv_w, conv_b, filt_w1, filt_b1, filt_w2, filt_b2, filt_w3, filt_b3,
           filt_w4, filt_sin_freq, hyena_bias, q_norm_g, k_norm_g, attn_sink, hy_out_norm_g,
           attn_out_norm_g, w_out):
    bsz, seq, d_model = x.shape
    assert seq == SEQ and d_model == D_MODEL and bsz % SUBLANES == 0
    assert norm_g.shape[0] == 1, "one layer"
    gm = _filter_windows(filt_w1[0], filt_b1[0], filt_w2[0], filt_b2[0], filt_w3[0], filt_b3[0],
                         filt_w4[0], filt_sin_freq[0], hyena_bias[0])
    hy, q, k2, v2, ga = _in_proj(x, norm_g[0], w_in[0], conv_w[0], conv_b[0], q_norm_g[0], k_norm_g[0])
    yh = _hyena(hy, gm, bsz)
    return _attn_out(x, yh, hy, q, k2, v2, ga, attn_sink[0], hy_out_norm_g[0], attn_out_norm_g[0],
                     w_out[0])
```

```python
import functools
import math

import jax
import jax.numpy as jnp
from jax import lax
from jax.experimental import pallas as pl
from jax.experimental.pallas import tpu as pltpu

F32 = jnp.float32
BF16 = jnp.bfloat16
HI = lax.Precision.HIGHEST

D_MODEL = 1024
SEQ = 2048
D_HYENA = 512
D_ATTN = 512
HEAD_DIM = 64
N_KV_HEADS = 2
WINDOW = 128
ROPE_THETA = 10000.0
FILTER_HIDDEN = 64
N_BANDS = 16
DECAY_TARGET = 1e-2
FAST_DECAY_PCT = 0.3
SLOW_DECAY_PCT = 1.5
EPS = 1e-6

LANES = 128
SUBLANES = 8
PACKED_ROWS = 16
TBLK = 256
N_TBLK = SEQ // TBLK
N_SHIFT = 2 * N_TBLK - 1
N_HY_T = 4 * D_HYENA
N_HY_TILES = N_HY_T // SUBLANES
HY_TILES = D_HYENA // SUBLANES
D_KV = HEAD_DIM * N_KV_HEADS

FILT_LPAD = LANES
N_WINQ = (FILT_LPAD + 2 * SEQ) // LANES
N_POS = (N_WINQ + 1) * LANES
WIN_PHASES = (0, SUBLANES)
N_WIN = len(WIN_PHASES) * N_WINQ
FILT_CH = 128
TOK_TILE = 512
VMEM_LIMIT = 56 * 1024 * 1024


def _nt_dot(a, b, precision=None):
    return lax.dot_general(a, b, (((1,), (1,)), ((), ())), preferred_element_type=F32,
                           precision=precision)


def _filter_kernel(w1t_ref, w1c_ref, w1s_ref, b1_ref, w2_ref, b2_ref, w3_ref, b3_ref,
                   w4_ref, fr_ref, hb_ref, o_ref, hid_ref):
    centre = FILT_LPAD + SEQ
    first = jnp.logical_and(pl.program_id(0) == 0, pl.program_id(1) == 0)

    @pl.when(first)
    def _():
        off_c = lax.broadcasted_iota(jnp.int32, (N_POS, 1), 0) - centre
        idx_c = jnp.minimum(jnp.abs(off_c), SEQ - 1).astype(F32)
        t_c = idx_c / (SEQ - 1)
        w_c = (2.0 * math.pi) * idx_c / SEQ
        band = lax.broadcasted_iota(jnp.int32, (1, N_BANDS), 1).astype(F32)
        freqs = 1e-4 + band * ((N_BANDS - 1 - 1e-4) / (N_BANDS - 1))
        ang = w_c * freqs
        fr = fr_ref[...]
        pre = (t_c * w1t_ref[...]
               + jnp.dot(jnp.cos(ang), w1c_ref[...], preferred_element_type=F32, precision=HI)
               - jnp.dot(jnp.sin(ang), w1s_ref[...], preferred_element_type=F32, precision=HI)
               + b1_ref[...])
        h = jnp.sin(fr * pre)
        h = jnp.sin(fr * (jnp.dot(h, w2_ref[...], preferred_element_type=F32, precision=HI) + b2_ref[...]))
        h = jnp.sin(fr * (jnp.dot(h, w3_ref[...], preferred_element_type=F32, precision=HI) + b3_ref[...]))
        hid_ref[...] = h

    hid = hid_ref[...]
    fwd = _nt_dot(w4_ref[0, 0], hid, precision=HI)
    bwd = _nt_dot(w4_ref[0, 1], hid, precision=HI)
    off_r = lax.broadcasted_iota(jnp.int32, (1, N_POS), 1) - centre
    idx_r = jnp.minimum(jnp.abs(off_r), SEQ - 1).astype(F32)
    max_decay = math.log(DECAY_TARGET) / FAST_DECAY_PCT
    min_decay = math.log(DECAY_TARGET) / SLOW_DECAY_PCT
    chan = (pl.program_id(1) * FILT_CH
            + lax.broadcasted_iota(jnp.int32, (FILT_CH, 1), 0)).astype(F32)
    deltas = min_decay + chan * ((max_decay - min_decay) / (D_HYENA - 1))
    decay = jnp.exp(-(idx_r / (SEQ - 1)) * jnp.abs(deltas))
    decay = jnp.where(jnp.abs(off_r) <= SEQ - 1, decay, 0.0)
    val = jnp.where(off_r < 0, bwd, fwd) * decay
    val = val + jnp.where(off_r == 0, hb_ref[0], 0.0)
    for p, phase in enumerate(WIN_PHASES):
        shifted = val if phase == 0 else pltpu.roll(val, phase, 1)
        for w in range(N_WINQ):
            o_ref[:, p * N_WINQ + w, :, :] = (
                shifted[:, w * LANES:(w + 1) * LANES].reshape(FILT_CH // SUBLANES, SUBLANES, LANES))


def _filter_windows(w1, b1, w2, b2, w3, b3, w4, sin_freq, hyena_bias):
    full = lambda shape: pl.BlockSpec(shape, lambda o, j: (0,) * len(shape))
    w4r = w4.T.reshape(2, 2, D_HYENA, FILTER_HIDDEN)
    args = (w1[0:1], w1[1:1 + N_BANDS], w1[1 + N_BANDS:], b1[None], w2, b2[None], w3, b3[None])
    n_j = D_HYENA // FILT_CH
    tiles = FILT_CH // SUBLANES
    out = pl.pallas_call(
        _filter_kernel,
        grid=(2, n_j),
        in_specs=[full(a.shape) for a in args] + [
            pl.BlockSpec((1, 2, FILT_CH, FILTER_HIDDEN), lambda o, j: (o, 0, j, 0)),
            full((1, FILTER_HIDDEN)),
            pl.BlockSpec((1, FILT_CH, 1), lambda o, j: (o, j, 0)),
        ],
        out_specs=pl.BlockSpec((tiles, N_WIN, SUBLANES, LANES), lambda o, j: (o * n_j + j, 0, 0, 0)),
        out_shape=jax.ShapeDtypeStruct((2 * HY_TILES, N_WIN, SUBLANES, LANES), F32),
        scratch_shapes=[pltpu.VMEM((N_POS, FILTER_HIDDEN), F32)],
        compiler_params=pltpu.CompilerParams(dimension_semantics=("arbitrary", "arbitrary"),
                                             vmem_limit_bytes=VMEM_LIMIT),
        name="filter_windows",
    )(*args, w4r, sin_freq[None], hyena_bias[:, :, None])
    return out.reshape(2 * HY_TILES, N_WIN * SUBLANES, LANES)


N_STREAM = 4
SEG_W = 256
CONV_SUB = 32


def _in_proj_kernel(x_ref, g_ref, wht_ref, cw_ref, wa_ref, seg_ref, cos_ref, sin_ref, qg_ref, kg_ref,
                    hy_ref, q_ref, k_ref, v_ref, ga_ref, h0_ref, h1_ref, r0_ref, r1_ref):
    bb = pl.program_id(0)
    j = pl.program_id(1)
    n_tt = SEQ // TOK_TILE
    n_chunk = SEQ // LANES
    lane = lax.broadcasted_iota(jnp.int32, (TOK_TILE, LANES), 1)
    lane_c = lax.broadcasted_iota(jnp.int32, (CONV_SUB, LANES), 1)

    def normalise(h_cur):
        x = x_ref[0]
        ms = jnp.mean(x * x, axis=-1, keepdims=True)
        t0 = pl.multiple_of(j * TOK_TILE, TOK_TILE)
        h_cur[pl.ds(t0, TOK_TILE), :] = (x * lax.rsqrt(ms + EPS) * g_ref[...]).astype(BF16)

    def project(h_prev, dst_ref):
        for t in range(n_tt):
            dst_ref[:, t * TOK_TILE:(t + 1) * TOK_TILE] = _nt_dot(
                wht_ref[...], h_prev[t * TOK_TILE:(t + 1) * TOK_TILE, :])

    def emit(src_ref, conv):
        for sub in range(D_HYENA // CONV_SUB):
            rows = slice(sub * CONV_SUB, (sub + 1) * CONV_SUB)
            tiles = slice(sub * CONV_SUB // SUBLANES, (sub + 1) * CONV_SUB // SUBLANES)
            if conv:
                w0, w1, w2, wb = (cw_ref[0, i, rows, :] for i in range(4))
                chunk = lambda k: src_ref[rows, k * LANES:(k + 1) * LANES]
                prev_r = None
                cur = chunk(0)
                cur_r, cur_l = pltpu.roll(cur, 1, 1), pltpu.roll(cur, LANES - 1, 1)
            for k in range(n_chunk):
                if conv:
                    if k + 1 < n_chunk:
                        nxt = chunk(k + 1)
                        nxt_r, nxt_l = pltpu.roll(nxt, 1, 1), pltpu.roll(nxt, LANES - 1, 1)
                    um = jnp.where(lane_c == 0, 0.0 if prev_r is None else prev_r, cur_r)
                    up = jnp.where(lane_c == LANES - 1, nxt_l if k + 1 < n_chunk else 0.0, cur_l)
                    out = w0 * um + w1 * cur + w2 * up + wb
                    prev_r = cur_r
                    if k + 1 < n_chunk:
                        cur, cur_r, cur_l = nxt, nxt_r, nxt_l
                else:
                    out = src_ref[rows, k * LANES:(k + 1) * LANES]
                blk, half = k // 2, k % 2
                hy_ref[tiles, half, blk, 0, :, :] = out.reshape(CONV_SUB // SUBLANES, SUBLANES, LANES)

    def attention_cols(h_prev, t):
        att = jnp.dot(h_prev[t * TOK_TILE:(t + 1) * TOK_TILE, :], wa_ref[...],
                      preferred_element_type=F32)
        cos = cos_ref[...]
        sin = sin_ref[...]
        first_half = (lane % HEAD_DIM) < (HEAD_DIM // 2)

        def norm_rope(t, gain):
            sq = (t * t).astype(BF16)
            w = min(t.shape[1], SEG_W)
            seg = seg_ref[:w, :w]
            slabs = [jnp.dot(sq[:, i:i + w], seg, preferred_element_type=F32)
                     for i in range(0, t.shape[1], w)]
            ms_h = slabs[0] if len(slabs) == 1 else jnp.concatenate(slabs, axis=1)
            tn = t * lax.rsqrt(ms_h + EPS) * gain
            outs = []
            for i in range(t.shape[1] // LANES):
                c = tn[:, i * LANES:(i + 1) * LANES]
                swapped = jnp.where(first_half, pltpu.roll(c, LANES - HEAD_DIM // 2, 1),
                                    pltpu.roll(c, HEAD_DIM // 2, 1))
                outs.append(c * cos + swapped * sin)
            return outs[0] if len(outs) == 1 else jnp.concatenate(outs, axis=1)

        q = norm_rope(att[:, :D_ATTN], qg_ref[...])
        k = norm_rope(att[:, D_ATTN:D_ATTN + D_KV], kg_ref[...])
        v = att[:, D_ATTN + D_KV:D_ATTN + 2 * D_KV]
        q_ref[0] = (q * (HEAD_DIM ** -0.5)).astype(BF16)
        k_ref[0] = jnp.concatenate([k, pltpu.roll(k, HEAD_DIM, 1)], axis=1).astype(BF16)
        v_sw = pltpu.roll(v, HEAD_DIM, 1)
        low = lane < HEAD_DIM
        v_ref[0] = jnp.concatenate([jnp.where(low, v, 1.0), jnp.where(low, 1.0, v_sw),
                                    jnp.where(low, v_sw, 1.0), jnp.where(low, 1.0, v)], axis=1).astype(BF16)
        ga_ref[0] = att[:, D_ATTN + 2 * D_KV:].astype(BF16)

    def steps(h_cur, h_prev):
        @pl.when(jnp.logical_and(bb == 0, j < N_STREAM))
        def _():
            normalise(h_cur)

        @pl.when(jnp.logical_and(bb > 0, j == 0))
        def _():
            normalise(h_cur)
            project(h_prev, r0_ref)

        for step in range(1, N_STREAM):
            @pl.when(jnp.logical_and(bb > 0, j == step))
            def _(step=step):
                attention_cols(h_prev, step - 1)
                normalise(h_cur)
                project(h_prev, r1_ref if step % 2 else r0_ref)
                emit(r0_ref if step % 2 else r1_ref, True)

        @pl.when(jnp.logical_and(bb > 0, j == N_STREAM))
        def _():
            attention_cols(h_prev, N_STREAM - 1)
            emit(r1_ref, False)

    @pl.when(bb % 2 == 0)
    def _():
        steps(h0_ref, h1_ref)

    @pl.when(bb % 2 == 1)
    def _():
        steps(h1_ref, h0_ref)


def _in_proj(x, norm_g, w_in, conv_w, conv_b, q_norm_g, k_norm_g):
    bsz = x.shape[0]
    n_tt = SEQ // TOK_TILE
    assert n_tt == N_STREAM
    o1 = N_STREAM * D_HYENA
    wht = w_in[:, :o1].T.astype(BF16)
    wa = w_in[:, o1:].astype(BF16)
    cw = jnp.concatenate([conv_w.reshape(3, 3, D_HYENA).transpose(1, 0, 2),
                          conv_b.reshape(3, 1, D_HYENA)], axis=1)
    cw = jnp.broadcast_to(cw[..., None], (3, 4, D_HYENA, LANES))
    head = jnp.arange(SEG_W) // HEAD_DIM
    seg = jnp.where(head[:, None] == head[None, :], 1.0 / HEAD_DIM, 0.0).astype(BF16)
    half = HEAD_DIM // 2
    inv = ROPE_THETA ** (-jnp.arange(half, dtype=F32) / half)
    ang = jnp.arange(SEQ, dtype=F32)[:, None] * inv[None, :]
    cos = jnp.tile(jnp.cos(ang), (1, LANES // half))
    sin = jnp.tile(jnp.concatenate([-jnp.sin(ang), jnp.sin(ang)], axis=1), (1, LANES // HEAD_DIM))
    qg = jnp.tile(q_norm_g, D_ATTN // HEAD_DIM)[None]
    kg = jnp.tile(k_norm_g, D_KV // HEAD_DIM)[None]
    const = lambda shape: pl.BlockSpec(shape, lambda b, j: (0,) * len(shape))
    last = N_STREAM - 1
    out_b = lambda bb: jnp.maximum(bb - 1, 0)
    att_t = lambda bb, j: jnp.where(bb == 0, 0, jnp.clip(j - 1, 0, last))
    tok = lambda width: pl.BlockSpec((1, TOK_TILE, width), lambda bb, j: (out_b(bb), att_t(bb, j), 0))
    return pl.pallas_call(
        _in_proj_kernel,
        grid=(bsz + 1, N_STREAM + 1),
        in_specs=[
            pl.BlockSpec((1, TOK_TILE, D_MODEL),
                         lambda bb, j: (jnp.minimum(bb, bsz - 1), jnp.minimum(j, last), 0)),
            const((1, D_MODEL)),
            pl.BlockSpec((D_HYENA, D_MODEL), lambda b, j: (jnp.minimum(j, last), 0)),
            pl.BlockSpec((1, 4, D_HYENA, LANES), lambda b, j: (jnp.clip(j - 1, 0, 2), 0, 0, 0)),
            const(wa.shape),
            const(seg.shape),
            pl.BlockSpec((TOK_TILE, LANES), lambda bb, j: (att_t(bb, j), 0)),
            pl.BlockSpec((TOK_TILE, LANES), lambda bb, j: (att_t(bb, j), 0)),
            const((1, D_ATTN)),
            const((1, D_KV)),
        ],
        out_specs=[
            pl.BlockSpec((HY_TILES, 2, N_TBLK, 1, SUBLANES, LANES),
                         lambda bb, j: (jnp.where(bb == 0, 0, jnp.maximum(j - 1, 0)), 0, 0, out_b(bb), 0, 0)),
            tok(D_ATTN), tok(2 * D_KV), tok(4 * D_KV), tok(D_ATTN),
        ],
        out_shape=[
            jax.ShapeDtypeStruct((N_HY_TILES, 2, N_TBLK, bsz, SUBLANES, LANES), F32),
            jax.ShapeDtypeStruct((bsz, SEQ, D_ATTN), BF16),
            jax.ShapeDtypeStruct((bsz, SEQ, 2 * D_KV), BF16),
            jax.ShapeDtypeStruct((bsz, SEQ, 4 * D_KV), BF16),
            jax.ShapeDtypeStruct((bsz, SEQ, D_ATTN), BF16),
        ],
        scratch_shapes=[pltpu.VMEM((SEQ, D_MODEL), BF16), pltpu.VMEM((SEQ, D_MODEL), BF16),
                        pltpu.VMEM((D_HYENA, SEQ), F32), pltpu.VMEM((D_HYENA, SEQ), F32)],
        compiler_params=pltpu.CompilerParams(dimension_semantics=("arbitrary", "arbitrary"),
                                             vmem_limit_bytes=VMEM_LIMIT),
        name="in_proj",
    )(x, norm_g[None], wht, cw, wa, seg, cos, sin, qg, kg)


STRIP_ROWS = N_SHIFT * TBLK + LANES
HY_TILES_PER_STEP = 2


def _hyena_kernel(bsz, v_ref, x1_ref, x2_ref, g0_ref, g1_ref, g0n_ref, g1n_ref,
                  o_ref, *strip_refs):
    ct = pl.program_id(0)
    rows = N_TBLK * bsz
    half_rows = rows * SUBLANES
    pk = PACKED_ROWS
    n_pairs = STRIP_ROWS // pk
    sets = [[strip_refs[0:2], strip_refs[2:4]], [strip_refs[4:6], strip_refs[6:8]]]
    acc_refs = strip_refs[8:10]

    row_s = lax.broadcasted_iota(jnp.int32, (pk, LANES), 0) % SUBLANES
    lane_s = lax.broadcasted_iota(jnp.int32, (pk, LANES), 1)
    n_rot = LANES // pk
    inside = [lane_s >= row_s] + [lane_s - row_s + pk * r < LANES for r in range(1, n_rot)]

    def build_strip(g_ref, c, strip_ref):
        cache = {}

        def window_pair(q):
            if ("w", q) not in cache:
                rows = [jnp.broadcast_to(g_ref[pl.ds((p * N_WINQ + q) * SUBLANES + c, 1), :],
                                         (SUBLANES, LANES)) for p in range(len(WIN_PHASES))]
                cache["w", q] = rows
            return cache["w", q]

        def rotated(q, r):
            if (q, r) not in cache:
                tiles = [pltpu.roll(w, (LANES - pk * r) % LANES, 1, stride=1, stride_axis=0)
                         for w in window_pair(q)]
                cache[q, r] = jnp.concatenate(tiles, axis=0).astype(BF16)
            return cache[q, r]

        for k in range(n_pairs, 0, -1):
            q, r = k // n_rot + 1, k % n_rot
            pair = jnp.where(inside[r], rotated(q, r), rotated(q - 1 if r == 0 else q + 1, r))
            strip_ref[pk * (n_pairs - k):pk * (n_pairs - k) + pk, :] = pair

    def weights(strip_ref, e):
        return jnp.concatenate([strip_ref[TBLK * e + LANES:TBLK * e + LANES + TBLK, :],
                                strip_ref[TBLK * e:TBLK * e + TBLK, :]], axis=1)

    def load_rows(ref, c):
        lo = ref[pl.ds(c, rows, stride=SUBLANES), :]
        hi = ref[pl.ds(half_rows + c, rows, stride=SUBLANES), :]
        return jnp.concatenate([lo, hi], axis=1)

    def build_set(s, c0, g_refs):
        for i in range(2):
            for order in range(2):
                build_strip(g_refs[order], c0 + i, sets[s][i][order])

    def long_convs(us, strips):
        ubs = [u.astype(BF16) for u in us]
        n = len(us)
        dot = lambda i, lhs, e: jnp.dot(lhs, weights(strips[i], e), preferred_element_type=F32)
        for i in range(n):
            acc_refs[i][...] = dot(i, ubs[i], N_TBLK - 1)
        for d in range(1, N_TBLK):
            for i in range(n):
                acc_refs[i][bsz * d:rows, :] += dot(i, ubs[i][0:rows - bsz * d], N_TBLK - 1 - d)
            for i in range(n):
                acc_refs[i][0:rows - bsz * d, :] += dot(i, ubs[i][bsz * d:rows], N_TBLK - 1 + d)
        return [acc_refs[i][...] for i in range(n)]

    def process_pair(tt, c0, strips):
        cs = (c0, c0 + 1)
        hv = [load_rows(v_ref.at[tt], c) for c in cs]
        conv = long_convs(hv, [strips[i][0] for i in range(2)])
        z = [load_rows(x1_ref.at[tt], c) * conv[i] for i, c in enumerate(cs)]
        conv = long_convs(z, [strips[i][1] for i in range(2)])
        for i, c in enumerate(cs):
            y = load_rows(x2_ref.at[tt], c) * conv[i]
            o_ref[tt, pl.ds(c, rows, stride=SUBLANES), :] = y[:, :LANES]
            o_ref[tt, pl.ds(half_rows + c, rows, stride=SUBLANES), :] = y[:, LANES:]

    @pl.when(ct == 0)
    def _():
        build_set(0, 0, (g0_ref.at[0], g1_ref.at[0]))

    n_cp = SUBLANES // 2
    for tt in range(HY_TILES_PER_STEP):
        for p in range(n_cp):
            pair = tt * n_cp + p
            if p + 1 < n_cp:
                build_set((pair + 1) % 2, 2 * (p + 1), (g0_ref.at[tt], g1_ref.at[tt]))
            elif tt + 1 < HY_TILES_PER_STEP:
                build_set((pair + 1) % 2, 0, (g0_ref.at[tt + 1], g1_ref.at[tt + 1]))
            else:
                build_set((pair + 1) % 2, 0, (g0n_ref, g1n_ref))
            process_pair(tt, 2 * p, sets[pair % 2])


def _hyena(hy, gm, bsz):
    n_tiles = gm.shape[0] // 2
    rows = N_TBLK * bsz
    prow = 2 * rows * SUBLANES
    hy = hy.reshape(hy.shape[0], prow, LANES)
    tps = HY_TILES_PER_STEP
    gspec = lambda off: pl.BlockSpec((tps, N_WIN * SUBLANES, LANES), lambda i: (i + off // tps, 0, 0))
    gnext = lambda off: pl.BlockSpec((None, N_WIN * SUBLANES, LANES),
                                     lambda i: (jnp.minimum(tps * (i + 1), n_tiles - 1) + off, 0, 0))
    uspec = lambda off: pl.BlockSpec((tps, prow, LANES), lambda i: (i + off // tps, 0, 0))
    strip = pltpu.VMEM((STRIP_ROWS, LANES), BF16)
    return pl.pallas_call(
        functools.partial(_hyena_kernel, bsz),
        grid=(n_tiles // tps,),
        in_specs=[uspec(0), uspec(n_tiles), uspec(2 * n_tiles), gspec(0), gspec(n_tiles),
                  gnext(0), gnext(n_tiles)],
        out_specs=pl.BlockSpec((tps, prow, LANES), lambda i: (i, 0, 0)),
        out_shape=jax.ShapeDtypeStruct((n_tiles, prow, LANES), F32),
        scratch_shapes=[strip] * 8 + [pltpu.VMEM((rows, TBLK), F32)] * 2,
        compiler_params=pltpu.CompilerParams(dimension_semantics=("arbitrary",),
                                             vmem_limit_bytes=VMEM_LIMIT),
        name="hyena",
    )(hy, hy, hy, gm, gm, gm, gm)


ATT_QB = WINDOW
ATT_SPAN = 3 * WINDOW
ATT_TILE = 1024


def _attn_out_kernel(x_ref, yh_ref, gh_ref, q_ref, k_ref, v_ref, ga_ref, sink_ref, hg_ref, ag_ref,
                     woh_ref, woa_ref, o_ref, bias_ref, ya_ref):
    qb, span = ATT_QB, ATT_SPAN
    t = pl.program_id(1)

    @pl.when(jnp.logical_and(pl.program_id(0) == 0, t == 0))
    def _():
        r = lax.broadcasted_iota(jnp.int32, (qb, span), 0)
        c = lax.broadcasted_iota(jnp.int32, (qb, span), 1)
        for j in range(3):
            bias_ref[j] = jnp.where(jnp.abs(c - r - j * WINDOW) <= WINDOW, 0.0, -jnp.inf)

    lane = lax.broadcasted_iota(jnp.int32, (2 * qb, LANES), 1)
    low = lane < HEAD_DIM
    row2 = lax.broadcasted_iota(jnp.int32, (2 * qb, 1), 0)
    sinks = [[jnp.where(row2 < qb, sink_ref[4 * kvh + par], sink_ref[4 * kvh + 2 + par])
              for par in range(2)] for kvh in range(N_KV_HEADS)]

    for i in range(ATT_TILE // qb):
        r0 = i * qb
        q0 = pl.multiple_of(t * ATT_TILE + r0, qb)
        ws = pl.multiple_of(jnp.clip(q0 - WINDOW, 0, SEQ - span), WINDOW)
        bias1 = bias_ref[(q0 - ws) // WINDOW]
        bias = jnp.concatenate([bias1, bias1], axis=0)
        scores = []
        for kvh in range(N_KV_HEADS):
            qs = jnp.concatenate(
                [q_ref[0, r0:r0 + qb, (2 * kvh) * LANES:(2 * kvh + 1) * LANES],
                 q_ref[0, r0:r0 + qb, (2 * kvh + 1) * LANES:(2 * kvh + 2) * LANES]], axis=0)
            zero = jnp.zeros_like(qs)
            for par in range(2):
                qm = jnp.where(low, qs, zero) if par == 0 else jnp.where(low, zero, qs)
                ks = (kvh + par) % 2
                km = k_ref[0, pl.ds(ws, span), ks * LANES:(ks + 1) * LANES]
                scores.append(_nt_dot(qm, km) + bias)
        for kvh in range(N_KV_HEADS):
            parts = []
            for par in range(2):
                s = scores[2 * kvh + par]
                vs = 2 * kvh + par
                vm = v_ref[0, pl.ds(ws, span), vs * LANES:(vs + 1) * LANES]
                sink = sinks[kvh][par]
                m = jnp.maximum(jnp.max(s, axis=-1, keepdims=True), sink)
                p = jnp.exp((s - m).astype(BF16))
                parts.append((jnp.dot(p, vm, preferred_element_type=F32), jnp.exp(sink - m)))
            (o_e, sink_e), (o_o, sink_o) = parts
            num = jnp.where(low, o_e, o_o)
            den = jnp.where(low, pltpu.roll(o_e, HEAD_DIM, 1) + sink_e,
                            pltpu.roll(o_o, HEAD_DIM, 1) + sink_o)
            o = num / den
            ya_ref[r0:r0 + qb, (2 * kvh) * LANES:(2 * kvh + 1) * LANES] = o[:qb]
            ya_ref[r0:r0 + qb, (2 * kvh + 1) * LANES:(2 * kvh + 2) * LANES] = o[qb:]

    def chan_major(ref):
        parts = []
        for kk in range(ATT_TILE // LANES):
            blk, half = kk // 2, kk % 2
            parts.append(ref[:, half, blk, 0, :, :].reshape(D_HYENA, LANES))
        return jnp.concatenate(parts, axis=1)

    yh = chan_major(yh_ref)
    gh = chan_major(gh_ref)
    yh_n = yh * lax.rsqrt(jnp.mean(yh * yh, axis=0, keepdims=True) + EPS) * hg_ref[...]
    yh_g = (yh_n * (gh * jax.nn.sigmoid(gh))).astype(BF16)
    acc = lax.dot_general(yh_g, woh_ref[...], (((0,), (0,)), ((), ())), preferred_element_type=F32)

    ya = ya_ref[...]
    ga = ga_ref[0].astype(F32)
    ya_n = ya * lax.rsqrt(jnp.mean(ya * ya, axis=-1, keepdims=True) + EPS) * ag_ref[...]
    ya_g = (ya_n * (ga * jax.nn.sigmoid(ga))).astype(BF16)
    acc = acc + jnp.dot(ya_g, woa_ref[...], preferred_element_type=F32)
    o_ref[0] = x_ref[0] + acc


def _attn_out(x, yh, hy, q, k2, v4, ga, sink, hy_out_norm_g, attn_out_norm_g, w_out):
    bsz = x.shape[0]
    n_tt = SEQ // ATT_TILE
    yh = yh.reshape(HY_TILES, 2, N_TBLK, bsz, SUBLANES, LANES)
    woh = w_out[:D_HYENA].astype(BF16)
    woa = w_out[D_HYENA:].astype(BF16)
    const = lambda shape: pl.BlockSpec(shape, lambda b, t: (0,) * len(shape))
    tok = lambda width: pl.BlockSpec((1, ATT_TILE, width), lambda b, t: (b, t, 0))
    seq = lambda width: pl.BlockSpec((1, SEQ, width), lambda b, t: (b, 0, 0))
    packed = lambda tile_blk: pl.BlockSpec(
        (HY_TILES, 2, ATT_TILE // TBLK, 1, SUBLANES, LANES), lambda b, t: (tile_blk, 0, t, b, 0, 0))
    return pl.pallas_call(
        _attn_out_kernel,
        grid=(bsz, n_tt),
        in_specs=[
            tok(D_MODEL), packed(0), packed(3), tok(D_ATTN), seq(2 * D_KV), seq(4 * D_KV), tok(D_ATTN),
            pl.BlockSpec(memory_space=pltpu.SMEM),
            const((D_HYENA, 1)), const((1, D_ATTN)), const(woh.shape), const(woa.shape),
        ],
        out_specs=tok(D_MODEL),
        out_shape=jax.ShapeDtypeStruct(x.shape, x.dtype),
        scratch_shapes=[pltpu.VMEM((3, ATT_QB, ATT_SPAN), F32), pltpu.VMEM((ATT_TILE, D_ATTN), F32)],
        compiler_params=pltpu.CompilerParams(dimension_semantics=("arbitrary", "arbitrary"),
                                             vmem_limit_bytes=VMEM_LIMIT),
        name="attn_out",
    )(x, yh, hy, q, k2, v4, ga, sink, hy_out_norm_g[:, None], attn_out_norm_g[None], woh, woa)


def kernel(x, norm_g, w_in, conv_w, conv_b, filt_w1, filt_b1, filt_w2, filt_b2, filt_w3, filt_b3,
           filt_w4, filt_sin_freq, hyena_bias, q_norm_g, k_norm_g, attn_sink, hy_out_norm_g,
           attn_out_norm_g, w_out):
    bsz, seq, d_model = x.shape
    assert seq == SEQ and d_model == D_MODEL and bsz % SUBLANES == 0
    assert norm_g.shape[0] == 1, "one layer"
    gm = _filter_windows(filt_w1[0], filt_b1[0], filt_w2[0], filt_b2[0], filt_w3[0], filt_b3[0],
                         filt_w4[0], filt_sin_freq[0], hyena_bias[0])
    hy, q, k2, v2, ga = _in_proj(x, norm_g[0], w_in[0], conv_w[0], conv_b[0], q_norm_g[0], k_norm_g[0])
    yh = _hyena(hy, gm, bsz)
    return _attn_out(x, yh, hy, q, k2, v2, ga, attn_sink[0], hy_out_norm_g[0], attn_out_norm_g[0],
                     w_out[0])
```

```python
import functools
import math

import jax
import jax.numpy as jnp
from jax import lax
from jax.experimental import pallas as pl
from jax.experimental.pallas import tpu as pltpu

F32 = jnp.float32
BF16 = jnp.bfloat16
HI = lax.Precision.HIGHEST

D_MODEL = 1024
SEQ = 2048
D_HYENA = 512
D_ATTN = 512
HEAD_DIM = 64
N_KV_HEADS = 2
WINDOW = 128
ROPE_THETA = 10000.0
FILTER_HIDDEN = 64
N_BANDS = 16
DECAY_TARGET = 1e-2
FAST_DECAY_PCT = 0.3
SLOW_DECAY_PCT = 1.5
EPS = 1e-6

LANES = 128
SUBLANES = 8
PACKED_ROWS = 16
TBLK = 256
N_TBLK = SEQ // TBLK
N_SHIFT = 2 * N_TBLK - 1
N_HY_T = 4 * D_HYENA
N_HY_TILES = N_HY_T // SUBLANES
HY_TILES = D_HYENA // SUBLANES
D_KV = HEAD_DIM * N_KV_HEADS

FILT_LPAD = LANES
N_WINQ = (FILT_LPAD + 2 * SEQ) // LANES
N_POS = (N_WINQ + 1) * LANES
WIN_PHASES = (0, SUBLANES)
N_WIN = len(WIN_PHASES) * N_WINQ
FILT_CH = 128
TOK_TILE = 512
VMEM_LIMIT = 56 * 1024 * 1024


def _nt_dot(a, b, precision=None):
    return lax.dot_general(a, b, (((1,), (1,)), ((), ())), preferred_element_type=F32,
                           precision=precision)


def _filter_kernel(w1t_ref, w1c_ref, w1s_ref, b1_ref, w2_ref, b2_ref, w3_ref, b3_ref,
                   w4_ref, fr_ref, hb_ref, o_ref, hid_ref):
    centre = FILT_LPAD + SEQ
    first = jnp.logical_and(pl.program_id(0) == 0, pl.program_id(1) == 0)

    @pl.when(first)
    def _():
        off_c = lax.broadcasted_iota(jnp.int32, (N_POS, 1), 0) - centre
        idx_c = jnp.minimum(jnp.abs(off_c), SEQ - 1).astype(F32)
        t_c = idx_c / (SEQ - 1)
        off_l = lax.broadcasted_iota(jnp.int32, (1, N_POS), 1) - centre
        w_l = (2.0 * math.pi) * jnp.minimum(jnp.abs(off_l), SEQ - 1).astype(F32) / SEQ
        band = lax.broadcasted_iota(jnp.int32, (N_BANDS, 1), 0).astype(F32)
        freqs = 1e-4 + band * ((N_BANDS - 1 - 1e-4) / (N_BANDS - 1))
        ang = freqs * w_l
        band_dot = lambda f, w: lax.dot_general(f, w, (((0,), (0,)), ((), ())),
                                                preferred_element_type=F32, precision=HI)
        fr = fr_ref[...]
        pre = (t_c * w1t_ref[...] + band_dot(jnp.cos(ang), w1c_ref[...])
               - band_dot(jnp.sin(ang), w1s_ref[...]) + b1_ref[...])
        h = jnp.sin(fr * pre)
        h = jnp.sin(fr * (jnp.dot(h, w2_ref[...], preferred_element_type=F32, precision=HI) + b2_ref[...]))
        h = jnp.sin(fr * (jnp.dot(h, w3_ref[...], preferred_element_type=F32, precision=HI) + b3_ref[...]))
        hid_ref[...] = h

    hid = hid_ref[...]
    fwd = _nt_dot(w4_ref[0, 0], hid, precision=HI)
    bwd = _nt_dot(w4_ref[0, 1], hid, precision=HI)
    off_r = lax.broadcasted_iota(jnp.int32, (1, N_POS), 1) - centre
    idx_r = jnp.minimum(jnp.abs(off_r), SEQ - 1).astype(F32)
    max_decay = math.log(DECAY_TARGET) / FAST_DECAY_PCT
    min_decay = math.log(DECAY_TARGET) / SLOW_DECAY_PCT
    chan = (pl.program_id(1) * FILT_CH
            + lax.broadcasted_iota(jnp.int32, (FILT_CH, 1), 0)).astype(F32)
    deltas = min_decay + chan * ((max_decay - min_decay) / (D_HYENA - 1))
    decay = jnp.exp(-(idx_r / (SEQ - 1)) * jnp.abs(deltas))
    decay = jnp.where(jnp.abs(off_r) <= SEQ - 1, decay, 0.0)
    val = jnp.where(off_r < 0, bwd, fwd) * decay
    val = val + jnp.where(off_r == 0, hb_ref[0], 0.0)
    for p, phase in enumerate(WIN_PHASES):
        shifted = val if phase == 0 else pltpu.roll(val, phase, 1)
        for w in range(N_WINQ):
            o_ref[:, p * N_WINQ + w, :, :] = (
                shifted[:, w * LANES:(w + 1) * LANES].reshape(FILT_CH // SUBLANES, SUBLANES, LANES))


def _filter_windows(w1, b1, w2, b2, w3, b3, w4, sin_freq, hyena_bias):
    full = lambda shape: pl.BlockSpec(shape, lambda o, j: (0,) * len(shape))
    w4r = w4.T.reshape(2, 2, D_HYENA, FILTER_HIDDEN)
    args = (w1[0:1], w1[1:1 + N_BANDS], w1[1 + N_BANDS:], b1[None], w2, b2[None], w3, b3[None])
    n_j = D_HYENA // FILT_CH
    tiles = FILT_CH // SUBLANES
    out = pl.pallas_call(
        _filter_kernel,
        grid=(2, n_j),
        in_specs=[full(a.shape) for a in args] + [
            pl.BlockSpec((1, 2, FILT_CH, FILTER_HIDDEN), lambda o, j: (o, 0, j, 0)),
            full((1, FILTER_HIDDEN)),
            pl.BlockSpec((1, FILT_CH, 1), lambda o, j: (o, j, 0)),
        ],
        out_specs=pl.BlockSpec((tiles, N_WIN, SUBLANES, LANES), lambda o, j: (o * n_j + j, 0, 0, 0)),
        out_shape=jax.ShapeDtypeStruct((2 * HY_TILES, N_WIN, SUBLANES, LANES), F32),
        scratch_shapes=[pltpu.VMEM((N_POS, FILTER_HIDDEN), F32)],
        compiler_params=pltpu.CompilerParams(dimension_semantics=("arbitrary", "arbitrary"),
                                             vmem_limit_bytes=VMEM_LIMIT),
        name="filter_windows",
    )(*args, w4r, sin_freq[None], hyena_bias[:, :, None])
    return out.reshape(2 * HY_TILES, N_WIN * SUBLANES, LANES)


N_STREAM = 4
SEG_W = 256
CONV_SUB = 32


def _in_proj_kernel(x_ref, g_ref, wht_ref, cw_ref, wa_ref, seg_ref, cos_ref, sin_ref, qg_ref, kg_ref,
                    hy_ref, q_ref, k_ref, v_ref, ga_ref, h0_ref, h1_ref, r0_ref, r1_ref):
    bb = pl.program_id(0)
    j = pl.program_id(1)
    n_tt = SEQ // TOK_TILE
    n_chunk = SEQ // LANES
    lane = lax.broadcasted_iota(jnp.int32, (TOK_TILE, LANES), 1)
    lane_c = lax.broadcasted_iota(jnp.int32, (CONV_SUB, LANES), 1)

    def normalise(h_cur):
        x = x_ref[0]
        ms = jnp.mean(x * x, axis=-1, keepdims=True)
        t0 = pl.multiple_of(j * TOK_TILE, TOK_TILE)
        h_cur[pl.ds(t0, TOK_TILE), :] = (x * lax.rsqrt(ms + EPS) * g_ref[...]).astype(BF16)

    def project(h_prev, dst_ref):
        for t in range(n_tt):
            dst_ref[:, t * TOK_TILE:(t + 1) * TOK_TILE] = _nt_dot(
                wht_ref[...], h_prev[t * TOK_TILE:(t + 1) * TOK_TILE, :])

    def emit(src_ref, conv):
        for sub in range(D_HYENA // CONV_SUB):
            rows = slice(sub * CONV_SUB, (sub + 1) * CONV_SUB)
            tiles = slice(sub * CONV_SUB // SUBLANES, (sub + 1) * CONV_SUB // SUBLANES)
            if conv:
                w0, w1, w2, wb = (cw_ref[0, i, rows, :] for i in range(4))
                chunk = lambda k: src_ref[rows, k * LANES:(k + 1) * LANES]
                prev_r = None
                cur = chunk(0)
                cur_r, cur_l = pltpu.roll(cur, 1, 1), pltpu.roll(cur, LANES - 1, 1)
            for k in range(n_chunk):
                if conv:
                    if k + 1 < n_chunk:
                        nxt = chunk(k + 1)
                        nxt_r, nxt_l = pltpu.roll(nxt, 1, 1), pltpu.roll(nxt, LANES - 1, 1)
                    um = jnp.where(lane_c == 0, 0.0 if prev_r is None else prev_r, cur_r)
                    up = jnp.where(lane_c == LANES - 1, nxt_l if k + 1 < n_chunk else 0.0, cur_l)
                    out = w0 * um + w1 * cur + w2 * up + wb
                    prev_r = cur_r
                    if k + 1 < n_chunk:
                        cur, cur_r, cur_l = nxt, nxt_r, nxt_l
                else:
                    out = src_ref[rows, k * LANES:(k + 1) * LANES]
                blk, half = k // 2, k % 2
                hy_ref[tiles, half, blk, 0, :, :] = out.reshape(CONV_SUB // SUBLANES, SUBLANES, LANES)

    def attention_cols(h_prev, t):
        att = jnp.dot(h_prev[t * TOK_TILE:(t + 1) * TOK_TILE, :], wa_ref[...],
                      preferred_element_type=F32)
        cos = cos_ref[...]
        sin = sin_ref[...]
        first_half = (lane % HEAD_DIM) < (HEAD_DIM // 2)

        def norm_rope(t, gain):
            sq = (t * t).astype(BF16)
            w = min(t.shape[1], SEG_W)
            seg = seg_ref[:w, :w]
            slabs = [jnp.dot(sq[:, i:i + w], seg, preferred_element_type=F32)
                     for i in range(0, t.shape[1], w)]
            ms_h = slabs[0] if len(slabs) == 1 else jnp.concatenate(slabs, axis=1)
            tn = t * lax.rsqrt(ms_h + EPS) * gain
            outs = []
            for i in range(t.shape[1] // LANES):
                c = tn[:, i * LANES:(i + 1) * LANES]
                swapped = jnp.where(first_half, pltpu.roll(c, LANES - HEAD_DIM // 2, 1),
                                    pltpu.roll(c, HEAD_DIM // 2, 1))
                outs.append(c * cos + swapped * sin)
            return outs[0] if len(outs) == 1 else jnp.concatenate(outs, axis=1)

        q = norm_rope(att[:, :D_ATTN], qg_ref[...])
        k = norm_rope(att[:, D_ATTN:D_ATTN + D_KV], kg_ref[...])
        v = att[:, D_ATTN + D_KV:D_ATTN + 2 * D_KV]
        q_ref[0] = (q * (HEAD_DIM ** -0.5)).astype(BF16)
        k_ref[0] = jnp.concatenate([k, pltpu.roll(k, HEAD_DIM, 1)], axis=1).astype(BF16)
        v_sw = pltpu.roll(v, HEAD_DIM, 1)
        low = lane < HEAD_DIM
        v_ref[0] = jnp.concatenate([jnp.where(low, v, 1.0), jnp.where(low, 1.0, v_sw),
                                    jnp.where(low, v_sw, 1.0), jnp.where(low, 1.0, v)], axis=1).astype(BF16)
        ga_ref[0] = att[:, D_ATTN + 2 * D_KV:].astype(BF16)

    def steps(h_cur, h_prev):
        @pl.when(jnp.logical_and(bb == 0, j < N_STREAM))
        def _():
            normalise(h_cur)

        @pl.when(jnp.logical_and(bb > 0, j == 0))
        def _():
            normalise(h_cur)
            project(h_prev, r0_ref)

        for step in range(1, N_STREAM):
            @pl.when(jnp.logical_and(bb > 0, j == step))
            def _(step=step):
                attention_cols(h_prev, step - 1)
                normalise(h_cur)
                project(h_prev, r1_ref if step % 2 else r0_ref)
                emit(r0_ref if step % 2 else r1_ref, True)

        @pl.when(jnp.logical_and(bb > 0, j == N_STREAM))
        def _():
            attention_cols(h_prev, N_STREAM - 1)
            emit(r1_ref, False)

    @pl.when(bb % 2 == 0)
    def _():
        steps(h0_ref, h1_ref)

    @pl.when(bb % 2 == 1)
    def _():
        steps(h1_ref, h0_ref)


def _in_proj(x, norm_g, w_in, conv_w, conv_b, q_norm_g, k_norm_g):
    bsz = x.shape[0]
    n_tt = SEQ // TOK_TILE
    assert n_tt == N_STREAM
    o1 = N_STREAM * D_HYENA
    wht = w_in[:, :o1].T.astype(BF16)
    wa = w_in[:, o1:].astype(BF16)
    cw = jnp.concatenate([conv_w.reshape(3, 3, D_HYENA).transpose(1, 0, 2),
                          conv_b.reshape(3, 1, D_HYENA)], axis=1)
    cw = jnp.broadcast_to(cw[..., None], (3, 4, D_HYENA, LANES))
    head = jnp.arange(SEG_W) // HEAD_DIM
    seg = jnp.where(head[:, None] == head[None, :], 1.0 / HEAD_DIM, 0.0).astype(BF16)
    half = HEAD_DIM // 2
    inv = ROPE_THETA ** (-jnp.arange(half, dtype=F32) / half)
    ang = jnp.arange(SEQ, dtype=F32)[:, None] * inv[None, :]
    cos = jnp.tile(jnp.cos(ang), (1, LANES // half))
    sin = jnp.tile(jnp.concatenate([-jnp.sin(ang), jnp.sin(ang)], axis=1), (1, LANES // HEAD_DIM))
    qg = jnp.tile(q_norm_g, D_ATTN // HEAD_DIM)[None]
    kg = jnp.tile(k_norm_g, D_KV // HEAD_DIM)[None]
    const = lambda shape: pl.BlockSpec(shape, lambda b, j: (0,) * len(shape))
    last = N_STREAM - 1
    out_b = lambda bb: jnp.maximum(bb - 1, 0)
    att_t = lambda bb, j: jnp.where(bb == 0, 0, jnp.clip(j - 1, 0, last))
    tok = lambda width: pl.BlockSpec((1, TOK_TILE, width), lambda bb, j: (out_b(bb), att_t(bb, j), 0))
    return pl.pallas_call(
        _in_proj_kernel,
        grid=(bsz + 1, N_STREAM + 1),
        in_specs=[
            pl.BlockSpec((1, TOK_TILE, D_MODEL),
                         lambda bb, j: (jnp.minimum(bb, bsz - 1), jnp.minimum(j, last), 0)),
            const((1, D_MODEL)),
            pl.BlockSpec((D_HYENA, D_MODEL), lambda b, j: (jnp.minimum(j, last), 0)),
            pl.BlockSpec((1, 4, D_HYENA, LANES), lambda b, j: (jnp.clip(j - 1, 0, 2), 0, 0, 0)),
            const(wa.shape),
            const(seg.shape),
            pl.BlockSpec((TOK_TILE, LANES), lambda bb, j: (att_t(bb, j), 0)),
            pl.BlockSpec((TOK_TILE, LANES), lambda bb, j: (att_t(bb, j), 0)),
            const((1, D_ATTN)),
            const((1, D_KV)),
        ],
        out_specs=[
            pl.BlockSpec((HY_TILES, 2, N_TBLK, 1, SUBLANES, LANES),
                         lambda bb, j: (jnp.where(bb == 0, 0, jnp.maximum(j - 1, 0)), 0, 0, out_b(bb), 0, 0)),
            tok(D_ATTN), tok(2 * D_KV), tok(4 * D_KV), tok(D_ATTN),
        ],
        out_shape=[
            jax.ShapeDtypeStruct((N_HY_TILES, 2, N_TBLK, bsz, SUBLANES, LANES), F32),
            jax.ShapeDtypeStruct((bsz, SEQ, D_ATTN), BF16),
            jax.ShapeDtypeStruct((bsz, SEQ, 2 * D_KV), BF16),
            jax.ShapeDtypeStruct((bsz, SEQ, 4 * D_KV), BF16),
            jax.ShapeDtypeStruct((bsz, SEQ, D_ATTN), BF16),
        ],
        scratch_shapes=[pltpu.VMEM((SEQ, D_MODEL), BF16), pltpu.VMEM((SEQ, D_MODEL), BF16),
                        pltpu.VMEM((D_HYENA, SEQ), F32), pltpu.VMEM((D_HYENA, SEQ), F32)],
        compiler_params=pltpu.CompilerParams(dimension_semantics=("arbitrary", "arbitrary"),
                                             vmem_limit_bytes=VMEM_LIMIT),
        name="in_proj",
    )(x, norm_g[None], wht, cw, wa, seg, cos, sin, qg, kg)


STRIP_ROWS = N_SHIFT * TBLK + LANES
HY_TILES_PER_STEP = 2


def _hyena_kernel(bsz, v_ref, x1_ref, x2_ref, g0_ref, g1_ref, g0n_ref, g1n_ref,
                  o_ref, *strip_refs):
    ct = pl.program_id(0)
    rows = N_TBLK * bsz
    half_rows = rows * SUBLANES
    pk = PACKED_ROWS
    n_pairs = STRIP_ROWS // pk
    sets = [[strip_refs[0:2], strip_refs[2:4]], [strip_refs[4:6], strip_refs[6:8]]]
    acc_refs = strip_refs[8:10]

    row_s = lax.broadcasted_iota(jnp.int32, (pk, LANES), 0) % SUBLANES
    lane_s = lax.broadcasted_iota(jnp.int32, (pk, LANES), 1)
    n_rot = LANES // pk
    inside = [lane_s >= row_s] + [lane_s - row_s + pk * r < LANES for r in range(1, n_rot)]

    def build_strip(g_ref, c, strip_ref):
        cache = {}

        def window_pair(q):
            if ("w", q) not in cache:
                rows = [jnp.broadcast_to(g_ref[pl.ds((p * N_WINQ + q) * SUBLANES + c, 1), :],
                                         (SUBLANES, LANES)) for p in range(len(WIN_PHASES))]
                cache["w", q] = rows
            return cache["w", q]

        def rotated(q, r):
            if (q, r) not in cache:
                tiles = [pltpu.roll(w, (LANES - pk * r) % LANES, 1, stride=1, stride_axis=0)
                         for w in window_pair(q)]
                cache[q, r] = jnp.concatenate(tiles, axis=0).astype(BF16)
            return cache[q, r]

        for k in range(n_pairs, 0, -1):
            q, r = k // n_rot + 1, k % n_rot
            pair = jnp.where(inside[r], rotated(q, r), rotated(q - 1 if r == 0 else q + 1, r))
            strip_ref[pk * (n_pairs - k):pk * (n_pairs - k) + pk, :] = pair

    def weights(strip_ref, e):
        return jnp.concatenate([strip_ref[TBLK * e + LANES:TBLK * e + LANES + TBLK, :],
                                strip_ref[TBLK * e:TBLK * e + TBLK, :]], axis=1)

    def load_rows(ref, c):
        lo = ref[pl.ds(c, rows, stride=SUBLANES), :]
        hi = ref[pl.ds(half_rows + c, rows, stride=SUBLANES), :]
        return jnp.concatenate([lo, hi], axis=1)

    def build_set(s, c0, g_refs):
        for i in range(2):
            for order in range(2):
                build_strip(g_refs[order], c0 + i, sets[s][i][order])

    def long_convs(us, strips):
        ubs = [u.astype(BF16) for u in us]
        n = len(us)
        dot = lambda i, lhs, e: jnp.dot(lhs, weights(strips[i], e), preferred_element_type=F32)
        for i in range(n):
            acc_refs[i][...] = dot(i, ubs[i], N_TBLK - 1)
        for d in range(1, N_TBLK):
            for i in range(n):
                acc_refs[i][bsz * d:rows, :] += dot(i, ubs[i][0:rows - bsz * d], N_TBLK - 1 - d)
            for i in range(n):
                acc_refs[i][0:rows - bsz * d, :] += dot(i, ubs[i][bsz * d:rows], N_TBLK - 1 + d)
        return [acc_refs[i][...] for i in range(n)]

    def process_pair(tt, c0, strips):
        cs = (c0, c0 + 1)
        hv = [load_rows(v_ref.at[tt], c) for c in cs]
        conv = long_convs(hv, [strips[i][0] for i in range(2)])
        z = [load_rows(x1_ref.at[tt], c) * conv[i] for i, c in enumerate(cs)]
        conv = long_convs(z, [strips[i][1] for i in range(2)])
        for i, c in enumerate(cs):
            y = load_rows(x2_ref.at[tt], c) * conv[i]
            o_ref[tt, pl.ds(c, rows, stride=SUBLANES), :] = y[:, :LANES]
            o_ref[tt, pl.ds(half_rows + c, rows, stride=SUBLANES), :] = y[:, LANES:]

    @pl.when(ct == 0)
    def _():
        build_set(0, 0, (g0_ref.at[0], g1_ref.at[0]))

    n_cp = SUBLANES // 2
    for tt in range(HY_TILES_PER_STEP):
        for p in range(n_cp):
            pair = tt * n_cp + p
            if p + 1 < n_cp:
                build_set((pair + 1) % 2, 2 * (p + 1), (g0_ref.at[tt], g1_ref.at[tt]))
            elif tt + 1 < HY_TILES_PER_STEP:
                build_set((pair + 1) % 2, 0, (g0_ref.at[tt + 1], g1_ref.at[tt + 1]))
            else:
                build_set((pair + 1) % 2, 0, (g0n_ref, g1n_ref))
            process_pair(tt, 2 * p, sets[pair % 2])


def _hyena(hy, gm, bsz):
    n_tiles = gm.shape[0] // 2
    rows = N_TBLK * bsz
    prow = 2 * rows * SUBLANES
    hy = hy.reshape(hy.shape[0], prow, LANES)
    tps = HY_TILES_PER_STEP
    gspec = lambda off: pl.BlockSpec((tps, N_WIN * SUBLANES, LANES), lambda i: (i + off // tps, 0, 0))
    gnext = lambda off: pl.BlockSpec((None, N_WIN * SUBLANES, LANES),
                                     lambda i: (jnp.minimum(tps * (i + 1), n_tiles - 1) + off, 0, 0))
    uspec = lambda off: pl.BlockSpec((tps, prow, LANES), lambda i: (i + off // tps, 0, 0))
    strip = pltpu.VMEM((STRIP_ROWS, LANES), BF16)
    return pl.pallas_call(
        functools.partial(_hyena_kernel, bsz),
        grid=(n_tiles // tps,),
        in_specs=[uspec(0), uspec(n_tiles), uspec(2 * n_tiles), gspec(0), gspec(n_tiles),
                  gnext(0), gnext(n_tiles)],
        out_specs=pl.BlockSpec((tps, prow, LANES), lambda i: (i, 0, 0)),
        out_shape=jax.ShapeDtypeStruct((n_tiles, prow, LANES), F32),
        scratch_shapes=[strip] * 8 + [pltpu.VMEM((rows, TBLK), F32)] * 2,
        compiler_params=pltpu.CompilerParams(dimension_semantics=("arbitrary",),
                                             vmem_limit_bytes=VMEM_LIMIT),
        name="hyena",
    )(hy, hy, hy, gm, gm, gm, gm)


ATT_QB = WINDOW
ATT_SPAN = 3 * WINDOW
ATT_TILE = 1024


def _attn_out_kernel(x_ref, yh_ref, gh_ref, q_ref, k_ref, v_ref, ga_ref, sink_ref, hg_ref, ag_ref,
                     woh_ref, woa_ref, o_ref, bias_ref, ya_ref):
    qb, span = ATT_QB, ATT_SPAN
    t = pl.program_id(1)

    @pl.when(jnp.logical_and(pl.program_id(0) == 0, t == 0))
    def _():
        r = lax.broadcasted_iota(jnp.int32, (qb, span), 0)
        c = lax.broadcasted_iota(jnp.int32, (qb, span), 1)
        for j in range(3):
            bias_ref[j] = jnp.where(jnp.abs(c - r - j * WINDOW) <= WINDOW, 0.0, -jnp.inf)

    lane = lax.broadcasted_iota(jnp.int32, (2 * qb, LANES), 1)
    low = lane < HEAD_DIM
    row2 = lax.broadcasted_iota(jnp.int32, (2 * qb, 1), 0)
    sinks = [[jnp.where(row2 < qb, sink_ref[4 * kvh + par], sink_ref[4 * kvh + 2 + par])
              for par in range(2)] for kvh in range(N_KV_HEADS)]

    for i in range(ATT_TILE // qb):
        r0 = i * qb
        q0 = pl.multiple_of(t * ATT_TILE + r0, qb)
        ws = pl.multiple_of(jnp.clip(q0 - WINDOW, 0, SEQ - span), WINDOW)
        bias1 = bias_ref[(q0 - ws) // WINDOW]
        bias = jnp.concatenate([bias1, bias1], axis=0)
        scores = []
        for kvh in range(N_KV_HEADS):
            qs = jnp.concatenate(
                [q_ref[0, r0:r0 + qb, (2 * kvh) * LANES:(2 * kvh + 1) * LANES],
                 q_ref[0, r0:r0 + qb, (2 * kvh + 1) * LANES:(2 * kvh + 2) * LANES]], axis=0)
            zero = jnp.zeros_like(qs)
            for par in range(2):
                qm = jnp.where(low, qs, zero) if par == 0 else jnp.where(low, zero, qs)
                ks = (kvh + par) % 2
                km = k_ref[0, pl.ds(ws, span), ks * LANES:(ks + 1) * LANES]
                scores.append(_nt_dot(qm, km) + bias)
        for kvh in range(N_KV_HEADS):
            parts = []
            for par in range(2):
                s = scores[2 * kvh + par]
                vs = 2 * kvh + par
                vm = v_ref[0, pl.ds(ws, span), vs * LANES:(vs + 1) * LANES]
                sink = sinks[kvh][par]
                m = jnp.maximum(jnp.max(s, axis=-1, keepdims=True), sink)
                p = jnp.exp((s - m).astype(BF16))
                parts.append((jnp.dot(p, vm, preferred_element_type=F32), jnp.exp(sink - m)))
            (o_e, sink_e), (o_o, sink_o) = parts
            num = jnp.where(low, o_e, o_o)
            den = jnp.where(low, pltpu.roll(o_e, HEAD_DIM, 1) + sink_e,
                            pltpu.roll(o_o, HEAD_DIM, 1) + sink_o)
            o = num / den
            ya_ref[r0:r0 + qb, (2 * kvh) * LANES:(2 * kvh + 1) * LANES] = o[:qb]
            ya_ref[r0:r0 + qb, (2 * kvh + 1) * LANES:(2 * kvh + 2) * LANES] = o[qb:]

    def chan_major(ref):
        parts = []
        for kk in range(ATT_TILE // LANES):
            blk, half = kk // 2, kk % 2
            parts.append(ref[:, half, blk, 0, :, :].reshape(D_HYENA, LANES))
        return jnp.concatenate(parts, axis=1)

    yh = chan_major(yh_ref)
    gh = chan_major(gh_ref)
    yh_n = yh * lax.rsqrt(jnp.mean(yh * yh, axis=0, keepdims=True) + EPS) * hg_ref[...]
    yh_g = (yh_n * (gh * jax.nn.sigmoid(gh))).astype(BF16)
    acc = lax.dot_general(yh_g, woh_ref[...], (((0,), (0,)), ((), ())), preferred_element_type=F32)

    ya = ya_ref[...]
    ga = ga_ref[0].astype(F32)
    ya_n = ya * lax.rsqrt(jnp.mean(ya * ya, axis=-1, keepdims=True) + EPS) * ag_ref[...]
    ya_g = (ya_n * (ga * jax.nn.sigmoid(ga))).astype(BF16)
    acc = acc + jnp.dot(ya_g, woa_ref[...], preferred_element_type=F32)
    o_ref[0] = x_ref[0] + acc


def _attn_out(x, yh, hy, q, k2, v4, ga, sink, hy_out_norm_g, attn_out_norm_g, w_out):
    bsz = x.shape[0]
    n_tt = SEQ // ATT_TILE
    yh = yh.reshape(HY_TILES, 2, N_TBLK, bsz, SUBLANES, LANES)
    woh = w_out[:D_HYENA].astype(BF16)
    woa = w_out[D_HYENA:].astype(BF16)
    const = lambda shape: pl.BlockSpec(shape, lambda b, t: (0,) * len(shape))
    tok = lambda width: pl.BlockSpec((1, ATT_TILE, width), lambda b, t: (b, t, 0))
    seq = lambda width: pl.BlockSpec((1, SEQ, width), lambda b, t: (b, 0, 0))
    packed = lambda tile_blk: pl.BlockSpec(
        (HY_TILES, 2, ATT_TILE // TBLK, 1, SUBLANES, LANES), lambda b, t: (tile_blk, 0, t, b, 0, 0))
    return pl.pallas_call(
        _attn_out_kernel,
        grid=(bsz, n_tt),
        in_specs=[
            tok(D_MODEL), packed(0), packed(3), tok(D_ATTN), seq(2 * D_KV), seq(4 * D_KV), tok(D_ATTN),
            pl.BlockSpec(memory_space=pltpu.SMEM),
            const((D_HYENA, 1)), const((1, D_ATTN)), const(woh.shape), const(woa.shape),
        ],
        out_specs=tok(D_MODEL),
        out_shape=jax.ShapeDtypeStruct(x.shape, x.dtype),
        scratch_shapes=[pltpu.VMEM((3, ATT_QB, ATT_SPAN), F32), pltpu.VMEM((ATT_TILE, D_ATTN), F32)],
        compiler_params=pltpu.CompilerParams(dimension_semantics=("arbitrary", "arbitrary"),
                                             vmem_limit_bytes=VMEM_LIMIT),
        name="attn_out",
    )(x, yh, hy, q, k2, v4, ga, sink, hy_out_norm_g[:, None], attn_out_norm_g[None], woh, woa)


def kernel(x, norm_g, w_in, conv_w, conv_b, filt_w1, filt_b1, filt_w2, filt_b2, filt_w3, filt_b3,
           filt_w4, filt_sin_freq, hyena_bias, q_norm_g, k_norm_g, attn_sink, hy_out_norm_g,
           attn_out_norm_g, w_out):
    bsz, seq, d_model = x.shape
    assert seq == SEQ and d_model == D_MODEL and bsz % SUBLANES == 0
    assert norm_g.shape[0] == 1, "one layer"
    gm = _filter_windows(filt_w1[0], filt_b1[0], filt_w2[0], filt_b2[0], filt_w3[0], filt_b3[0],
                         filt_w4[0], filt_sin_freq[0], hyena_bias[0])
    hy, q, k2, v2, ga = _in_proj(x, norm_g[0], w_in[0], conv_w[0], conv_b[0], q_norm_g[0], k_norm_g[0])
    yh = _hyena(hy, gm, bsz)
    return _attn_out(x, yh, hy, q, k2, v2, ga, attn_sink[0], hy_out_norm_g[0], attn_out_norm_g[0],
                     w_out[0])
```
